```python
import jax, jax.numpy as jnp
from jax import lax
import numpy as np

D_MODEL = 1024
BATCH = 1
SEQ = 16384
DEPTH = 1
DEC_BATCH = 16
DEC_SEQ = 2048
PAST_LEN = 128

HEAD_DIM = 64
M_HEADS = 8
M_WIDTH = M_HEADS * HEAD_DIM
A_HEADS = 8
A_KV_HEADS = 2
A_WIDTH = A_HEADS * HEAD_DIM
KV_WIDTH = A_KV_HEADS * HEAD_DIM
MIX_WIDTH = M_WIDTH + A_WIDTH
N_GATES = 4 * M_HEADS
IN_COLS = 4 * M_WIDTH + N_GATES + A_WIDTH + 2 * KV_WIDTH
D_FF = 2816
CONV_W = 3
GRID_W = 64
CHUNK = 64
Q_BLOCK = 128
ROPE_THETA = 10000.0
EPS = 1e-6

kernel_name = "hybrid_mlstm_gqa_convffn_encoder"


def rmsnorm(x, w):
    xf = x.astype(jnp.float32)
    y = xf * lax.rsqrt(jnp.mean(xf * xf, axis=-1, keepdims=True) + EPS)
    return (y * w.astype(jnp.float32)).astype(x.dtype)


def mlstm_scan(q, k, v, ig, fg):
    B, H, T, d = q.shape
    nc = T // CHUNK
    logf = jax.nn.log_sigmoid(fg)

    def to_chunks(a):
        a = a.reshape((B, H, nc, CHUNK) + a.shape[3:])
        return jnp.moveaxis(a, 2, 0)

    qc, kc, vc, ic, fc = (to_chunks(a) for a in (q, k, v, ig, logf))
    causal = jnp.tril(jnp.ones((CHUNK, CHUNK), dtype=bool))

    def step(carry, inp):
        C, n, m = carry
        qb, kb, vb, ib, fb = inp
        b = jnp.cumsum(fb, axis=-1)
        Dm = b[..., :, None] - b[..., None, :] + ib[..., None, :]
        Dm = jnp.where(causal, Dm, -jnp.inf)
        inter = b + m[..., None]
        m_t = jnp.maximum(inter, jnp.max(Dm, axis=-1))
        Dw = jnp.exp(Dm - m_t[..., None])
        iw = jnp.exp(inter - m_t)
        s = jnp.einsum('bhld,bhsd->bhls', qb, kb) * Dw
        num = iw[..., None] * jnp.einsum('bhld,bhde->bhle', qb, C) + jnp.einsum('bhls,bhse->bhle', s, vb)
        den = iw * jnp.einsum('bhld,bhd->bhl', qb, n) + jnp.sum(s, axis=-1)
        h = num / jnp.maximum(jnp.abs(den), jnp.exp(-m_t))[..., None]
        bL = b[..., -1]
        wend = bL[..., None] - b + ib
        m_new = jnp.maximum(bL + m, jnp.max(wend, axis=-1))
        dec = jnp.exp(bL + m - m_new)
        we = jnp.exp(wend - m_new[..., None])
        C_new = dec[..., None, None] * C + jnp.einsum('bhs,bhsd,bhse->bhde', we, kb, vb)
        n_new = dec[..., None] * n + jnp.einsum('bhs,bhsd->bhd', we, kb)
        return (C_new, n_new, m_new), h

    init = (jnp.zeros((B, H, d, d), jnp.float32), jnp.zeros((B, H, d), jnp.float32),
            jnp.zeros((B, H), jnp.float32))
    _, hs = lax.scan(step, init, (qc, kc, vc, ic, fc))
    return jnp.moveaxis(hs, 0, 2).reshape(B, H, T, d)


def mlstm_mixer(mq, mk, mv, mo, gates, b_gates, mh_norm_w):
    B, T, _ = mq.shape

    def heads(a):
        return a.reshape(B, T, M_HEADS, HEAD_DIM).transpose(0, 2, 1, 3).astype(jnp.float32)

    q = heads(mq)
    k = heads(mk) * (HEAD_DIM ** -0.5)
    v = heads(mv)
    g = (gates.astype(jnp.float32) + b_gates.astype(jnp.float32)).transpose(0, 2, 1)
    i_f, i_b, f_f, f_b = jnp.split(g, 4, axis=1)
    h_f = mlstm_scan(q, k, v, i_f, f_f)
    fl = lambda a: jnp.flip(a, axis=2)
    h_b = fl(mlstm_scan(fl(q), fl(k), fl(v), fl(i_b), fl(f_b)))
    h = h_f + h_b
    h = h * lax.rsqrt(jnp.mean(h * h, axis=-1, keepdims=True) + EPS)
    h = h * mh_norm_w.astype(jnp.float32).reshape(M_HEADS, 1, HEAD_DIM)
    h = h.transpose(0, 2, 1, 3).reshape(B, T, M_WIDTH)
    return (jax.nn.sigmoid(mo.astype(jnp.float32)) * h).astype(mq.dtype)


def axial_rope_tables(T):
    rows = T // GRID_W
    row = jnp.repeat(jnp.arange(rows, dtype=jnp.float32), GRID_W)
    col = jnp.tile(jnp.arange(GRID_W, dtype=jnp.float32), rows)
    nf = HEAD_DIM // 4
    inv = ROPE_THETA ** (-jnp.arange(nf, dtype=jnp.float32) / nf)
    ar = row[:, None] * inv
    ac = col[:, None] * inv
    ang = jnp.concatenate([ar, ar, ac, ac], axis=-1)
    return jnp.cos(ang), jnp.sin(ang)


def apply_rope(x, cos, sin):
    x1r, x2r, x1c, x2c = jnp.split(x, 4, axis=-1)
    rot = jnp.concatenate([-x2r, x1r, -x2c, x1c], axis=-1)
    return (x.astype(jnp.float32) * cos + rot.astype(jnp.float32) * sin).astype(x.dtype)


def attention_mixer(aq, ak, av, q_norm_w, k_norm_w):
    B, T, _ = aq.shape
    G = A_HEADS // A_KV_HEADS
    q = rmsnorm(aq.reshape(B, T, A_KV_HEADS, G, HEAD_DIM), q_norm_w)
    k = rmsnorm(ak.reshape(B, T, A_KV_HEADS, HEAD_DIM), k_norm_w)
    v = av.reshape(B, T, A_KV_HEADS, HEAD_DIM)
    cos, sin = axial_rope_tables(T)
    q = apply_rope(q, cos[None, :, None, None], sin[None, :, None, None])
    k = apply_rope(k, cos[None, :, None], sin[None, :, None])
    nb = T // Q_BLOCK
    qb = q.reshape(B, nb, Q_BLOCK, A_KV_HEADS, G, HEAD_DIM).transpose(1, 0, 2, 3, 4, 5)
    scale = HEAD_DIM ** -0.5

    def block(qi):
        s = jnp.einsum('bqhgd,bkhd->bhgqk', qi, k).astype(jnp.float32) * scale
        p = jax.nn.softmax(s, axis=-1)
        return jnp.einsum('bhgqk,bkhd->bqhgd', p.astype(v.dtype), v)

    o = lax.map(block, qb)
    return o.transpose(1, 0, 2, 3, 4, 5).reshape(B, T, A_WIDTH)


def centred_dwconv(u, w, b):
    up = jnp.pad(u, ((0, 0), (1, 1), (0, 0)))
    return up[:, :-2] * w[0] + up[:, 1:-1] * w[1] + up[:, 2:] * w[2] + b


def encoder_layer(x, w_in, b_gates, mh_norm_w, q_norm_w, k_norm_w, w_out, norm1_w, norm2_w,
                  w_up, conv_w, conv_b, w_down):
    h = rmsnorm(x, norm1_w)
    p = h @ w_in
    cuts = np.cumsum([M_WIDTH, M_WIDTH, M_WIDTH, M_WIDTH, N_GATES, A_WIDTH, KV_WIDTH]).tolist()
    mq, mk, mv, mo, gates, aq, ak, av = jnp.split(p, cuts, axis=-1)
    m_out = mlstm_mixer(mq, mk, mv, mo, gates, b_gates, mh_norm_w)
    a_out = attention_mixer(aq, ak, av, q_norm_w, k_norm_w)
    x = x + jnp.concatenate([m_out, a_out], axis=-1) @ w_out
    h = rmsnorm(x, norm2_w)
    u = centred_dwconv(h @ w_up, conv_w, conv_b)
    a, g = jnp.split(u, 2, axis=-1)
    return x + (jax.nn.silu(g) * a) @ w_down


def run_trunk(x, w_in, b_gates, mh_norm_w, q_norm_w, k_norm_w, w_out, norm1_w, norm2_w,
              w_up, conv_w, conv_b, w_down, final_norm_w):
    for l in range(DEPTH):
        x = encoder_layer(x, w_in[l], b_gates[l], mh_norm_w[l], q_norm_w[l], k_norm_w[l], w_out[l],
                          norm1_w[l], norm2_w[l], w_up[l], conv_w[l], conv_b[l], w_down[l])
    return rmsnorm(x, final_norm_w)


def setup_inputs(seed: int = 0) -> dict:
    key = jax.random.key(seed)
    ks = jax.random.split(key, 20)
    nrm = jax.random.normal
    f32 = jnp.float32
    forget_bias = jnp.tile(jnp.linspace(3.0, 6.0, M_HEADS, dtype=f32), 2)
    b_gates = jnp.concatenate([
        0.1 * nrm(ks[0], (DEPTH, 2 * M_HEADS), f32),
        forget_bias[None] + 0.1 * nrm(ks[1], (DEPTH, 2 * M_HEADS), f32),
    ], axis=-1)
    conv_center = jnp.array([0.0, 1.0, 0.0], f32).reshape(1, 3, 1)
    return {
        "x_prompt": nrm(ks[2], (BATCH, SEQ, D_MODEL), f32),
        "x_sample": nrm(ks[3], (DEC_BATCH, DEC_SEQ, D_MODEL), f32),
        "w_in": nrm(ks[4], (DEPTH, D_MODEL, IN_COLS), f32) * D_MODEL ** -0.5,
        "b_gates": b_gates,
        "mh_norm_w": 1.0 + 0.05 * nrm(ks[5], (DEPTH, M_WIDTH), f32),
        "q_norm_w": 1.0 + 0.05 * nrm(ks[6], (DEPTH, HEAD_DIM), f32),
        "k_norm_w": 1.0 + 0.05 * nrm(ks[7], (DEPTH, HEAD_DIM), f32),
        "w_out": nrm(ks[8], (DEPTH, MIX_WIDTH, D_MODEL), f32) * MIX_WIDTH ** -0.5,
        "norm1_w": 1.0 + 0.05 * nrm(ks[9], (DEPTH, D_MODEL), f32),
        "norm2_w": 1.0 + 0.05 * nrm(ks[10], (DEPTH, D_MODEL), f32),
        "w_up": nrm(ks[11], (DEPTH, D_MODEL, 2 * D_FF), f32) * D_MODEL ** -0.5,
        "conv_w": conv_center + 0.3 * nrm(ks[12], (DEPTH, CONV_W, 2 * D_FF), f32),
        "conv_b": 0.01 * nrm(ks[13], (DEPTH, 2 * D_FF), f32),
        "w_down": nrm(ks[14], (DEPTH, D_FF, D_MODEL), f32) * D_FF ** -0.5,
        "final_norm_w": 1.0 + 0.05 * nrm(ks[15], (D_MODEL,), f32),
    }


def reference(x_prompt, x_sample, w_in, b_gates, mh_norm_w, q_norm_w, k_norm_w, w_out, norm1_w, norm2_w,
              w_up, conv_w, conv_b, w_down, final_norm_w):
    y_prompt = run_trunk(x_prompt, w_in, b_gates, mh_norm_w, q_norm_w, k_norm_w, w_out, norm1_w, norm2_w,
                         w_up, conv_w, conv_b, w_down, final_norm_w)
    y_sample = run_trunk(x_sample, w_in, b_gates, mh_norm_w, q_norm_w, k_norm_w, w_out, norm1_w, norm2_w,
                         w_up, conv_w, conv_b, w_down, final_norm_w)
    return (y_prompt, y_sample)
```

```python
import functools
import math

import jax
import jax.numpy as jnp
import numpy as np
from jax import lax
from jax.experimental import pallas as pl
from jax.experimental.pallas import tpu as pltpu

F32 = jnp.float32
BF16 = jnp.bfloat16

D_MODEL = 1024
HEAD_DIM = 64
M_HEADS = 8
M_WIDTH = M_HEADS * HEAD_DIM
A_HEADS = 8
A_KV_HEADS = 2
A_GROUP = A_HEADS // A_KV_HEADS
A_WIDTH = A_HEADS * HEAD_DIM
KV_WIDTH = A_KV_HEADS * HEAD_DIM
N_GATES = 4 * M_HEADS
D_FF = 2816
GRID_W = 64
ROPE_THETA = 10000.0
EPS = 1e-6

LANES = 128
VMEM_LIMIT = 56 * 1024 * 1024

MLSTM_CHUNK = 128
FF_CHUNK = 256
CONV_HALO = 16
NEG = -1e30


def _cparams(sem):
    return pltpu.CompilerParams(dimension_semantics=sem, vmem_limit_bytes=VMEM_LIMIT)


def _const_spec(shape):
    nd = len(shape)
    return pl.BlockSpec(shape, lambda *_: (0,) * nd, pipeline_mode=pl.Buffered(1))


def _inproj_kernel(x_ref, n1_ref, wm_ref, wg_ref, wa_ref, bg_ref, qw_ref, kw_ref, cos_ref, sa_ref, sb_ref,
                   ones_ref, mq_ref, mk_ref, mv_ref, mo_ref, g_ref, qt_ref, k_ref, vt_ref):
    x = x_ref[...]
    h = x * lax.rsqrt(jnp.mean(x * x, axis=-1, keepdims=True) + EPS) * n1_ref[...]
    h = h.astype(BF16)

    pm = jnp.dot(h, wm_ref[...], preferred_element_type=F32)
    mq_ref[...] = pm[:, 0:M_WIDTH].astype(BF16)
    mk_ref[...] = (pm[:, M_WIDTH:2 * M_WIDTH] * (HEAD_DIM ** -0.5)).astype(BF16)
    mv_ref[...] = pm[:, 2 * M_WIDTH:3 * M_WIDTH].astype(BF16)
    mo_ref[...] = pm[:, 3 * M_WIDTH:4 * M_WIDTH].astype(BF16)

    g_ref[...] = jnp.dot(h, wg_ref[...], preferred_element_type=F32) + bg_ref[...]

    pa = jnp.dot(h, wa_ref[...], preferred_element_type=F32)
    cos, sa, sb = cos_ref[...], sa_ref[...], sb_ref[...]
    ones_bd = ones_ref[...]

    def norm_rope(y, w):
        sq = y * y
        hi = sq.astype(BF16)
        lo = (sq - hi.astype(F32)).astype(BF16)
        ss = (jnp.dot(hi, ones_bd, preferred_element_type=F32)
              + jnp.dot(lo, ones_bd, preferred_element_type=F32))
        yn = y * lax.rsqrt(ss * (1.0 / HEAD_DIM) + EPS) * w
        up = pltpu.roll(yn, LANES - HEAD_DIM // 4, axis=1)
        dn = pltpu.roll(yn, HEAD_DIM // 4, axis=1)
        return yn * cos + up * sa + dn * sb

    qscale = (HEAD_DIM ** -0.5) * math.log2(math.e)
    for c in range(A_WIDTH // LANES):
        q2 = norm_rope(pa[:, c * LANES:(c + 1) * LANES], qw_ref[...]) * qscale
        q2t = q2.T.astype(BF16)
        qt_ref[2 * c] = q2t[0:HEAD_DIM]
        qt_ref[2 * c + 1] = q2t[HEAD_DIM:2 * HEAD_DIM]
    k2 = norm_rope(pa[:, A_WIDTH:A_WIDTH + KV_WIDTH], kw_ref[...]).astype(BF16)
    k_ref[0] = k2[:, 0:HEAD_DIM]
    k_ref[1] = k2[:, HEAD_DIM:2 * HEAD_DIM]
    v2t = pa[:, A_WIDTH + KV_WIDTH:A_WIDTH + 2 * KV_WIDTH].T.astype(BF16)
    vt_ref[0] = v2t[0:HEAD_DIM]
    vt_ref[1] = v2t[HEAD_DIM:2 * HEAD_DIM]


def _inproj(x2, T, n1, wm, wg, wa, bg, qw2, kw2, cos, sa, sb, ones_bd, tm):
    N = x2.shape[0]
    tps = T // tm
    row = lambda i: (i, 0)
    pos = lambda i: (i % tps, 0)
    out_shape = (
        jax.ShapeDtypeStruct((N, M_WIDTH), BF16), jax.ShapeDtypeStruct((N, M_WIDTH), BF16),
        jax.ShapeDtypeStruct((N, M_WIDTH), BF16), jax.ShapeDtypeStruct((N, M_WIDTH), BF16),
        jax.ShapeDtypeStruct((N, LANES), F32),
        jax.ShapeDtypeStruct((A_HEADS, HEAD_DIM, N), BF16),
        jax.ShapeDtypeStruct((A_KV_HEADS, N, HEAD_DIM), BF16),
        jax.ShapeDtypeStruct((A_KV_HEADS, HEAD_DIM, N), BF16),
    )
    return pl.pallas_call(
        _inproj_kernel,
        grid=(N // tm,),
        in_specs=[
            pl.BlockSpec((tm, D_MODEL), row),
            _const_spec(n1.shape), _const_spec(wm.shape), _const_spec(wg.shape), _const_spec(wa.shape),
            _const_spec(bg.shape), _const_spec(qw2.shape), _const_spec(kw2.shape),
            pl.BlockSpec((tm, LANES), pos), pl.BlockSpec((tm, LANES), pos), pl.BlockSpec((tm, LANES), pos),
            _const_spec(ones_bd.shape),
        ],
        out_specs=(
            pl.BlockSpec((tm, M_WIDTH), row), pl.BlockSpec((tm, M_WIDTH), row),
            pl.BlockSpec((tm, M_WIDTH), row), pl.BlockSpec((tm, M_WIDTH), row),
            pl.BlockSpec((tm, LANES), row),
            pl.BlockSpec((A_HEADS, HEAD_DIM, tm), lambda i: (0, 0, i)),
            pl.BlockSpec((A_KV_HEADS, tm, HEAD_DIM), lambda i: (0, i, 0)),
            pl.BlockSpec((A_KV_HEADS, HEAD_DIM, tm), lambda i: (0, 0, i)),
        ),
        out_shape=out_shape,
        compiler_params=_cparams(("parallel",)),
        name="inproj",
    )(x2, n1, wm, wg, wa, bg, qw2, kw2, cos, sa, sb, ones_bd)


def _log_sigmoid(x):
    return jnp.minimum(x, 0.0) - jnp.log1p(jnp.exp(-jnp.abs(x)))


def _split3(x):
    a = x.astype(BF16)
    r = x - a.astype(F32)
    b = r.astype(BF16)
    c = (r - b.astype(F32)).astype(BF16)
    return a, b, c


def _mlstm_kernel(*refs, reverse, final):
    if final:
        (q_ref, k_ref, v_ref, g_ref, tri_ref, hb_ref, mo_ref, nw_ref, out_ref, c_ref, m_ref) = refs
    else:
        (q_ref, k_ref, v_ref, g_ref, tri_ref, out_ref, c_ref, m_ref) = refs
    L = q_ref.shape[0]
    d = 1 if reverse else 0
    gate_lane0 = 2 * M_HEADS + d * M_HEADS

    @pl.when(pl.program_id(1) == 0)
    def _():
        c_ref[...] = jnp.zeros_like(c_ref)
        m_ref[...] = jnp.zeros_like(m_ref)

    g = g_ref[...]
    logf = _log_sigmoid(g)
    a, b, c = _split3(logf)
    tri = tri_ref[...]
    cum = (jnp.dot(tri, a, preferred_element_type=F32) + jnp.dot(tri, b, preferred_element_type=F32)
           + jnp.dot(tri, c, preferred_element_type=F32))
    tot = jnp.sum(logf, axis=0, keepdims=True)
    ial = pltpu.roll(g, 2 * M_HEADS, axis=1)
    e = ial - cum
    et = e.T
    m_prev = m_ref[...]
    inter = cum + m_prev
    wend = tot + e
    m_new = jnp.maximum(tot + m_prev, jnp.max(wend, axis=0, keepdims=True))
    dec = jnp.exp(tot + m_prev - m_new)
    we = jnp.exp(wend - m_new)
    m_ref[...] = m_new

    li = lax.broadcasted_iota(jnp.int32, (L, L), 0)
    si = lax.broadcasted_iota(jnp.int32, (L, L), 1)
    keep = (si >= li) if reverse else (si <= li)
    lane = lax.broadcasted_iota(jnp.int32, (L, LANES), 1)
    ones_col = (lane == HEAD_DIM).astype(F32)

    for hd in range(M_HEADS):
        gl = gate_lane0 + hd
        sl = slice(hd * HEAD_DIM, (hd + 1) * HEAD_DIM)
        q = q_ref[:, sl]
        k = k_ref[:, sl]
        v = v_ref[:, sl]
        vext = jnp.concatenate([v.astype(F32), jnp.zeros((L, LANES - HEAD_DIM), F32)], axis=1) + ones_col
        dm = jnp.where(keep, cum[:, gl:gl + 1] + et[gl:gl + 1, :], NEG)
        inter_h = inter[:, gl:gl + 1]
        m_t = jnp.maximum(inter_h, jnp.max(dm, axis=1, keepdims=True))
        dw = jnp.exp(dm - m_t)
        iw = jnp.exp(inter_h - m_t)
        s = lax.dot_general(q, k, (((1,), (1,)), ((), ())), preferred_element_type=F32) * dw
        cext = c_ref[hd]
        nd = (iw * jnp.dot(q, cext.astype(BF16), preferred_element_type=F32)
              + jnp.dot(s.astype(BF16), vext.astype(BF16), preferred_element_type=F32))
        num = nd[:, 0:HEAD_DIM]
        den = nd[:, HEAD_DIM:HEAD_DIM + 1]
        hh = num / jnp.maximum(jnp.abs(den), jnp.exp(-m_t))
        wv = (we[:, gl:gl + 1] * vext).astype(BF16)
        c_ref[hd] = dec[:, gl:gl + 1] * cext + lax.dot_general(
            k, wv, (((0,), (0,)), ((), ())), preferred_element_type=F32)
        if final:
            hs = hh + hb_ref[:, sl]
            hn = hs * lax.rsqrt(jnp.mean(hs * hs, axis=-1, keepdims=True) + EPS) * nw_ref[:, sl]
            mo = mo_ref[:, sl].astype(F32)
            out_ref[:, sl] = (hn / (1.0 + jnp.exp(-mo))).astype(out_ref.dtype)
        else:
            out_ref[:, sl] = hh


def _mlstm(mq, mk, mv, gates, T, tri, hb=None, mo=None, nw=None, *, reverse):
    N = mq.shape[0]
    L = MLSTM_CHUNK
    nc = T // L
    B = N // T
    final = hb is not None
    if reverse:
        row = lambda b, c: (b * nc + (nc - 1 - c), 0)
    else:
        row = lambda b, c: (b * nc + c, 0)
    tok = pl.BlockSpec((L, M_WIDTH), row)
    in_specs = [tok, tok, tok, pl.BlockSpec((L, LANES), row), _const_spec(tri.shape)]
    args = [mq, mk, mv, gates, tri]
    if final:
        in_specs += [tok, tok, _const_spec(nw.shape)]
        args += [hb, mo, nw]
    return pl.pallas_call(
        functools.partial(_mlstm_kernel, reverse=reverse, final=final),
        grid=(B, nc),
        in_specs=in_specs,
        out_specs=tok,
        out_shape=jax.ShapeDtypeStruct((N, M_WIDTH), BF16 if final else F32),
        scratch_shapes=[pltpu.VMEM((M_HEADS, HEAD_DIM, LANES), F32), pltpu.VMEM((1, LANES), F32)],
        compiler_params=_cparams(("parallel", "arbitrary")),
        name="mlstm_fwd" if final else "mlstm_bwd",
    )(*args)


def _attn_kernel(qt_ref, k_ref, vt_ref, o_ref, m_ref, l_ref, acc_ref):
    ki = pl.program_id(3)

    @pl.when(ki == 0)
    def _():
        m_ref[...] = jnp.full_like(m_ref, -jnp.inf)
        l_ref[...] = jnp.zeros_like(l_ref)
        acc_ref[...] = jnp.zeros_like(acc_ref)

    k = k_ref[0]
    vt = vt_ref[0]
    for j in range(A_GROUP):
        st = jnp.dot(k, qt_ref[j], preferred_element_type=F32)
        m_old = m_ref[j]
        m_new = jnp.maximum(m_old, jnp.max(st, axis=0, keepdims=True))
        alpha = jnp.exp2(m_old - m_new)
        p = jnp.exp2(st - m_new)
        l_ref[j] = alpha * l_ref[j] + jnp.sum(p, axis=0, keepdims=True)
        acc_ref[j] = alpha * acc_ref[j] + jnp.dot(vt, p.astype(BF16), preferred_element_type=F32)
        m_ref[j] = m_new

    @pl.when(ki == pl.num_programs(3) - 1)
    def _():
        for jj in range(A_GROUP // 2):
            o2 = jnp.concatenate([acc_ref[2 * jj] / l_ref[2 * jj],
                                  acc_ref[2 * jj + 1] / l_ref[2 * jj + 1]], axis=0)
            o_ref[:, jj * LANES:(jj + 1) * LANES] = o2.T.astype(o_ref.dtype)


def _attn(qt, k, vt, T, tq, tk):
    N = qt.shape[2]
    B = N // T
    nq, nk = T // tq, T // tk
    return pl.pallas_call(
        _attn_kernel,
        grid=(B, A_KV_HEADS, nq, nk),
        in_specs=[
            pl.BlockSpec((A_GROUP, HEAD_DIM, tq), lambda b, g, qi, ki: (g, 0, b * nq + qi)),
            pl.BlockSpec((1, tk, HEAD_DIM), lambda b, g, qi, ki: (g, b * nk + ki, 0)),
            pl.BlockSpec((1, HEAD_DIM, tk), lambda b, g, qi, ki: (g, 0, b * nk + ki)),
        ],
        out_specs=pl.BlockSpec((tq, A_GROUP * HEAD_DIM), lambda b, g, qi, ki: (b * nq + qi, g)),
        out_shape=jax.ShapeDtypeStruct((N, A_WIDTH), BF16),
        scratch_shapes=[pltpu.VMEM((A_GROUP, 1, tq), F32), pltpu.VMEM((A_GROUP, 1, tq), F32),
                        pltpu.VMEM((A_GROUP, HEAD_DIM, tq), F32)],
        compiler_params=_cparams(("parallel", "parallel", "parallel", "arbitrary")),
        name="attn",
    )(qt, k, vt)


def _outproj_kernel(x_ref, m_ref, a_ref, wom_ref, woa_ref, n2_ref, x1_ref, h2_ref):
    x1 = (x_ref[...] + jnp.dot(m_ref[...], wom_ref[...], preferred_element_type=F32)
          + jnp.dot(a_ref[...], woa_ref[...], preferred_element_type=F32))
    x1_ref[...] = x1
    h2 = x1 * lax.rsqrt(jnp.mean(x1 * x1, axis=-1, keepdims=True) + EPS) * n2_ref[...]
    h2_ref[...] = h2.astype(BF16)


def _outproj(x2, m_out, a_out, wom, woa, n2, tm):
    N = x2.shape[0]
    row = lambda i: (i, 0)
    return pl.pallas_call(
        _outproj_kernel,
        grid=(N // tm,),
        in_specs=[pl.BlockSpec((tm, D_MODEL), row), pl.BlockSpec((tm, M_WIDTH), row),
                  pl.BlockSpec((tm, A_WIDTH), row), _const_spec(wom.shape), _const_spec(woa.shape),
                  _const_spec(n2.shape)],
        out_specs=(pl.BlockSpec((tm, D_MODEL), row), pl.BlockSpec((tm, D_MODEL), row)),
        out_shape=(jax.ShapeDtypeStruct((N, D_MODEL), F32), jax.ShapeDtypeStruct((N, D_MODEL), BF16)),
        compiler_params=_cparams(("parallel",)),
        name="outproj",
    )(x2, m_out, a_out, wom, woa, n2)


def _ffn_kernel(hp_ref, hm_ref, hn_ref, x1_ref, wup_ref, cw_ref, cb_ref, wdn_ref, fw_ref, y_ref,
                lhs_ref, u_ref, acc_ref, *, tiles_per_seq):
    i = pl.program_id(0)
    tm = hm_ref.shape[0]
    H = CONV_HALO
    t = i % tiles_per_seq
    lhs_ref[0:H] = jnp.where(t == 0, jnp.zeros_like(hp_ref), hp_ref[...])
    lhs_ref[H:H + tm] = hm_ref[...]
    lhs_ref[H + tm:H + tm + H] = jnp.where(t == tiles_per_seq - 1, jnp.zeros_like(hn_ref), hn_ref[...])
    acc_ref[...] = jnp.zeros_like(acc_ref)

    def body(c, carry):
        u_ref[...] = jnp.dot(lhs_ref[...], wup_ref[c], preferred_element_type=F32)
        w = cw_ref[c]
        conv = (u_ref[pl.ds(H - 1, tm), :] * w[0:1] + u_ref[pl.ds(H, tm), :] * w[1:2]
                + u_ref[pl.ds(H + 1, tm), :] * w[2:3] + cb_ref[c])
        a = conv[:, 0:FF_CHUNK]
        gt = conv[:, FF_CHUNK:2 * FF_CHUNK]
        act = (gt / (1.0 + jnp.exp(-gt))) * a
        acc_ref[...] += jnp.dot(act.astype(BF16), wdn_ref[c], preferred_element_type=F32)
        return carry

    lax.fori_loop(0, D_FF // FF_CHUNK, body, 0)
    x2 = x1_ref[...] + acc_ref[...]
    y_ref[...] = x2 * lax.rsqrt(jnp.mean(x2 * x2, axis=-1, keepdims=True) + EPS) * fw_ref[...]


def _ffn(h2, x1, T, wup, cw, cb, wdn, fw, tm):
    N = h2.shape[0]
    H = CONV_HALO
    tps = T // tm
    r = tm // H
    nblk = N // H
    row = lambda i: (i, 0)
    return pl.pallas_call(
        functools.partial(_ffn_kernel, tiles_per_seq=tps),
        grid=(N // tm,),
        in_specs=[
            pl.BlockSpec((H, D_MODEL), lambda i: (jnp.maximum(i * r - 1, 0), 0)),
            pl.BlockSpec((tm, D_MODEL), row),
            pl.BlockSpec((H, D_MODEL), lambda i: (jnp.minimum((i + 1) * r, nblk - 1), 0)),
            pl.BlockSpec((tm, D_MODEL), row),
            _const_spec(wup.shape), _const_spec(cw.shape), _const_spec(cb.shape), _const_spec(wdn.shape),
            _const_spec(fw.shape),
        ],
        out_specs=pl.BlockSpec((tm, D_MODEL), row),
        out_shape=jax.ShapeDtypeStruct((N, D_MODEL), F32),
        scratch_shapes=[pltpu.VMEM((tm + 2 * H, D_MODEL), BF16), pltpu.VMEM((tm + 2 * H, 2 * FF_CHUNK), F32),
                        pltpu.VMEM((tm, D_MODEL), F32)],
        compiler_params=_cparams(("parallel",)),
        name="ffn",
    )(h2, h2, h2, x1, wup, cw, cb, wdn, fw)


def _rope_tables(T):
    rows = T // GRID_W
    row = jnp.repeat(jnp.arange(rows, dtype=F32), GRID_W)
    col = jnp.tile(jnp.arange(GRID_W, dtype=F32), rows)
    nf = HEAD_DIM // 4
    inv = ROPE_THETA ** (-jnp.arange(nf, dtype=F32) / nf)
    ar = row[:, None] * inv
    ac = col[:, None] * inv
    ang = jnp.concatenate([ar, ar, ac, ac], axis=-1)
    cos, sin = jnp.cos(ang), jnp.sin(ang)
    quarter = (np.arange(HEAD_DIM) // nf) % 2
    sa = jnp.where(quarter == 0, -sin, 0.0)
    sb = jnp.where(quarter == 1, sin, 0.0)
    two = lambda a: jnp.concatenate([a, a], axis=-1)
    return two(cos), two(sa), two(sb)


def _prep_weights(w_in, b_gates, mh_norm_w, q_norm_w, k_norm_w, w_out, norm1_w, norm2_w, w_up, conv_w, conv_b,
                  w_down, final_norm_w):
    gate0 = 4 * M_WIDTH
    wm = w_in[:, :gate0].astype(BF16)
    wg = jnp.pad(w_in[:, gate0:gate0 + N_GATES], ((0, 0), (0, LANES - N_GATES))).astype(BF16)
    wa = w_in[:, gate0 + N_GATES:].astype(BF16)
    bg = jnp.pad(b_gates, (0, LANES - N_GATES)).reshape(1, LANES)
    nchunk = D_FF // FF_CHUNK
    wua = w_up[:, :D_FF].reshape(D_MODEL, nchunk, FF_CHUNK)
    wug = w_up[:, D_FF:].reshape(D_MODEL, nchunk, FF_CHUNK)
    wup = jnp.concatenate([wua, wug], axis=-1).transpose(1, 0, 2).astype(BF16)
    cwa = conv_w[:, :D_FF].reshape(3, nchunk, FF_CHUNK)
    cwg = conv_w[:, D_FF:].reshape(3, nchunk, FF_CHUNK)
    cw = jnp.concatenate([cwa, cwg], axis=-1).transpose(1, 0, 2)
    cb = jnp.concatenate([conv_b[:D_FF].reshape(nchunk, 1, FF_CHUNK),
                          conv_b[D_FF:].reshape(nchunk, 1, FF_CHUNK)], axis=-1)
    wdn = w_down.reshape(nchunk, FF_CHUNK, D_MODEL).astype(BF16)
    return dict(
        wm=wm, wg=wg, wa=wa, bg=bg,
        n1=norm1_w.reshape(1, D_MODEL), n2=norm2_w.reshape(1, D_MODEL), fw=final_norm_w.reshape(1, D_MODEL),
        qw2=jnp.tile(q_norm_w, 2).reshape(1, LANES), kw2=jnp.tile(k_norm_w, 2).reshape(1, LANES),
        nw=mh_norm_w.reshape(1, M_WIDTH),
        wom=w_out[:M_WIDTH].astype(BF16), woa=w_out[M_WIDTH:].astype(BF16),
        wup=wup, cw=cw, cb=cb, wdn=wdn,
    )


def _tiles(T):
    tm = min(512, T)
    tq = min(512, T)
    tk = min(1024, T)
    return tm, tq, tk


def _trunk(x, p):
    B, T, _ = x.shape
    N = B * T
    tm, tq, tk = _tiles(T)
    x2 = x.reshape(N, D_MODEL)
    cos, sa, sb = _rope_tables(T)
    blk = np.arange(LANES) // HEAD_DIM
    ones_bd = jnp.asarray(blk[:, None] == blk[None, :], BF16)
    idx = np.arange(MLSTM_CHUNK)
    tri_f = jnp.asarray(idx[None, :] <= idx[:, None], BF16)
    tri_b = jnp.asarray(idx[None, :] >= idx[:, None], BF16)

    mq, mk, mv, mo, gates, qt, k, vt = _inproj(x2, T, p["n1"], p["wm"], p["wg"], p["wa"], p["bg"], p["qw2"],
                                               p["kw2"], cos, sa, sb, ones_bd, tm)
    hb = _mlstm(mq, mk, mv, gates, T, tri_b, reverse=True)
    m_out = _mlstm(mq, mk, mv, gates, T, tri_f, hb, mo, p["nw"], reverse=False)
    a_out = _attn(qt, k, vt, T, tq, tk)
    x1, h2 = _outproj(x2, m_out, a_out, p["wom"], p["woa"], p["n2"], tm)
    y = _ffn(h2, x1, T, p["wup"], p["cw"], p["cb"], p["wdn"], p["fw"], tm)
    return y.reshape(B, T, D_MODEL)


def kernel(x_prompt, x_sample, w_in, b_gates, mh_norm_w, q_norm_w, k_norm_w, w_out, norm1_w, norm2_w, w_up,
           conv_w, conv_b, w_down, final_norm_w):
    assert w_in.shape[0] == 1, "single-layer trunk"
    p = _prep_weights(w_in[0], b_gates[0], mh_norm_w[0], q_norm_w[0], k_norm_w[0], w_out[0], norm1_w[0],
                      norm2_w[0], w_up[0], conv_w[0], conv_b[0], w_down[0], final_norm_w)
    return (_trunk(x_prompt, p), _trunk(x_sample, p))
```

```python
import functools
import math

import jax
import jax.numpy as jnp
import numpy as np
from jax import lax
from jax.experimental import pallas as pl
from jax.experimental.pallas import tpu as pltpu

F32 = jnp.float32
BF16 = jnp.bfloat16

D_MODEL = 1024
HEAD_DIM = 64
M_HEADS = 8
M_WIDTH = M_HEADS * HEAD_DIM
A_HEADS = 8
A_KV_HEADS = 2
A_GROUP = A_HEADS // A_KV_HEADS
A_WIDTH = A_HEADS * HEAD_DIM
KV_WIDTH = A_KV_HEADS * HEAD_DIM
N_GATES = 4 * M_HEADS
D_FF = 2816
GRID_W = 64
ROPE_THETA = 10000.0
EPS = 1e-6

LANES = 128
VMEM_LIMIT = 56 * 1024 * 1024

MLSTM_CHUNK = 128
FF_CHUNK = 256
CONV_HALO = 16
NEG = -1e30
ATTN_QW = 256
ATTN_VPAD = 16


def _cparams(sem):
    return pltpu.CompilerParams(dimension_semantics=sem, vmem_limit_bytes=VMEM_LIMIT)


def _const_spec(shape):
    nd = len(shape)
    return pl.BlockSpec(shape, lambda *_: (0,) * nd, pipeline_mode=pl.Buffered(1))


def _inproj_kernel(x_ref, n1_ref, wm_ref, wg_ref, wa_ref, bg_ref, qw_ref, kw_ref, cos_ref, sa_ref, sb_ref,
                   ones_ref, mq_ref, mk_ref, mv_ref, mo_ref, g_ref, qt_ref, k_ref, vt_ref):
    x = x_ref[...]
    h = x * lax.rsqrt(jnp.mean(x * x, axis=-1, keepdims=True) + EPS) * n1_ref[...]
    h = h.astype(BF16)

    pm = jnp.dot(h, wm_ref[...], preferred_element_type=F32)
    mq_ref[...] = pm[:, 0:M_WIDTH].astype(BF16)
    mk_ref[...] = (pm[:, M_WIDTH:2 * M_WIDTH] * (HEAD_DIM ** -0.5)).astype(BF16)
    mv_ref[...] = pm[:, 2 * M_WIDTH:3 * M_WIDTH].astype(BF16)
    mo_ref[...] = pm[:, 3 * M_WIDTH:4 * M_WIDTH].astype(BF16)

    g_ref[...] = jnp.dot(h, wg_ref[...], preferred_element_type=F32) + bg_ref[...]

    pa = jnp.dot(h, wa_ref[...], preferred_element_type=F32)
    cos, sa, sb = cos_ref[...], sa_ref[...], sb_ref[...]
    ones_bd = ones_ref[...]

    def norm_rope(y, w):
        sq = y * y
        hi = sq.astype(BF16)
        lo = (sq - hi.astype(F32)).astype(BF16)
        ss = (jnp.dot(hi, ones_bd, preferred_element_type=F32)
              + jnp.dot(lo, ones_bd, preferred_element_type=F32))
        yn = y * lax.rsqrt(ss * (1.0 / HEAD_DIM) + EPS) * w
        up = pltpu.roll(yn, LANES - HEAD_DIM // 4, axis=1)
        dn = pltpu.roll(yn, HEAD_DIM // 4, axis=1)
        return yn * cos + up * sa + dn * sb

    qscale = (HEAD_DIM ** -0.5) * math.log2(math.e)
    for c in range(A_WIDTH // LANES):
        q2 = norm_rope(pa[:, c * LANES:(c + 1) * LANES], qw_ref[...]) * qscale
        q2t = q2.T.astype(BF16)
        qt_ref[2 * c, 0] = q2t[0:HEAD_DIM]
        qt_ref[2 * c + 1, 0] = q2t[HEAD_DIM:2 * HEAD_DIM]
    k2 = norm_rope(pa[:, A_WIDTH:A_WIDTH + KV_WIDTH], kw_ref[...]).astype(BF16)
    k_ref[0, 0] = k2[:, 0:HEAD_DIM]
    k_ref[1, 0] = k2[:, HEAD_DIM:2 * HEAD_DIM]
    v2t = pa[:, A_WIDTH + KV_WIDTH:A_WIDTH + 2 * KV_WIDTH].T.astype(BF16)
    vt_ref[0, 0] = v2t[0:HEAD_DIM]
    vt_ref[1, 0] = v2t[HEAD_DIM:2 * HEAD_DIM]


def _inproj(x2, T, n1, wm, wg, wa, bg, qw2, kw2, cos, sa, sb, ones_bd, tm):
    N = x2.shape[0]
    tps = T // tm
    row = lambda i: (i, 0)
    pos = lambda i: (i % tps, 0)
    out_shape = (
        jax.ShapeDtypeStruct((N, M_WIDTH), BF16), jax.ShapeDtypeStruct((N, M_WIDTH), BF16),
        jax.ShapeDtypeStruct((N, M_WIDTH), BF16), jax.ShapeDtypeStruct((N, M_WIDTH), BF16),
        jax.ShapeDtypeStruct((N, LANES), F32),
        jax.ShapeDtypeStruct((A_HEADS, N // tm, HEAD_DIM, tm), BF16),
        jax.ShapeDtypeStruct((A_KV_HEADS, N // tm, tm, HEAD_DIM), BF16),
        jax.ShapeDtypeStruct((A_KV_HEADS, N // tm, HEAD_DIM, tm), BF16),
    )
    return pl.pallas_call(
        _inproj_kernel,
        grid=(N // tm,),
        in_specs=[
            pl.BlockSpec((tm, D_MODEL), row),
            _const_spec(n1.shape), _const_spec(wm.shape), _const_spec(wg.shape), _const_spec(wa.shape),
            _const_spec(bg.shape), _const_spec(qw2.shape), _const_spec(kw2.shape),
            pl.BlockSpec((tm, LANES), pos), pl.BlockSpec((tm, LANES), pos), pl.BlockSpec((tm, LANES), pos),
            _const_spec(ones_bd.shape),
        ],
        out_specs=(
            pl.BlockSpec((tm, M_WIDTH), row), pl.BlockSpec((tm, M_WIDTH), row),
            pl.BlockSpec((tm, M_WIDTH), row), pl.BlockSpec((tm, M_WIDTH), row),
            pl.BlockSpec((tm, LANES), row),
            pl.BlockSpec((A_HEADS, 1, HEAD_DIM, tm), lambda i: (0, i, 0, 0)),
            pl.BlockSpec((A_KV_HEADS, 1, tm, HEAD_DIM), lambda i: (0, i, 0, 0)),
            pl.BlockSpec((A_KV_HEADS, 1, HEAD_DIM, tm), lambda i: (0, i, 0, 0)),
        ),
        out_shape=out_shape,
        compiler_params=_cparams(("parallel",)),
        name="inproj",
    )(x2, n1, wm, wg, wa, bg, qw2, kw2, cos, sa, sb, ones_bd)


def _log_sigmoid(x):
    return jnp.minimum(x, 0.0) - jnp.log1p(jnp.exp(-jnp.abs(x)))


def _split3(x):
    a = x.astype(BF16)
    r = x - a.astype(F32)
    b = r.astype(BF16)
    c = (r - b.astype(F32)).astype(BF16)
    return a, b, c


def _mlstm_kernel(*refs, reverse, final):
    if final:
        (q_ref, k_ref, v_ref, g_ref, tri_ref, hb_ref, mo_ref, nw_ref, out_ref, c_ref, m_ref) = refs
    else:
        (q_ref, k_ref, v_ref, g_ref, tri_ref, out_ref, c_ref, m_ref) = refs
    L = q_ref.shape[0]
    d = 1 if reverse else 0
    gate_lane0 = 2 * M_HEADS + d * M_HEADS

    @pl.when(pl.program_id(1) == 0)
    def _():
        c_ref[...] = jnp.zeros_like(c_ref)
        m_ref[...] = jnp.zeros_like(m_ref)

    g = g_ref[...]
    logf = _log_sigmoid(g)
    a, b, c = _split3(logf)
    tri = tri_ref[...]
    cum = (jnp.dot(tri, a, preferred_element_type=F32) + jnp.dot(tri, b, preferred_element_type=F32)
           + jnp.dot(tri, c, preferred_element_type=F32))
    tot = jnp.sum(logf, axis=0, keepdims=True)
    ial = pltpu.roll(g, 2 * M_HEADS, axis=1)
    e = ial - cum
    et = e.T
    m_prev = m_ref[...]
    inter = cum + m_prev
    wend = tot + e
    m_new = jnp.maximum(tot + m_prev, jnp.max(wend, axis=0, keepdims=True))
    dec = jnp.exp(tot + m_prev - m_new)
    we = jnp.exp(wend - m_new)
    m_ref[...] = m_new

    li = lax.broadcasted_iota(jnp.int32, (L, L), 0)
    si = lax.broadcasted_iota(jnp.int32, (L, L), 1)
    keep = (si >= li) if reverse else (si <= li)
    lane = lax.broadcasted_iota(jnp.int32, (L, LANES), 1)
    ones_col = (lane == HEAD_DIM).astype(F32)

    for hd in range(M_HEADS):
        gl = gate_lane0 + hd
        sl = slice(hd * HEAD_DIM, (hd + 1) * HEAD_DIM)
        q = q_ref[:, sl]
        k = k_ref[:, sl]
        v = v_ref[:, sl]
        vext = jnp.concatenate([v.astype(F32), jnp.zeros((L, LANES - HEAD_DIM), F32)], axis=1) + ones_col
        dm = jnp.where(keep, cum[:, gl:gl + 1] + et[gl:gl + 1, :], NEG)
        inter_h = inter[:, gl:gl + 1]
        m_t = jnp.maximum(inter_h, jnp.max(dm, axis=1, keepdims=True))
        dw = jnp.exp(dm - m_t)
        iw = jnp.exp(inter_h - m_t)
        s = lax.dot_general(q, k, (((1,), (1,)), ((), ())), preferred_element_type=F32) * dw
        cext = c_ref[hd]
        nd = (iw * jnp.dot(q, cext.astype(BF16), preferred_element_type=F32)
              + jnp.dot(s.astype(BF16), vext.astype(BF16), preferred_element_type=F32))
        num = nd[:, 0:HEAD_DIM]
        den = nd[:, HEAD_DIM:HEAD_DIM + 1]
        hh = num / jnp.maximum(jnp.abs(den), jnp.exp(-m_t))
        wv = (we[:, gl:gl + 1] * vext).astype(BF16)
        c_ref[hd] = dec[:, gl:gl + 1] * cext + lax.dot_general(
            k, wv, (((0,), (0,)), ((), ())), preferred_element_type=F32)
        if final:
            hs = hh + hb_ref[:, sl]
            hn = hs * lax.rsqrt(jnp.mean(hs * hs, axis=-1, keepdims=True) + EPS) * nw_ref[:, sl]
            mo = mo_ref[:, sl].astype(F32)
            out_ref[:, sl] = (hn / (1.0 + jnp.exp(-mo))).astype(out_ref.dtype)
        else:
            out_ref[:, sl] = hh


def _mlstm(mq, mk, mv, gates, T, tri, hb=None, mo=None, nw=None, *, reverse):
    N = mq.shape[0]
    L = MLSTM_CHUNK
    nc = T // L
    B = N // T
    final = hb is not None
    if reverse:
        row = lambda b, c: (b * nc + (nc - 1 - c), 0)
    else:
        row = lambda b, c: (b * nc + c, 0)
    tok = pl.BlockSpec((L, M_WIDTH), row)
    in_specs = [tok, tok, tok, pl.BlockSpec((L, LANES), row), _const_spec(tri.shape)]
    args = [mq, mk, mv, gates, tri]
    if final:
        in_specs += [tok, tok, _const_spec(nw.shape)]
        args += [hb, mo, nw]
    return pl.pallas_call(
        functools.partial(_mlstm_kernel, reverse=reverse, final=final),
        grid=(B, nc),
        in_specs=in_specs,
        out_specs=tok,
        out_shape=jax.ShapeDtypeStruct((N, M_WIDTH), BF16 if final else F32),
        scratch_shapes=[pltpu.VMEM((M_HEADS, HEAD_DIM, LANES), F32), pltpu.VMEM((1, LANES), F32)],
        compiler_params=_cparams(("parallel", "arbitrary")),
        name="mlstm_fwd" if final else "mlstm_bwd",
    )(*args)


def _attn_kernel(qt_ref, k_ref, vt_ref, o_ref, m_ref, acc_ref, s0_ref, s1_ref):
    nkb = k_ref.shape[1]
    tk = k_ref.shape[2]
    tq = qt_ref.shape[3]
    units = [(j, c) for j in range(A_GROUP) for c in range(0, tq, ATTN_QW)]
    ones = jnp.ones((ATTN_VPAD, tk), BF16)

    m_ref[...] = jnp.full_like(m_ref, -jnp.inf)
    acc_ref[...] = jnp.zeros_like(acc_ref)

    def stage(cur, nxt):
        if nxt is not None:
            k = k_ref[0, nxt[0]]
        if cur is not None:
            vt = jnp.concatenate([vt_ref[0, cur[0]], ones], axis=0)
        for u, (j, c) in enumerate(units):
            cols = slice(c, c + ATTN_QW)
            if nxt is not None:
                nxt[1][u] = jnp.dot(k, qt_ref[j, 0, :, cols], preferred_element_type=F32)
            if cur is not None:
                st = cur[1][u]
                m_old = m_ref[j, :, cols]
                m_new = jnp.maximum(m_old, jnp.max(st, axis=0, keepdims=True))
                alpha = jnp.exp2(m_old - m_new)
                p = jnp.exp2(st - m_new)
                acc_ref[j, :, cols] = (alpha * acc_ref[j, :, cols]
                                       + jnp.dot(vt, p.astype(BF16), preferred_element_type=F32))
                m_ref[j, :, cols] = m_new

    stage(None, (0, s0_ref))

    def body(i, carry):
        kb = 2 * i
        stage((kb, s0_ref), (kb + 1, s1_ref))
        stage((kb + 1, s1_ref), (kb + 2, s0_ref))
        return carry

    lax.fori_loop(0, nkb // 2 - 1, body, 0)
    stage((nkb - 2, s0_ref), (nkb - 1, s1_ref))
    stage((nkb - 1, s1_ref), None)

    def head_out(j):
        return acc_ref[j, 0:HEAD_DIM, :] / acc_ref[j, HEAD_DIM:HEAD_DIM + 1, :]

    for jj in range(A_GROUP // 2):
        o2 = jnp.concatenate([head_out(2 * jj), head_out(2 * jj + 1)], axis=0)
        o_ref[:, jj * LANES:(jj + 1) * LANES] = o2.T.astype(o_ref.dtype)


def _attn(qt, k, vt, T):
    _, nblk, _, blk = qt.shape
    N = nblk * blk
    B = N // T
    nq = T // blk
    assert nq % 2 == 0, "attention pipelines key blocks in pairs"
    nu = A_GROUP * (blk // ATTN_QW)
    return pl.pallas_call(
        _attn_kernel,
        grid=(B, A_KV_HEADS, nq),
        in_specs=[
            pl.BlockSpec((A_GROUP, 1, HEAD_DIM, blk), lambda b, g, qi: (g, b * nq + qi, 0, 0)),
            pl.BlockSpec((1, nq, blk, HEAD_DIM), lambda b, g, qi: (g, b, 0, 0)),
            pl.BlockSpec((1, nq, HEAD_DIM, blk), lambda b, g, qi: (g, b, 0, 0)),
        ],
        out_specs=pl.BlockSpec((blk, A_GROUP * HEAD_DIM), lambda b, g, qi: (b * nq + qi, g)),
        out_shape=jax.ShapeDtypeStruct((N, A_WIDTH), BF16),
        scratch_shapes=[pltpu.VMEM((A_GROUP, 1, blk), F32),
                        pltpu.VMEM((A_GROUP, HEAD_DIM + ATTN_VPAD, blk), F32),
                        pltpu.VMEM((nu, blk, ATTN_QW), F32), pltpu.VMEM((nu, blk, ATTN_QW), F32)],
        compiler_params=_cparams(("parallel", "parallel", "arbitrary")),
        name="attn",
    )(qt, k, vt)


def _outproj_kernel(x_ref, m_ref, a_ref, wom_ref, woa_ref, n2_ref, x1_ref, h2_ref):
    x1 = (x_ref[...] + jnp.dot(m_ref[...], wom_ref[...], preferred_element_type=F32)
          + jnp.dot(a_ref[...], woa_ref[...], preferred_element_type=F32))
    x1_ref[...] = x1
    h2 = x1 * lax.rsqrt(jnp.mean(x1 * x1, axis=-1, keepdims=True) + EPS) * n2_ref[...]
    h2_ref[...] = h2.astype(BF16)


def _outproj(x2, m_out, a_out, wom, woa, n2, tm):
    N = x2.shape[0]
    row = lambda i: (i, 0)
    return pl.pallas_call(
        _outproj_kernel,
        grid=(N // tm,),
        in_specs=[pl.BlockSpec((tm, D_MODEL), row), pl.BlockSpec((tm, M_WIDTH), row),
                  pl.BlockSpec((tm, A_WIDTH), row), _const_spec(wom.shape), _const_spec(woa.shape),
                  _const_spec(n2.shape)],
        out_specs=(pl.BlockSpec((tm, D_MODEL), row), pl.BlockSpec((tm, D_MODEL), row)),
        out_shape=(jax.ShapeDtypeStruct((N, D_MODEL), F32), jax.ShapeDtypeStruct((N, D_MODEL), BF16)),
        compiler_params=_cparams(("parallel",)),
        name="outproj",
    )(x2, m_out, a_out, wom, woa, n2)


def _ffn_kernel(hp_ref, hm_ref, hn_ref, x1_ref, wup_ref, cw_ref, cb_ref, wdn_ref, fw_ref, y_ref,
                lhs_ref, u_ref, acc_ref, *, tiles_per_seq):
    i = pl.program_id(0)
    tm = hm_ref.shape[0]
    H = CONV_HALO
    t = i % tiles_per_seq
    lhs_ref[0:H] = jnp.where(t == 0, jnp.zeros_like(hp_ref), hp_ref[...])
    lhs_ref[H:H + tm] = hm_ref[...]
    lhs_ref[H + tm:H + tm + H] = jnp.where(t == tiles_per_seq - 1, jnp.zeros_like(hn_ref), hn_ref[...])
    acc_ref[...] = jnp.zeros_like(acc_ref)

    def body(c, carry):
        u_ref[...] = jnp.dot(lhs_ref[...], wup_ref[c], preferred_element_type=F32)
        w = cw_ref[c]
        conv = (u_ref[pl.ds(H - 1, tm), :] * w[0:1] + u_ref[pl.ds(H, tm), :] * w[1:2]
                + u_ref[pl.ds(H + 1, tm), :] * w[2:3] + cb_ref[c])
        a = conv[:, 0:FF_CHUNK]
        gt = conv[:, FF_CHUNK:2 * FF_CHUNK]
        act = (gt / (1.0 + jnp.exp(-gt))) * a
        acc_ref[...] += jnp.dot(act.astype(BF16), wdn_ref[c], preferred_element_type=F32)
        return carry

    lax.fori_loop(0, D_FF // FF_CHUNK, body, 0)
    x2 = x1_ref[...] + acc_ref[...]
    y_ref[...] = x2 * lax.rsqrt(jnp.mean(x2 * x2, axis=-1, keepdims=True) + EPS) * fw_ref[...]


def _ffn(h2, x1, T, wup, cw, cb, wdn, fw, tm):
    N = h2.shape[0]
    H = CONV_HALO
    tps = T // tm
    r = tm // H
    nblk = N // H
    row = lambda i: (i, 0)
    return pl.pallas_call(
        functools.partial(_ffn_kernel, tiles_per_seq=tps),
        grid=(N // tm,),
        in_specs=[
            pl.BlockSpec((H, D_MODEL), lambda i: (jnp.maximum(i * r - 1, 0), 0)),
            pl.BlockSpec((tm, D_MODEL), row),
            pl.BlockSpec((H, D_MODEL), lambda i: (jnp.minimum((i + 1) * r, nblk - 1), 0)),
            pl.BlockSpec((tm, D_MODEL), row),
            _const_spec(wup.shape), _const_spec(cw.shape), _const_spec(cb.shape), _const_spec(wdn.shape),
            _const_spec(fw.shape),
        ],
        out_specs=pl.BlockSpec((tm, D_MODEL), row),
        out_shape=jax.ShapeDtypeStruct((N, D_MODEL), F32),
        scratch_shapes=[pltpu.VMEM((tm + 2 * H, D_MODEL), BF16), pltpu.VMEM((tm + 2 * H, 2 * FF_CHUNK), F32),
                        pltpu.VMEM((tm, D_MODEL), F32)],
        compiler_params=_cparams(("parallel",)),
        name="ffn",
    )(h2, h2, h2, x1, wup, cw, cb, wdn, fw)


def _rope_tables(T):
    rows = T // GRID_W
    row = jnp.repeat(jnp.arange(rows, dtype=F32), GRID_W)
    col = jnp.tile(jnp.arange(GRID_W, dtype=F32), rows)
    nf = HEAD_DIM // 4
    inv = ROPE_THETA ** (-jnp.arange(nf, dtype=F32) / nf)
    ar = row[:, None] * inv
    ac = col[:, None] * inv
    ang = jnp.concatenate([ar, ar, ac, ac], axis=-1)
    cos, sin = jnp.cos(ang), jnp.sin(ang)
    quarter = (np.arange(HEAD_DIM) // nf) % 2
    sa = jnp.where(quarter == 0, -sin, 0.0)
    sb = jnp.where(quarter == 1, sin, 0.0)
    two = lambda a: jnp.concatenate([a, a], axis=-1)
    return two(cos), two(sa), two(sb)


def _prep_weights(w_in, b_gates, mh_norm_w, q_norm_w, k_norm_w, w_out, norm1_w, norm2_w, w_up, conv_w, conv_b,
                  w_down, final_norm_w):
    gate0 = 4 * M_WIDTH
    wm = w_in[:, :gate0].astype(BF16)
    wg = jnp.pad(w_in[:, gate0:gate0 + N_GATES], ((0, 0), (0, LANES - N_GATES))).astype(BF16)
    wa = w_in[:, gate0 + N_GATES:].astype(BF16)
    bg = jnp.pad(b_gates, (0, LANES - N_GATES)).reshape(1, LANES)
    nchunk = D_FF // FF_CHUNK
    wua = w_up[:, :D_FF].reshape(D_MODEL, nchunk, FF_CHUNK)
    wug = w_up[:, D_FF:].reshape(D_MODEL, nchunk, FF_CHUNK)
    wup = jnp.concatenate([wua, wug], axis=-1).transpose(1, 0, 2).astype(BF16)
    cwa = conv_w[:, :D_FF].reshape(3, nchunk, FF_CHUNK)
    cwg = conv_w[:, D_FF:].reshape(3, nchunk, FF_CHUNK)
    cw = jnp.concatenate([cwa, cwg], axis=-1).transpose(1, 0, 2)
    cb = jnp.concatenate([conv_b[:D_FF].reshape(nchunk, 1, FF_CHUNK),
                          conv_b[D_FF:].reshape(nchunk, 1, FF_CHUNK)], axis=-1)
    wdn = w_down.reshape(nchunk, FF_CHUNK, D_MODEL).astype(BF16)
    return dict(
        wm=wm, wg=wg, wa=wa, bg=bg,
        n1=norm1_w.reshape(1, D_MODEL), n2=norm2_w.reshape(1, D_MODEL), fw=final_norm_w.reshape(1, D_MODEL),
        qw2=jnp.tile(q_norm_w, 2).reshape(1, LANES), kw2=jnp.tile(k_norm_w, 2).reshape(1, LANES),
        nw=mh_norm_w.reshape(1, M_WIDTH),
        wom=w_out[:M_WIDTH].astype(BF16), woa=w_out[M_WIDTH:].astype(BF16),
        wup=wup, cw=cw, cb=cb, wdn=wdn,
    )


def _tiles(T):
    return min(512, T // 2)


def _trunk(x, p):
    B, T, _ = x.shape
    N = B * T
    tm = _tiles(T)
    x2 = x.reshape(N, D_MODEL)
    cos, sa, sb = _rope_tables(T)
    blk = np.arange(LANES) // HEAD_DIM
    ones_bd = jnp.asarray(blk[:, None] == blk[None, :], BF16)
    idx = np.arange(MLSTM_CHUNK)
    tri_f = jnp.asarray(idx[None, :] <= idx[:, None], BF16)
    tri_b = jnp.asarray(idx[None, :] >= idx[:, None], BF16)

    mq, mk, mv, mo, gates, qt, k, vt = _inproj(x2, T, p["n1"], p["wm"], p["wg"], p["wa"], p["bg"], p["qw2"],
                                               p["kw2"], cos, sa, sb, ones_bd, tm)
    hb = _mlstm(mq, mk, mv, gates, T, tri_b, reverse=True)
    m_out = _mlstm(mq, mk, mv, gates, T, tri_f, hb, mo, p["nw"], reverse=False)
    a_out = _attn(qt, k, vt, T)
    x1, h2 = _outproj(x2, m_out, a_out, p["wom"], p["woa"], p["n2"], tm)
    y = _ffn(h2, x1, T, p["wup"], p["cw"], p["cb"], p["wdn"], p["fw"], tm)
    return y.reshape(B, T, D_MODEL)


def kernel(x_prompt, x_sample, w_in, b_gates, mh_norm_w, q_norm_w, k_norm_w, w_out, norm1_w, norm2_w, w_up,
           conv_w, conv_b, w_down, final_norm_w):
    assert w_in.shape[0] == 1, "single-layer trunk"
    p = _prep_weights(w_in[0], b_gates[0], mh_norm_w[0], q_norm_w[0], k_norm_w[0], w_out[0], norm1_w[0],
                      norm2_w[0], w_up[0], conv_w[0], conv_b[0], w_down[0], final_norm_w)
    return (_trunk(x_prompt, p), _trunk(x_sample, p))
```

```python
import functools
import math

import jax
import jax.numpy as jnp
import numpy as np
from jax import lax
from jax.experimental import pallas as pl
from jax.experimental.pallas import tpu as pltpu

F32 = jnp.float32
BF16 = jnp.bfloat16

D_MODEL = 1024
HEAD_DIM = 64
M_HEADS = 8
M_WIDTH = M_HEADS * HEAD_DIM
A_HEADS = 8
A_KV_HEADS = 2
A_GROUP = A_HEADS // A_KV_HEADS
A_WIDTH = A_HEADS * HEAD_DIM
KV_WIDTH = A_KV_HEADS * HEAD_DIM
N_GATES = 4 * M_HEADS
D_FF = 2816
GRID_W = 64
ROPE_THETA = 10000.0
EPS = 1e-6

LANES = 128
VMEM_LIMIT = 56 * 1024 * 1024

MLSTM_CHUNK = 256
MLSTM_VPAD = 16
FF_CHUNK = 256
CONV_HALO = 16
NEG = -1e30
ATTN_QW = 256
ATTN_VPAD = 16


def _cparams(sem):
    return pltpu.CompilerParams(dimension_semantics=sem, vmem_limit_bytes=VMEM_LIMIT)


def _const_spec(shape):
    nd = len(shape)
    return pl.BlockSpec(shape, lambda *_: (0,) * nd, pipeline_mode=pl.Buffered(1))


def _inproj_kernel(x_ref, n1_ref, wm_ref, wg_ref, wa_ref, bg_ref, qw_ref, kw_ref, cos_ref, sa_ref, sb_ref,
                   ones_ref, mq_ref, mk_ref, mv_ref, mo_ref, g_ref, qt_ref, k_ref, vt_ref):
    x = x_ref[...]
    h = x * lax.rsqrt(jnp.mean(x * x, axis=-1, keepdims=True) + EPS) * n1_ref[...]
    h = h.astype(BF16)

    pm = jnp.dot(h, wm_ref[...], preferred_element_type=F32)

    def heads_t(cols0, ref):
        for c in range(M_WIDTH // LANES):
            t2 = pm[:, cols0 + c * LANES:cols0 + (c + 1) * LANES].T.astype(BF16)
            ref[2 * c, 0] = t2[0:HEAD_DIM]
            ref[2 * c + 1, 0] = t2[HEAD_DIM:2 * HEAD_DIM]

    heads_t(0, mq_ref)
    heads_t(2 * M_WIDTH, mv_ref)
    heads_t(3 * M_WIDTH, mo_ref)
    for hd in range(M_HEADS):
        c0 = M_WIDTH + hd * HEAD_DIM
        mk_ref[hd, 0] = (pm[:, c0:c0 + HEAD_DIM] * (HEAD_DIM ** -0.5)).astype(BF16)

    g_ref[...] = jnp.dot(h, wg_ref[...], preferred_element_type=F32) + bg_ref[...]

    pa = jnp.dot(h, wa_ref[...], preferred_element_type=F32)
    cos, sa, sb = cos_ref[...], sa_ref[...], sb_ref[...]
    ones_bd = ones_ref[...]

    def norm_rope(y, w):
        sq = y * y
        hi = sq.astype(BF16)
        lo = (sq - hi.astype(F32)).astype(BF16)
        ss = (jnp.dot(hi, ones_bd, preferred_element_type=F32)
              + jnp.dot(lo, ones_bd, preferred_element_type=F32))
        yn = y * lax.rsqrt(ss * (1.0 / HEAD_DIM) + EPS) * w
        up = pltpu.roll(yn, LANES - HEAD_DIM // 4, axis=1)
        dn = pltpu.roll(yn, HEAD_DIM // 4, axis=1)
        return yn * cos + up * sa + dn * sb

    qscale = (HEAD_DIM ** -0.5) * math.log2(math.e)
    for c in range(A_WIDTH // LANES):
        q2 = norm_rope(pa[:, c * LANES:(c + 1) * LANES], qw_ref[...]) * qscale
        q2t = q2.T.astype(BF16)
        qt_ref[2 * c, 0] = q2t[0:HEAD_DIM]
        qt_ref[2 * c + 1, 0] = q2t[HEAD_DIM:2 * HEAD_DIM]
    k2 = norm_rope(pa[:, A_WIDTH:A_WIDTH + KV_WIDTH], kw_ref[...]).astype(BF16)
    k_ref[0, 0] = k2[:, 0:HEAD_DIM]
    k_ref[1, 0] = k2[:, HEAD_DIM:2 * HEAD_DIM]
    v2t = pa[:, A_WIDTH + KV_WIDTH:A_WIDTH + 2 * KV_WIDTH].T.astype(BF16)
    vt_ref[0, 0] = v2t[0:HEAD_DIM]
    vt_ref[1, 0] = v2t[HEAD_DIM:2 * HEAD_DIM]


def _inproj(x2, T, n1, wm, wg, wa, bg, qw2, kw2, cos, sa, sb, ones_bd, tm):
    N = x2.shape[0]
    tps = T // tm
    row = lambda i: (i, 0)
    pos = lambda i: (i % tps, 0)
    out_shape = (
        jax.ShapeDtypeStruct((M_HEADS, N // tm, HEAD_DIM, tm), BF16),
        jax.ShapeDtypeStruct((M_HEADS, N // tm, tm, HEAD_DIM), BF16),
        jax.ShapeDtypeStruct((M_HEADS, N // tm, HEAD_DIM, tm), BF16),
        jax.ShapeDtypeStruct((M_HEADS, N // tm, HEAD_DIM, tm), BF16),
        jax.ShapeDtypeStruct((N, LANES), F32),
        jax.ShapeDtypeStruct((A_HEADS, N // tm, HEAD_DIM, tm), BF16),
        jax.ShapeDtypeStruct((A_KV_HEADS, N // tm, tm, HEAD_DIM), BF16),
        jax.ShapeDtypeStruct((A_KV_HEADS, N // tm, HEAD_DIM, tm), BF16),
    )
    return pl.pallas_call(
        _inproj_kernel,
        grid=(N // tm,),
        in_specs=[
            pl.BlockSpec((tm, D_MODEL), row),
            _const_spec(n1.shape), _const_spec(wm.shape), _const_spec(wg.shape), _const_spec(wa.shape),
            _const_spec(bg.shape), _const_spec(qw2.shape), _const_spec(kw2.shape),
            pl.BlockSpec((tm, LANES), pos), pl.BlockSpec((tm, LANES), pos), pl.BlockSpec((tm, LANES), pos),
            _const_spec(ones_bd.shape),
        ],
        out_specs=(
            pl.BlockSpec((M_HEADS, 1, HEAD_DIM, tm), lambda i: (0, i, 0, 0)),
            pl.BlockSpec((M_HEADS, 1, tm, HEAD_DIM), lambda i: (0, i, 0, 0)),
            pl.BlockSpec((M_HEADS, 1, HEAD_DIM, tm), lambda i: (0, i, 0, 0)),
            pl.BlockSpec((M_HEADS, 1, HEAD_DIM, tm), lambda i: (0, i, 0, 0)),
            pl.BlockSpec((tm, LANES), row),
            pl.BlockSpec((A_HEADS, 1, HEAD_DIM, tm), lambda i: (0, i, 0, 0)),
            pl.BlockSpec((A_KV_HEADS, 1, tm, HEAD_DIM), lambda i: (0, i, 0, 0)),
            pl.BlockSpec((A_KV_HEADS, 1, HEAD_DIM, tm), lambda i: (0, i, 0, 0)),
        ),
        out_shape=out_shape,
        compiler_params=_cparams(("parallel",)),
        name="inproj",
    )(x2, n1, wm, wg, wa, bg, qw2, kw2, cos, sa, sb, ones_bd)


def _log_sigmoid(x):
    return jnp.minimum(x, 0.0) - jnp.log1p(jnp.exp(-jnp.abs(x)))


def _split3(x):
    a = x.astype(BF16)
    r = x - a.astype(F32)
    b = r.astype(BF16)
    c = (r - b.astype(F32)).astype(BF16)
    return a, b, c


def _mlstm_kernel(*refs, reverse, final):
    if final:
        (qt_ref, k_ref, vt_ref, g_ref, tri_ref, hb_ref, mot_ref, nw_ref, out_ref, c_ref, m_ref) = refs
    else:
        (qt_ref, k_ref, vt_ref, g_ref, tri_ref, out_ref, c_ref, m_ref) = refs
    L = g_ref.shape[0]
    d = 1 if reverse else 0
    gate_lane0 = 2 * M_HEADS + d * M_HEADS

    @pl.when(pl.program_id(1) == 0)
    def _():
        c_ref[...] = jnp.zeros_like(c_ref)
        m_ref[...] = jnp.zeros_like(m_ref)

    g = g_ref[...]
    logf = _log_sigmoid(g)
    a, b, c = _split3(logf)
    tri = tri_ref[...]
    cum = (jnp.dot(tri, a, preferred_element_type=F32) + jnp.dot(tri, b, preferred_element_type=F32)
           + jnp.dot(tri, c, preferred_element_type=F32))
    tot = jnp.sum(logf, axis=0, keepdims=True)
    ial = pltpu.roll(g, 2 * M_HEADS, axis=1)
    e = ial - cum
    m_prev = m_ref[...]
    wend = tot + e
    m_new = jnp.maximum(tot + m_prev, jnp.max(wend, axis=0, keepdims=True))
    dec = jnp.exp(tot + m_prev - m_new)
    m_ref[...] = m_new
    cum_t = cum.T
    inter_t = (cum + m_prev).T
    we_t = jnp.exp(wend - m_new).T

    si = lax.broadcasted_iota(jnp.int32, (L, L), 0)
    li = lax.broadcasted_iota(jnp.int32, (L, L), 1)
    keep = (si >= li) if reverse else (si <= li)
    ones = jnp.ones((MLSTM_VPAD, L), BF16)

    outs = []
    for hd in range(M_HEADS):
        gl = gate_lane0 + hd
        qt = qt_ref[hd, 0]
        k = k_ref[hd, 0]
        vext = jnp.concatenate([vt_ref[hd, 0], ones], axis=0)
        dm = jnp.where(keep, e[:, gl:gl + 1] + cum_t[gl:gl + 1, :], NEG)
        inter_h = inter_t[gl:gl + 1, :]
        m_t = jnp.maximum(inter_h, jnp.max(dm, axis=0, keepdims=True))
        dw = jnp.exp(dm - m_t)
        iw = jnp.exp(inter_h - m_t)
        s = jnp.dot(k, qt, preferred_element_type=F32) * dw
        cext = c_ref[hd]
        nd = (iw * jnp.dot(cext.astype(BF16), qt, preferred_element_type=F32)
              + jnp.dot(vext, s.astype(BF16), preferred_element_type=F32))
        num = nd[0:HEAD_DIM]
        den = nd[HEAD_DIM:HEAD_DIM + 1]
        hh = num / jnp.maximum(jnp.abs(den), jnp.exp(-m_t))
        wv = (vext.astype(F32) * we_t[gl:gl + 1, :]).astype(BF16)
        c_ref[hd] = dec[:, gl:gl + 1] * cext + jnp.dot(wv, k, preferred_element_type=F32)
        if final:
            hs = hh + hb_ref[hd, 0]
            hn = hs * lax.rsqrt(jnp.mean(hs * hs, axis=0, keepdims=True) + EPS) * nw_ref[hd]
            mo = mot_ref[hd, 0].astype(F32)
            outs.append(hn / (1.0 + jnp.exp(-mo)))
        else:
            out_ref[hd, 0] = hh
    if final:
        for jj in range(M_HEADS // 2):
            o2 = jnp.concatenate([outs[2 * jj], outs[2 * jj + 1]], axis=0)
            out_ref[:, jj * LANES:(jj + 1) * LANES] = o2.T.astype(out_ref.dtype)


def _mlstm(mqt, mk, mvt, gates, T, tri, hb=None, mot=None, nwb=None, *, reverse):
    _, nblk, _, blk = mqt.shape
    N = nblk * blk
    L = MLSTM_CHUNK
    per = blk // L
    nc = T // L
    B = N // T
    final = hb is not None
    if reverse:
        cidx = lambda b, c: b * nc + (nc - 1 - c)
    else:
        cidx = lambda b, c: b * nc + c
    tspec = pl.BlockSpec((M_HEADS, 1, HEAD_DIM, L), lambda b, c: (0, cidx(b, c) // per, 0, cidx(b, c) % per))
    kspec = pl.BlockSpec((M_HEADS, 1, L, HEAD_DIM), lambda b, c: (0, cidx(b, c) // per, cidx(b, c) % per, 0))
    hspec = pl.BlockSpec((M_HEADS, 1, HEAD_DIM, L), lambda b, c: (0, cidx(b, c), 0, 0))
    in_specs = [tspec, kspec, tspec, pl.BlockSpec((L, LANES), lambda b, c: (cidx(b, c), 0)), _const_spec(tri.shape)]
    args = [mqt, mk, mvt, gates, tri]
    if final:
        in_specs += [hspec, tspec, _const_spec(nwb.shape)]
        args += [hb, mot, nwb]
        out_specs = pl.BlockSpec((L, M_WIDTH), lambda b, c: (cidx(b, c), 0))
        out_shape = jax.ShapeDtypeStruct((N, M_WIDTH), BF16)
    else:
        out_specs = hspec
        out_shape = jax.ShapeDtypeStruct((M_HEADS, N // L, HEAD_DIM, L), F32)
    return pl.pallas_call(
        functools.partial(_mlstm_kernel, reverse=reverse, final=final),
        grid=(B, nc),
        in_specs=in_specs,
        out_specs=out_specs,
        out_shape=out_shape,
        scratch_shapes=[pltpu.VMEM((M_HEADS, HEAD_DIM + MLSTM_VPAD, HEAD_DIM), F32), pltpu.VMEM((1, LANES), F32)],
        compiler_params=_cparams(("parallel", "arbitrary")),
        name="mlstm_fwd" if final else "mlstm_bwd",
    )(*args)


def _attn_kernel(qt_ref, k_ref, vt_ref, o_ref, m_ref, acc_ref, s0_ref, s1_ref):
    nkb = k_ref.shape[1]
    tk = k_ref.shape[2]
    tq = qt_ref.shape[3]
    units = [(j, c) for j in range(A_GROUP) for c in range(0, tq, ATTN_QW)]
    ones = jnp.ones((ATTN_VPAD, tk), BF16)

    m_ref[...] = jnp.full_like(m_ref, -jnp.inf)
    acc_ref[...] = jnp.zeros_like(acc_ref)

    def stage(cur, nxt):
        if nxt is not None:
            k = k_ref[0, nxt[0]]
        if cur is not None:
            vt = jnp.concatenate([vt_ref[0, cur[0]], ones], axis=0)
        for u, (j, c) in enumerate(units):
            cols = slice(c, c + ATTN_QW)
            if nxt is not None:
                nxt[1][u] = jnp.dot(k, qt_ref[j, 0, :, cols], preferred_element_type=F32)
            if cur is not None:
                st = cur[1][u]
                m_old = m_ref[j, :, cols]
                m_new = jnp.maximum(m_old, jnp.max(st, axis=0, keepdims=True))
                alpha = jnp.exp2(m_old - m_new)
                p = jnp.exp2(st - m_new)
                acc_ref[j, :, cols] = (alpha * acc_ref[j, :, cols]
                                       + jnp.dot(vt, p.astype(BF16), preferred_element_type=F32))
                m_ref[j, :, cols] = m_new

    stage(None, (0, s0_ref))

    def body(i, carry):
        kb = 2 * i
        stage((kb, s0_ref), (kb + 1, s1_ref))
        stage((kb + 1, s1_ref), (kb + 2, s0_ref))
        return carry

    lax.fori_loop(0, nkb // 2 - 1, body, 0)
    stage((nkb - 2, s0_ref), (nkb - 1, s1_ref))
    stage((nkb - 1, s1_ref), None)

    def head_out(j):
        return acc_ref[j, 0:HEAD_DIM, :] / acc_ref[j, HEAD_DIM:HEAD_DIM + 1, :]

    for jj in range(A_GROUP // 2):
        o2 = jnp.concatenate([head_out(2 * jj), head_out(2 * jj + 1)], axis=0)
        o_ref[:, jj * LANES:(jj + 1) * LANES] = o2.T.astype(o_ref.dtype)


def _attn(qt, k, vt, T):
    _, nblk, _, blk = qt.shape
    N = nblk * blk
    B = N // T
    nq = T // blk
    assert nq % 2 == 0, "attention pipelines key blocks in pairs"
    nu = A_GROUP * (blk // ATTN_QW)
    return pl.pallas_call(
        _attn_kernel,
        grid=(B, A_KV_HEADS, nq),
        in_specs=[
            pl.BlockSpec((A_GROUP, 1, HEAD_DIM, blk), lambda b, g, qi: (g, b * nq + qi, 0, 0)),
            pl.BlockSpec((1, nq, blk, HEAD_DIM), lambda b, g, qi: (g, b, 0, 0)),
            pl.BlockSpec((1, nq, HEAD_DIM, blk), lambda b, g, qi: (g, b, 0, 0)),
        ],
        out_specs=pl.BlockSpec((blk, A_GROUP * HEAD_DIM), lambda b, g, qi: (b * nq + qi, g)),
        out_shape=jax.ShapeDtypeStruct((N, A_WIDTH), BF16),
        scratch_shapes=[pltpu.VMEM((A_GROUP, 1, blk), F32),
                        pltpu.VMEM((A_GROUP, HEAD_DIM + ATTN_VPAD, blk), F32),
                        pltpu.VMEM((nu, blk, ATTN_QW), F32), pltpu.VMEM((nu, blk, ATTN_QW), F32)],
        compiler_params=_cparams(("parallel", "parallel", "arbitrary")),
        name="attn",
    )(qt, k, vt)


def _outproj_kernel(x_ref, m_ref, a_ref, wom_ref, woa_ref, n2_ref, x1_ref, h2_ref):
    x1 = (x_ref[...] + jnp.dot(m_ref[...], wom_ref[...], preferred_element_type=F32)
          + jnp.dot(a_ref[...], woa_ref[...], preferred_element_type=F32))
    x1_ref[...] = x1
    h2 = x1 * lax.rsqrt(jnp.mean(x1 * x1, axis=-1, keepdims=True) + EPS) * n2_ref[...]
    h2_ref[...] = h2.astype(BF16)


def _outproj(x2, m_out, a_out, wom, woa, n2, tm):
    N = x2.shape[0]
    row = lambda i: (i, 0)
    return pl.pallas_call(
        _outproj_kernel,
        grid=(N // tm,),
        in_specs=[pl.BlockSpec((tm, D_MODEL), row), pl.BlockSpec((tm, M_WIDTH), row),
                  pl.BlockSpec((tm, A_WIDTH), row), _const_spec(wom.shape), _const_spec(woa.shape),
                  _const_spec(n2.shape)],
        out_specs=(pl.BlockSpec((tm, D_MODEL), row), pl.BlockSpec((tm, D_MODEL), row)),
        out_shape=(jax.ShapeDtypeStruct((N, D_MODEL), F32), jax.ShapeDtypeStruct((N, D_MODEL), BF16)),
        compiler_params=_cparams(("parallel",)),
        name="outproj",
    )(x2, m_out, a_out, wom, woa, n2)


def _ffn_kernel(hp_ref, hm_ref, hn_ref, x1_ref, wup_ref, cw_ref, cb_ref, wdn_ref, fw_ref, y_ref,
                lhs_ref, u0_ref, u1_ref, acc_ref, *, tiles_per_seq):
    i = pl.program_id(0)
    tm = hm_ref.shape[0]
    H = CONV_HALO
    t = i % tiles_per_seq
    lhs_ref[0:H] = jnp.where(t == 0, jnp.zeros_like(hp_ref), hp_ref[...])
    lhs_ref[H:H + tm] = hm_ref[...]
    lhs_ref[H + tm:H + tm + H] = jnp.where(t == tiles_per_seq - 1, jnp.zeros_like(hn_ref), hn_ref[...])
    acc_ref[...] = jnp.zeros_like(acc_ref)
    nchunk = D_FF // FF_CHUNK

    def up(c, u_ref):
        u_ref[...] = jnp.dot(lhs_ref[...], wup_ref[c], preferred_element_type=F32)

    def gate_down(c, u_ref):
        w = cw_ref[c]
        conv = (u_ref[pl.ds(H - 1, tm), :] * w[0:1] + u_ref[pl.ds(H, tm), :] * w[1:2]
                + u_ref[pl.ds(H + 1, tm), :] * w[2:3] + cb_ref[c])
        a = conv[:, 0:FF_CHUNK]
        gt = conv[:, FF_CHUNK:2 * FF_CHUNK]
        act = (gt / (1.0 + jnp.exp(-gt))) * a
        acc_ref[...] += jnp.dot(act.astype(BF16), wdn_ref[c], preferred_element_type=F32)

    assert nchunk % 2 == 1
    up(0, u0_ref)

    def body(i, carry):
        c = 2 * i
        up(c + 1, u1_ref)
        gate_down(c, u0_ref)
        up(c + 2, u0_ref)
        gate_down(c + 1, u1_ref)
        return carry

    lax.fori_loop(0, nchunk // 2, body, 0)
    gate_down(nchunk - 1, u0_ref)
    x2 = x1_ref[...] + acc_ref[...]
    y_ref[...] = x2 * lax.rsqrt(jnp.mean(x2 * x2, axis=-1, keepdims=True) + EPS) * fw_ref[...]


def _ffn(h2, x1, T, wup, cw, cb, wdn, fw, tm):
    N = h2.shape[0]
    H = CONV_HALO
    tps = T // tm
    r = tm // H
    nblk = N // H
    row = lambda i: (i, 0)
    return pl.pallas_call(
        functools.partial(_ffn_kernel, tiles_per_seq=tps),
        grid=(N // tm,),
        in_specs=[
            pl.BlockSpec((H, D_MODEL), lambda i: (jnp.maximum(i * r - 1, 0), 0)),
            pl.BlockSpec((tm, D_MODEL), row),
            pl.BlockSpec((H, D_MODEL), lambda i: (jnp.minimum((i + 1) * r, nblk - 1), 0)),
            pl.BlockSpec((tm, D_MODEL), row),
            _const_spec(wup.shape), _const_spec(cw.shape), _const_spec(cb.shape), _const_spec(wdn.shape),
            _const_spec(fw.shape),
        ],
        out_specs=pl.BlockSpec((tm, D_MODEL), row),
        out_shape=jax.ShapeDtypeStruct((N, D_MODEL), F32),
        scratch_shapes=[pltpu.VMEM((tm + 2 * H, D_MODEL), BF16), pltpu.VMEM((tm + 2 * H, 2 * FF_CHUNK), F32),
                        pltpu.VMEM((tm + 2 * H, 2 * FF_CHUNK), F32),
                        pltpu.VMEM((tm, D_MODEL), F32)],
        compiler_params=_cparams(("parallel",)),
        name="ffn",
    )(h2, h2, h2, x1, wup, cw, cb, wdn, fw)


def _rope_tables(T):
    rows = T // GRID_W
    row = jnp.repeat(jnp.arange(rows, dtype=F32), GRID_W)
    col = jnp.tile(jnp.arange(GRID_W, dtype=F32), rows)
    nf = HEAD_DIM // 4
    inv = ROPE_THETA ** (-jnp.arange(nf, dtype=F32) / nf)
    ar = row[:, None] * inv
    ac = col[:, None] * inv
    ang = jnp.concatenate([ar, ar, ac, ac], axis=-1)
    cos, sin = jnp.cos(ang), jnp.sin(ang)
    quarter = (np.arange(HEAD_DIM) // nf) % 2
    sa = jnp.where(quarter == 0, -sin, 0.0)
    sb = jnp.where(quarter == 1, sin, 0.0)
    two = lambda a: jnp.concatenate([a, a], axis=-1)
    return two(cos), two(sa), two(sb)


def _prep_weights(w_in, b_gates, mh_norm_w, q_norm_w, k_norm_w, w_out, norm1_w, norm2_w, w_up, conv_w, conv_b,
                  w_down, final_norm_w):
    gate0 = 4 * M_WIDTH
    wm = w_in[:, :gate0].astype(BF16)
    wg = jnp.pad(w_in[:, gate0:gate0 + N_GATES], ((0, 0), (0, LANES - N_GATES))).astype(BF16)
    wa = w_in[:, gate0 + N_GATES:].astype(BF16)
    bg = jnp.pad(b_gates, (0, LANES - N_GATES)).reshape(1, LANES)
    nchunk = D_FF // FF_CHUNK
    wua = w_up[:, :D_FF].reshape(D_MODEL, nchunk, FF_CHUNK)
    wug = w_up[:, D_FF:].reshape(D_MODEL, nchunk, FF_CHUNK)
    wup = jnp.concatenate([wua, wug], axis=-1).transpose(1, 0, 2).astype(BF16)
    cwa = conv_w[:, :D_FF].reshape(3, nchunk, FF_CHUNK)
    cwg = conv_w[:, D_FF:].reshape(3, nchunk, FF_CHUNK)
    cw = jnp.concatenate([cwa, cwg], axis=-1).transpose(1, 0, 2)
    cb = jnp.concatenate([conv_b[:D_FF].reshape(nchunk, 1, FF_CHUNK),
                          conv_b[D_FF:].reshape(nchunk, 1, FF_CHUNK)], axis=-1)
    wdn = w_down.reshape(nchunk, FF_CHUNK, D_MODEL).astype(BF16)
    return dict(
        wm=wm, wg=wg, wa=wa, bg=bg,
        n1=norm1_w.reshape(1, D_MODEL), n2=norm2_w.reshape(1, D_MODEL), fw=final_norm_w.reshape(1, D_MODEL),
        qw2=jnp.tile(q_norm_w, 2).reshape(1, LANES), kw2=jnp.tile(k_norm_w, 2).reshape(1, LANES),
        nw=mh_norm_w.reshape(1, M_WIDTH),
        wom=w_out[:M_WIDTH].astype(BF16), woa=w_out[M_WIDTH:].astype(BF16),
        wup=wup, cw=cw, cb=cb, wdn=wdn,
    )


def _tiles(T):
    return min(512, T // 2)


def _trunk(x, p):
    B, T, _ = x.shape
    N = B * T
    tm = _tiles(T)
    x2 = x.reshape(N, D_MODEL)
    cos, sa, sb = _rope_tables(T)
    blk = np.arange(LANES) // HEAD_DIM
    ones_bd = jnp.asarray(blk[:, None] == blk[None, :], BF16)
    idx = np.arange(MLSTM_CHUNK)
    tri_f = jnp.asarray(idx[None, :] <= idx[:, None], BF16)
    tri_b = jnp.asarray(idx[None, :] >= idx[:, None], BF16)

    mq, mk, mv, mo, gates, qt, k, vt = _inproj(x2, T, p["n1"], p["wm"], p["wg"], p["wa"], p["bg"], p["qw2"],
                                               p["kw2"], cos, sa, sb, ones_bd, tm)
    nwb = jnp.broadcast_to(p["nw"].reshape(M_HEADS, HEAD_DIM, 1), (M_HEADS, HEAD_DIM, MLSTM_CHUNK))
    hb = _mlstm(mq, mk, mv, gates, T, tri_b, reverse=True)
    m_out = _mlstm(mq, mk, mv, gates, T, tri_f, hb, mo, nwb, reverse=False)
    a_out = _attn(qt, k, vt, T)
    x1, h2 = _outproj(x2, m_out, a_out, p["wom"], p["woa"], p["n2"], tm)
    y = _ffn(h2, x1, T, p["wup"], p["cw"], p["cb"], p["wdn"], p["fw"], tm)
    return y.reshape(B, T, D_MODEL)


def kernel(x_prompt, x_sample, w_in, b_gates, mh_norm_w, q_norm_w, k_norm_w, w_out, norm1_w, norm2_w, w_up,
           conv_w, conv_b, w_down, final_norm_w):
    assert w_in.shape[0] == 1, "single-layer trunk"
    p = _prep_weights(w_in[0], b_gates[0], mh_norm_w[0], q_norm_w[0], k_norm_w[0], w_out[0], norm1_w[0],
                      norm2_w[0], w_up[0], conv_w[0], conv_b[0], w_down[0], final_norm_w)
    return (_trunk(x_prompt, p), _trunk(x_sample, p))
```

```python
import functools
import math

import jax
import jax.numpy as jnp
import numpy as np
from jax import lax
from jax.experimental import pallas as pl
from jax.experimental.pallas import tpu as pltpu

F32 = jnp.float32
BF16 = jnp.bfloat16

D_MODEL = 1024
HEAD_DIM = 64
M_HEADS = 8
M_WIDTH = M_HEADS * HEAD_DIM
A_HEADS = 8
A_KV_HEADS = 2
A_GROUP = A_HEADS // A_KV_HEADS
A_WIDTH = A_HEADS * HEAD_DIM
KV_WIDTH = A_KV_HEADS * HEAD_DIM
N_GATES = 4 * M_HEADS
D_FF = 2816
GRID_W = 64
ROPE_THETA = 10000.0
EPS = 1e-6

LANES = 128
VMEM_LIMIT = 56 * 1024 * 1024

MLSTM_CHUNK = 256
MLSTM_VPAD = 16
FF_CHUNK = 256
CONV_HALO = 16
NEG = -1e30
ATTN_QW = 256
ATTN_VPAD = 16


def _cparams(sem):
    return pltpu.CompilerParams(dimension_semantics=sem, vmem_limit_bytes=VMEM_LIMIT)


def _const_spec(shape):
    nd = len(shape)
    return pl.BlockSpec(shape, lambda *_: (0,) * nd, pipeline_mode=pl.Buffered(1))


def _inproj_kernel(x_ref, n1_ref, wm_ref, wg_ref, wa_ref, bg_ref, qw_ref, kw_ref, cos_ref, sa_ref, sb_ref,
                   ones_ref, tri_ref, mq_ref, mk_ref, mv_ref, mo_ref, cum_ref, e_ref, emax_ref, qt_ref, k_ref, vt_ref):
    x = x_ref[...]
    h = x * lax.rsqrt(jnp.mean(x * x, axis=-1, keepdims=True) + EPS) * n1_ref[...]
    h = h.astype(BF16)

    g = jnp.dot(h, wg_ref[...], preferred_element_type=F32) + bg_ref[...]
    cum_ref[...], e_ref[...], emax_ref[...] = _gate_sums(g, tri_ref[...])

    pa = jnp.dot(h, wa_ref[...], preferred_element_type=F32)
    cos, sa, sb = cos_ref[...], sa_ref[...], sb_ref[...]
    ones_bd = ones_ref[...]

    def norm_rope(y, w):
        sq = y * y
        hi = sq.astype(BF16)
        lo = (sq - hi.astype(F32)).astype(BF16)
        ss = (jnp.dot(hi, ones_bd, preferred_element_type=F32)
              + jnp.dot(lo, ones_bd, preferred_element_type=F32))
        yn = y * lax.rsqrt(ss * (1.0 / HEAD_DIM) + EPS) * w
        up = pltpu.roll(yn, LANES - HEAD_DIM // 4, axis=1)
        dn = pltpu.roll(yn, HEAD_DIM // 4, axis=1)
        return yn * cos + up * sa + dn * sb

    qscale = (HEAD_DIM ** -0.5) * math.log2(math.e)
    for c in range(A_WIDTH // LANES):
        q2 = norm_rope(pa[:, c * LANES:(c + 1) * LANES], qw_ref[...]) * qscale
        q2t = q2.T.astype(BF16)
        qt_ref[2 * c, 0] = q2t[0:HEAD_DIM]
        qt_ref[2 * c + 1, 0] = q2t[HEAD_DIM:2 * HEAD_DIM]
    k2 = norm_rope(pa[:, A_WIDTH:A_WIDTH + KV_WIDTH], kw_ref[...]).astype(BF16)
    k_ref[0, 0] = k2[:, 0:HEAD_DIM]
    k_ref[1, 0] = k2[:, HEAD_DIM:2 * HEAD_DIM]
    v2t = pa[:, A_WIDTH + KV_WIDTH:A_WIDTH + 2 * KV_WIDTH].T.astype(BF16)
    vt_ref[0, 0] = v2t[0:HEAD_DIM]
    vt_ref[1, 0] = v2t[HEAD_DIM:2 * HEAD_DIM]

    pm = jnp.dot(h, wm_ref[...], preferred_element_type=F32)

    def heads_t(cols0, ref):
        for c in range(M_WIDTH // LANES):
            t2 = pm[:, cols0 + c * LANES:cols0 + (c + 1) * LANES].T.astype(BF16)
            ref[2 * c, 0] = t2[0:HEAD_DIM]
            ref[2 * c + 1, 0] = t2[HEAD_DIM:2 * HEAD_DIM]

    heads_t(0, mq_ref)
    heads_t(2 * M_WIDTH, mv_ref)
    heads_t(3 * M_WIDTH, mo_ref)
    for hd in range(M_HEADS):
        c0 = M_WIDTH + hd * HEAD_DIM
        mk_ref[hd, 0] = (pm[:, c0:c0 + HEAD_DIM] * (HEAD_DIM ** -0.5)).astype(BF16)


def _inproj(x2, T, n1, wm, wg, wa, bg, qw2, kw2, cos, sa, sb, ones_bd, tri, tm):
    N = x2.shape[0]
    tps = T // tm
    row = lambda i: (i, 0)
    pos = lambda i: (i % tps, 0)
    out_shape = (
        jax.ShapeDtypeStruct((M_HEADS, N // tm, HEAD_DIM, tm), BF16),
        jax.ShapeDtypeStruct((M_HEADS, N // tm, tm, HEAD_DIM), BF16),
        jax.ShapeDtypeStruct((M_HEADS, N // tm, HEAD_DIM, tm), BF16),
        jax.ShapeDtypeStruct((M_HEADS, N // tm, HEAD_DIM, tm), BF16),
        jax.ShapeDtypeStruct((N, LANES), F32), jax.ShapeDtypeStruct((N, LANES), F32),
        jax.ShapeDtypeStruct((N, LANES), F32),
        jax.ShapeDtypeStruct((A_HEADS, N // tm, HEAD_DIM, tm), BF16),
        jax.ShapeDtypeStruct((A_KV_HEADS, N // tm, tm, HEAD_DIM), BF16),
        jax.ShapeDtypeStruct((A_KV_HEADS, N // tm, HEAD_DIM, tm), BF16),
    )
    return pl.pallas_call(
        _inproj_kernel,
        grid=(N // tm,),
        in_specs=[
            pl.BlockSpec((tm, D_MODEL), row),
            _const_spec(n1.shape), _const_spec(wm.shape), _const_spec(wg.shape), _const_spec(wa.shape),
            _const_spec(bg.shape), _const_spec(qw2.shape), _const_spec(kw2.shape),
            pl.BlockSpec((tm, LANES), pos), pl.BlockSpec((tm, LANES), pos), pl.BlockSpec((tm, LANES), pos),
            _const_spec(ones_bd.shape), _const_spec(tri.shape),
        ],
        out_specs=(
            pl.BlockSpec((M_HEADS, 1, HEAD_DIM, tm), lambda i: (0, i, 0, 0)),
            pl.BlockSpec((M_HEADS, 1, tm, HEAD_DIM), lambda i: (0, i, 0, 0)),
            pl.BlockSpec((M_HEADS, 1, HEAD_DIM, tm), lambda i: (0, i, 0, 0)),
            pl.BlockSpec((M_HEADS, 1, HEAD_DIM, tm), lambda i: (0, i, 0, 0)),
            pl.BlockSpec((tm, LANES), row), pl.BlockSpec((tm, LANES), row), pl.BlockSpec((tm, LANES), row),
            pl.BlockSpec((A_HEADS, 1, HEAD_DIM, tm), lambda i: (0, i, 0, 0)),
            pl.BlockSpec((A_KV_HEADS, 1, tm, HEAD_DIM), lambda i: (0, i, 0, 0)),
            pl.BlockSpec((A_KV_HEADS, 1, HEAD_DIM, tm), lambda i: (0, i, 0, 0)),
        ),
        out_shape=out_shape,
        compiler_params=_cparams(("parallel",)),
        name="inproj",
    )(x2, n1, wm, wg, wa, bg, qw2, kw2, cos, sa, sb, ones_bd, tri)


def _log_sigmoid(x):
    return jnp.minimum(x, 0.0) - jnp.log1p(jnp.exp(-jnp.abs(x)))


def _split3(x):
    a = x.astype(BF16)
    r = x - a.astype(F32)
    b = r.astype(BF16)
    c = (r - b.astype(F32)).astype(BF16)
    return a, b, c


def _gate_sums(g, tri):
    tm = g.shape[0]
    logf = _log_sigmoid(g) * math.log2(math.e)
    a, b, c = _split3(logf)
    pre = (jnp.dot(tri, a, preferred_element_type=F32) + jnp.dot(tri, b, preferred_element_type=F32)
           + jnp.dot(tri, c, preferred_element_type=F32))
    tot = jnp.concatenate([jnp.broadcast_to(pre[r + MLSTM_CHUNK - 1:r + MLSTM_CHUNK], (MLSTM_CHUNK, LANES))
                           for r in range(0, tm, MLSTM_CHUNK)], axis=0)
    fwd = _is_fwd_lane((tm, LANES))
    cum = jnp.where(fwd, pre, tot - pre + logf)
    e = pltpu.roll(g, 2 * M_HEADS, axis=1) * math.log2(math.e) - cum

    row = lax.broadcasted_iota(jnp.int32, (tm, LANES), 0) & (MLSTM_CHUNK - 1)
    up, down = e, e
    shift = 1
    while shift < MLSTM_CHUNK:
        up = jnp.maximum(up, jnp.where(row >= shift, pltpu.roll(up, shift, axis=0), NEG))
        down = jnp.maximum(down, jnp.where(row < MLSTM_CHUNK - shift, pltpu.roll(down, tm - shift, axis=0), NEG))
        shift *= 2
    return cum, e, jnp.where(fwd, up, down)


def _is_fwd_lane(shape):
    return lax.broadcasted_iota(jnp.int32, shape, len(shape) - 1) < 3 * M_HEADS


def _mlstm_state_kernel(kf_ref, vtf_ref, cumf_ref, ef_ref, emaxf_ref, kb_ref, vtb_ref, cumb_ref, eb_ref, emaxb_ref,
                        sf_ref, mf_ref, sb_ref, mb_ref, c_ref, m_ref):
    L = ef_ref.shape[0]

    @pl.when(pl.program_id(1) == 0)
    def _():
        c_ref[...] = jnp.zeros_like(c_ref)
        m_ref[...] = jnp.zeros_like(m_ref)

    tot_f = cumf_ref[L - 1:L, :]
    tot_b = cumb_ref[0:1, :]
    wend_f = tot_f + ef_ref[...]
    wend_b = tot_b + eb_ref[...]
    fwd = _is_fwd_lane((1, LANES))
    tot = jnp.where(fwd, tot_f, tot_b)
    m_prev = m_ref[...]
    m_new = jnp.maximum(tot + m_prev, tot + jnp.where(fwd, emaxf_ref[L - 1:L, :], emaxb_ref[0:1, :]))
    dec = jnp.exp2(tot + m_prev - m_new)
    m_ref[...] = m_new
    m_rows = jnp.broadcast_to(m_prev, (8, LANES))
    mf_ref[0] = m_rows
    mb_ref[0] = m_rows
    we_t = (jnp.exp2(wend_f - m_new).T, jnp.exp2(wend_b - m_new).T)
    ones = jnp.ones((MLSTM_VPAD, L), BF16)

    pairs = [(d, hd) for d in range(2) for hd in range(M_HEADS)]
    k_refs, vt_refs, s_refs = (kf_ref, kb_ref), (vtf_ref, vtb_ref), (sf_ref, sb_ref)
    lane = lambda d, hd: (2 + d) * M_HEADS + hd
    wvs = {}
    for d, hd in pairs:
        vext = jnp.concatenate([vt_refs[d][hd, 0], ones], axis=0)
        wvs[d, hd] = (vext.astype(F32) * we_t[d][lane(d, hd):lane(d, hd) + 1, :]).astype(BF16)
    adds = {(d, hd): jnp.dot(wvs[d, hd], k_refs[d][hd, 0], preferred_element_type=F32) for d, hd in pairs}
    for d, hd in pairs:
        cext = c_ref[d * M_HEADS + hd]
        s_refs[d][0, hd] = cext
        c_ref[d * M_HEADS + hd] = dec[:, lane(d, hd):lane(d, hd) + 1] * cext + adds[d, hd]


def _mlstm_state(mk, mvt, cum, e, emax, T):
    _, nblk, blk, _ = mk.shape
    N = nblk * blk
    L = MLSTM_CHUNK
    per = blk // L
    nc = T // L
    B = N // T
    cf = lambda b, c: b * nc + c
    cb = lambda b, c: b * nc + (nc - 1 - c)

    def specs(ci):
        return [pl.BlockSpec((M_HEADS, 1, L, HEAD_DIM), lambda b, c: (0, ci(b, c) // per, ci(b, c) % per, 0)),
                pl.BlockSpec((M_HEADS, 1, HEAD_DIM, L), lambda b, c: (0, ci(b, c) // per, 0, ci(b, c) % per)),
                pl.BlockSpec((L, LANES), lambda b, c: (ci(b, c), 0)),
                pl.BlockSpec((L, LANES), lambda b, c: (ci(b, c), 0)),
                pl.BlockSpec((L, LANES), lambda b, c: (ci(b, c), 0))]

    def outs(ci):
        return [pl.BlockSpec((1, M_HEADS, HEAD_DIM + MLSTM_VPAD, HEAD_DIM), lambda b, c: (ci(b, c), 0, 0, 0)),
                pl.BlockSpec((1, 8, LANES), lambda b, c: (ci(b, c), 0, 0))]

    s_shape = jax.ShapeDtypeStruct((N // L, M_HEADS, HEAD_DIM + MLSTM_VPAD, HEAD_DIM), F32)
    m_shape = jax.ShapeDtypeStruct((N // L, 8, LANES), F32)
    return pl.pallas_call(
        _mlstm_state_kernel,
        grid=(B, nc),
        in_specs=specs(cf) + specs(cb),
        out_specs=tuple(outs(cf) + outs(cb)),
        out_shape=(s_shape, m_shape, s_shape, m_shape),
        scratch_shapes=[pltpu.VMEM((2 * M_HEADS, HEAD_DIM + MLSTM_VPAD, HEAD_DIM), F32), pltpu.VMEM((1, LANES), F32)],
        compiler_params=_cparams(("parallel", "arbitrary")),
        name="mlstm_state",
    )(mk, mvt, cum, e, emax, mk, mvt, cum, e, emax)


def _mlstm_out_kernel(qt_ref, k_ref, vt_ref, mot_ref, cum_ref, e_ref, emax_ref, sf_ref, mf_ref, sb_ref, mb_ref,
                      nw_ref, out_ref):
    L = e_ref.shape[0]
    e = e_ref[...]
    m_in = jnp.where(_is_fwd_lane((1, LANES)), mf_ref[0, 0:1, :], mb_ref[0, 0:1, :])
    cum_t = cum_ref[...].T
    mx_t = jnp.maximum(emax_ref[...], m_in).T

    si = lax.broadcasted_iota(jnp.int32, (L, L), 0)
    li = lax.broadcasted_iota(jnp.int32, (L, L), 1)
    keep = (si <= li, si >= li)
    ones = jnp.ones((MLSTM_VPAD, L), BF16)

    pairs = [(hd, d) for hd in range(M_HEADS) for d in range(2)]
    s_refs = (sf_ref, sb_ref)
    lane = lambda hd, d: (2 + d) * M_HEADS + hd
    qts = [qt_ref[hd, 0] for hd in range(M_HEADS)]
    raws = [jnp.dot(k_ref[hd, 0], qts[hd], preferred_element_type=F32) for hd in range(M_HEADS)]
    carried = {(hd, d): jnp.dot(s_refs[d][0, hd].astype(BF16), qts[hd], preferred_element_type=F32)
               for hd, d in pairs}
    mxs = {(hd, d): mx_t[lane(hd, d):lane(hd, d) + 1, :] for hd, d in pairs}
    ws = {(hd, d): jnp.where(keep[d], raws[hd] * jnp.exp2(e[:, lane(hd, d):lane(hd, d) + 1] - mxs[hd, d]),
                             0.0).astype(BF16) for hd, d in pairs}
    vexts = [jnp.concatenate([vt_ref[hd, 0], ones], axis=0) for hd in range(M_HEADS)]
    nds = {(hd, d): (jnp.exp2(m_in[:, lane(hd, d):lane(hd, d) + 1] - mxs[hd, d]) * carried[hd, d]
                     + jnp.dot(vexts[hd], ws[hd, d], preferred_element_type=F32)) for hd, d in pairs}
    hhs = {(hd, d): nds[hd, d][0:HEAD_DIM] / jnp.maximum(
        jnp.abs(nds[hd, d][HEAD_DIM:HEAD_DIM + 1]),
        jnp.exp2(-(cum_t[lane(hd, d):lane(hd, d) + 1, :] + mxs[hd, d]))) for hd, d in pairs}
    outs = []
    for hd in range(M_HEADS):
        hs = hhs[hd, 0] + hhs[hd, 1]
        hn = hs * lax.rsqrt(jnp.mean(hs * hs, axis=0, keepdims=True) + EPS) * nw_ref[hd]
        mo = mot_ref[hd, 0].astype(F32)
        outs.append(hn / (1.0 + jnp.exp(-mo)))
    for jj in range(M_HEADS // 2):
        o2 = jnp.concatenate([outs[2 * jj], outs[2 * jj + 1]], axis=0)
        out_ref[:, jj * LANES:(jj + 1) * LANES] = o2.T.astype(out_ref.dtype)


def _mlstm_out(mqt, mk, mvt, mot, cum, e, emax, sf, mf, sb, mb, nwb):
    _, nblk, _, blk = mqt.shape
    N = nblk * blk
    L = MLSTM_CHUNK
    per = blk // L
    tspec = pl.BlockSpec((M_HEADS, 1, HEAD_DIM, L), lambda c: (0, c // per, 0, c % per))
    kspec = pl.BlockSpec((M_HEADS, 1, L, HEAD_DIM), lambda c: (0, c // per, c % per, 0))
    sspec = pl.BlockSpec((1, M_HEADS, HEAD_DIM + MLSTM_VPAD, HEAD_DIM), lambda c: (c, 0, 0, 0))
    mspec = pl.BlockSpec((1, 8, LANES), lambda c: (c, 0, 0))
    gspec = pl.BlockSpec((L, LANES), lambda c: (c, 0))
    return pl.pallas_call(
        _mlstm_out_kernel,
        grid=(N // L,),
        in_specs=[tspec, kspec, tspec, tspec, gspec, gspec, gspec, sspec, mspec, sspec, mspec,
                  _const_spec(nwb.shape)],
        out_specs=pl.BlockSpec((L, M_WIDTH), lambda c: (c, 0)),
        out_shape=jax.ShapeDtypeStruct((N, M_WIDTH), BF16),
        compiler_params=_cparams(("parallel",)),
        name="mlstm_out",
    )(mqt, mk, mvt, mot, cum, e, emax, sf, mf, sb, mb, nwb)


def _attn_kernel(qt_ref, k_ref, vt_ref, o_ref, m_ref, acc_ref, s0_ref, s1_ref):
    nkb = k_ref.shape[1]
    tk = k_ref.shape[2]
    tq = qt_ref.shape[3]
    units = [(j, c) for j in range(A_GROUP) for c in range(0, tq, ATTN_QW)]
    ones = jnp.ones((ATTN_VPAD, tk), BF16)

    m_ref[...] = jnp.full_like(m_ref, -jnp.inf)
    acc_ref[...] = jnp.zeros_like(acc_ref)

    def stage(cur, nxt):
        if nxt is not None:
            k = k_ref[0, nxt[0]]
        if cur is not None:
            vt = jnp.concatenate([vt_ref[0, cur[0]], ones], axis=0)
        for u, (j, c) in enumerate(units):
            cols = slice(c, c + ATTN_QW)
            if nxt is not None:
                nxt[1][u] = jnp.dot(k, qt_ref[j, 0, :, cols], preferred_element_type=F32)
            if cur is not None:
                st = cur[1][u]
                m_old = m_ref[j, :, cols]
                m_new = jnp.maximum(m_old, jnp.max(st, axis=0, keepdims=True))
                alpha = jnp.exp2(m_old - m_new)
                p = jnp.exp2(st - m_new)
                acc_ref[j, :, cols] = (alpha * acc_ref[j, :, cols]
                                       + jnp.dot(vt, p.astype(BF16), preferred_element_type=F32))
                m_ref[j, :, cols] = m_new

    stage(None, (0, s0_ref))

    def body(i, carry):
        kb = 2 * i
        stage((kb, s0_ref), (kb + 1, s1_ref))
        stage((kb + 1, s1_ref), (kb + 2, s0_ref))
        return carry

    lax.fori_loop(0, nkb // 2 - 1, body, 0)
    stage((nkb - 2, s0_ref), (nkb - 1, s1_ref))
    stage((nkb - 1, s1_ref), None)

    def head_out(j):
        return acc_ref[j, 0:HEAD_DIM, :] / acc_ref[j, HEAD_DIM:HEAD_DIM + 1, :]

    for jj in range(A_GROUP // 2):
        o2 = jnp.concatenate([head_out(2 * jj), head_out(2 * jj + 1)], axis=0)
        o_ref[:, jj * LANES:(jj + 1) * LANES] = o2.T.astype(o_ref.dtype)


def _attn(qt, k, vt, T):
    _, nblk, _, blk = qt.shape
    N = nblk * blk
    B = N // T
    nq = T // blk
    assert nq % 2 == 0, "attention pipelines key blocks in pairs"
    nu = A_GROUP * (blk // ATTN_QW)
    return pl.pallas_call(
        _attn_kernel,
        grid=(B, A_KV_HEADS, nq),
        in_specs=[
            pl.BlockSpec((A_GROUP, 1, HEAD_DIM, blk), lambda b, g, qi: (g, b * nq + qi, 0, 0)),
            pl.BlockSpec((1, nq, blk, HEAD_DIM), lambda b, g, qi: (g, b, 0, 0)),
            pl.BlockSpec((1, nq, HEAD_DIM, blk), lambda b, g, qi: (g, b, 0, 0)),
        ],
        out_specs=pl.BlockSpec((blk, A_GROUP * HEAD_DIM), lambda b, g, qi: (b * nq + qi, g)),
        out_shape=jax.ShapeDtypeStruct((N, A_WIDTH), BF16),
        scratch_shapes=[pltpu.VMEM((A_GROUP, 1, blk), F32),
                        pltpu.VMEM((A_GROUP, HEAD_DIM + ATTN_VPAD, blk), F32),
                        pltpu.VMEM((nu, blk, ATTN_QW), F32), pltpu.VMEM((nu, blk, ATTN_QW), F32)],
        compiler_params=_cparams(("parallel", "parallel", "arbitrary")),
        name="attn",
    )(qt, k, vt)


def _outproj_kernel(x_ref, m_ref, a_ref, wom_ref, woa_ref, n2_ref, x1_ref, h2_ref):
    x1 = (x_ref[...] + jnp.dot(m_ref[...], wom_ref[...], preferred_element_type=F32)
          + jnp.dot(a_ref[...], woa_ref[...], preferred_element_type=F32))
    x1_ref[...] = x1
    h2 = x1 * lax.rsqrt(jnp.mean(x1 * x1, axis=-1, keepdims=True) + EPS) * n2_ref[...]
    h2_ref[...] = h2.astype(BF16)


def _outproj(x2, m_out, a_out, wom, woa, n2, tm):
    N = x2.shape[0]
    row = lambda i: (i, 0)
    return pl.pallas_call(
        _outproj_kernel,
        grid=(N // tm,),
        in_specs=[pl.BlockSpec((tm, D_MODEL), row), pl.BlockSpec((tm, M_WIDTH), row),
                  pl.BlockSpec((tm, A_WIDTH), row), _const_spec(wom.shape), _const_spec(woa.shape),
                  _const_spec(n2.shape)],
        out_specs=(pl.BlockSpec((tm, D_MODEL), row), pl.BlockSpec((tm, D_MODEL), row)),
        out_shape=(jax.ShapeDtypeStruct((N, D_MODEL), F32), jax.ShapeDtypeStruct((N, D_MODEL), BF16)),
        compiler_params=_cparams(("parallel",)),
        name="outproj",
    )(x2, m_out, a_out, wom, woa, n2)


def _ffn_kernel(hp_ref, hm_ref, hn_ref, x1_ref, wup_ref, cw_ref, cb_ref, wdn_ref, fw_ref, y_ref,
                lhs_ref, u0_ref, u1_ref, acc_ref, *, tiles_per_seq):
    i = pl.program_id(0)
    tm = hm_ref.shape[0]
    H = CONV_HALO
    t = i % tiles_per_seq
    lhs_ref[0:H] = jnp.where(t == 0, jnp.zeros_like(hp_ref), hp_ref[...])
    lhs_ref[H:H + tm] = hm_ref[...]
    lhs_ref[H + tm:H + tm + H] = jnp.where(t == tiles_per_seq - 1, jnp.zeros_like(hn_ref), hn_ref[...])
    acc_ref[...] = jnp.zeros_like(acc_ref)
    nchunk = D_FF // FF_CHUNK

    def up(c, u_ref):
        u_ref[...] = jnp.dot(lhs_ref[...], wup_ref[c], preferred_element_type=F32)

    def gate_down(c, u_ref):
        w = cw_ref[c]
        conv = (u_ref[pl.ds(H - 1, tm), :] * w[0:1] + u_ref[pl.ds(H, tm), :] * w[1:2]
                + u_ref[pl.ds(H + 1, tm), :] * w[2:3] + cb_ref[c])
        a = conv[:, 0:FF_CHUNK]
        gt = conv[:, FF_CHUNK:2 * FF_CHUNK]
        act = (gt / (1.0 + jnp.exp(-gt))) * a
        acc_ref[...] += jnp.dot(act.astype(BF16), wdn_ref[c], preferred_element_type=F32)

    assert nchunk % 2 == 1
    up(0, u0_ref)

    def body(i, carry):
        c = 2 * i
        up(c + 1, u1_ref)
        gate_down(c, u0_ref)
        up(c + 2, u0_ref)
        gate_down(c + 1, u1_ref)
        return carry

    lax.fori_loop(0, nchunk // 2, body, 0)
    gate_down(nchunk - 1, u0_ref)
    x2 = x1_ref[...] + acc_ref[...]
    y_ref[...] = x2 * lax.rsqrt(jnp.mean(x2 * x2, axis=-1, keepdims=True) + EPS) * fw_ref[...]


def _ffn(h2, x1, T, wup, cw, cb, wdn, fw, tm):
    N = h2.shape[0]
    H = CONV_HALO
    tps = T // tm
    r = tm // H
    nblk = N // H
    row = lambda i: (i, 0)
    return pl.pallas_call(
        functools.partial(_ffn_kernel, tiles_per_seq=tps),
        grid=(N // tm,),
        in_specs=[
            pl.BlockSpec((H, D_MODEL), lambda i: (jnp.maximum(i * r - 1, 0), 0)),
            pl.BlockSpec((tm, D_MODEL), row),
            pl.BlockSpec((H, D_MODEL), lambda i: (jnp.minimum((i + 1) * r, nblk - 1), 0)),
            pl.BlockSpec((tm, D_MODEL), row),
            _const_spec(wup.shape), _const_spec(cw.shape), _const_spec(cb.shape), _const_spec(wdn.shape),
            _const_spec(fw.shape),
        ],
        out_specs=pl.BlockSpec((tm, D_MODEL), row),
        out_shape=jax.ShapeDtypeStruct((N, D_MODEL), F32),
        scratch_shapes=[pltpu.VMEM((tm + 2 * H, D_MODEL), BF16), pltpu.VMEM((tm + 2 * H, 2 * FF_CHUNK), F32),
                        pltpu.VMEM((tm + 2 * H, 2 * FF_CHUNK), F32),
                        pltpu.VMEM((tm, D_MODEL), F32)],
        compiler_params=_cparams(("parallel",)),
        name="ffn",
    )(h2, h2, h2, x1, wup, cw, cb, wdn, fw)


def _rope_tables(T):
    rows = T // GRID_W
    row = jnp.repeat(jnp.arange(rows, dtype=F32), GRID_W)
    col = jnp.tile(jnp.arange(GRID_W, dtype=F32), rows)
    nf = HEAD_DIM // 4
    inv = ROPE_THETA ** (-jnp.arange(nf, dtype=F32) / nf)
    ar = row[:, None] * inv
    ac = col[:, None] * inv
    ang = jnp.concatenate([ar, ar, ac, ac], axis=-1)
    cos, sin = jnp.cos(ang), jnp.sin(ang)
    quarter = (np.arange(HEAD_DIM) // nf) % 2
    sa = jnp.where(quarter == 0, -sin, 0.0)
    sb = jnp.where(quarter == 1, sin, 0.0)
    two = lambda a: jnp.concatenate([a, a], axis=-1)
    return two(cos), two(sa), two(sb)


def _prep_weights(w_in, b_gates, mh_norm_w, q_norm_w, k_norm_w, w_out, norm1_w, norm2_w, w_up, conv_w, conv_b,
                  w_down, final_norm_w):
    gate0 = 4 * M_WIDTH
    wm = w_in[:, :gate0].astype(BF16)
    wg = jnp.pad(w_in[:, gate0:gate0 + N_GATES], ((0, 0), (0, LANES - N_GATES))).astype(BF16)
    wa = w_in[:, gate0 + N_GATES:].astype(BF16)
    bg = jnp.pad(b_gates, (0, LANES - N_GATES)).reshape(1, LANES)
    nchunk = D_FF // FF_CHUNK
    wua = w_up[:, :D_FF].reshape(D_MODEL, nchunk, FF_CHUNK)
    wug = w_up[:, D_FF:].reshape(D_MODEL, nchunk, FF_CHUNK)
    wup = jnp.concatenate([wua, wug], axis=-1).transpose(1, 0, 2).astype(BF16)
    cwa = conv_w[:, :D_FF].reshape(3, nchunk, FF_CHUNK)
    cwg = conv_w[:, D_FF:].reshape(3, nchunk, FF_CHUNK)
    cw = jnp.concatenate([cwa, cwg], axis=-1).transpose(1, 0, 2)
    cb = jnp.concatenate([conv_b[:D_FF].reshape(nchunk, 1, FF_CHUNK),
                          conv_b[D_FF:].reshape(nchunk, 1, FF_CHUNK)], axis=-1)
    wdn = w_down.reshape(nchunk, FF_CHUNK, D_MODEL).astype(BF16)
    return dict(
        wm=wm, wg=wg, wa=wa, bg=bg,
        n1=norm1_w.reshape(1, D_MODEL), n2=norm2_w.reshape(1, D_MODEL), fw=final_norm_w.reshape(1, D_MODEL),
        qw2=jnp.tile(q_norm_w, 2).reshape(1, LANES), kw2=jnp.tile(k_norm_w, 2).reshape(1, LANES),
        nw=mh_norm_w.reshape(1, M_WIDTH),
        wom=w_out[:M_WIDTH].astype(BF16), woa=w_out[M_WIDTH:].astype(BF16),
        wup=wup, cw=cw, cb=cb, wdn=wdn,
    )


def _tiles(T):
    return min(512, T // 2)


def _trunk(x, p):
    B, T, _ = x.shape
    N = B * T
    tm = _tiles(T)
    x2 = x.reshape(N, D_MODEL)
    cos, sa, sb = _rope_tables(T)
    blk = np.arange(LANES) // HEAD_DIM
    ones_bd = jnp.asarray(blk[:, None] == blk[None, :], BF16)
    idx = np.arange(tm)
    tri = jnp.asarray((idx[None, :] <= idx[:, None])
                      & (idx[None, :] // MLSTM_CHUNK == idx[:, None] // MLSTM_CHUNK), BF16)

    mq, mk, mv, mo, cum, e, emax, qt, k, vt = _inproj(x2, T, p["n1"], p["wm"], p["wg"], p["wa"], p["bg"], p["qw2"],
                                                      p["kw2"], cos, sa, sb, ones_bd, tri, tm)
    nwb = jnp.broadcast_to(p["nw"].reshape(M_HEADS, HEAD_DIM, 1), (M_HEADS, HEAD_DIM, MLSTM_CHUNK))
    s_f, m_f, s_b, m_b = _mlstm_state(mk, mv, cum, e, emax, T)
    m_out = _mlstm_out(mq, mk, mv, mo, cum, e, emax, s_f, m_f, s_b, m_b, nwb)
    a_out = _attn(qt, k, vt, T)
    x1, h2 = _outproj(x2, m_out, a_out, p["wom"], p["woa"], p["n2"], tm)
    y = _ffn(h2, x1, T, p["wup"], p["cw"], p["cb"], p["wdn"], p["fw"], tm)
    return y.reshape(B, T, D_MODEL)


def kernel(x_prompt, x_sample, w_in, b_gates, mh_norm_w, q_norm_w, k_norm_w, w_out, norm1_w, norm2_w, w_up,
           conv_w, conv_b, w_down, final_norm_w):
    assert w_in.shape[0] == 1, "single-layer trunk"
    p = _prep_weights(w_in[0], b_gates[0], mh_norm_w[0], q_norm_w[0], k_norm_w[0], w_out[0], norm1_w[0],
                      norm2_w[0], w_up[0], conv_w[0], conv_b[0], w_down[0], final_norm_w)
    return (_trunk(x_prompt, p), _trunk(x_sample, p))
```

```python
import functools
import math

import jax
import jax.numpy as jnp
import numpy as np
from jax import lax
from jax.experimental import pallas as pl
from jax.experimental.pallas import tpu as pltpu

F32 = jnp.float32
BF16 = jnp.bfloat16

D_MODEL = 1024
HEAD_DIM = 64
M_HEADS = 8
M_WIDTH = M_HEADS * HEAD_DIM
A_HEADS = 8
A_KV_HEADS = 2
A_GROUP = A_HEADS // A_KV_HEADS
A_WIDTH = A_HEADS * HEAD_DIM
KV_WIDTH = A_KV_HEADS * HEAD_DIM
N_GATES = 4 * M_HEADS
D_FF = 2816
GRID_W = 64
ROPE_THETA = 10000.0
EPS = 1e-6

LANES = 128
VMEM_LIMIT = 56 * 1024 * 1024

MLSTM_CHUNK = 256
MLSTM_VPAD = 16
FF_CHUNK = 256
CONV_HALO = 16
NEG = -1e30
ATTN_QW = 256
ATTN_VPAD = 16


def _cparams(sem):
    return pltpu.CompilerParams(dimension_semantics=sem, vmem_limit_bytes=VMEM_LIMIT)


def _const_spec(shape):
    nd = len(shape)
    return pl.BlockSpec(shape, lambda *_: (0,) * nd, pipeline_mode=pl.Buffered(1))


def _inproj_kernel(x_ref, n1_ref, wm_ref, wg_ref, wa_ref, bg_ref, qw_ref, kw_ref, rrow_ref, rcol_ref,
                   ones_ref, tri_ref, mq_ref, mk_ref, mv_ref, mo_ref, cum_ref, e_ref, emax_ref, qt_ref, k_ref, vt_ref):
    x = x_ref[...]
    h = x * lax.rsqrt(jnp.mean(x * x, axis=-1, keepdims=True) + EPS) * n1_ref[...]
    h = h.astype(BF16)

    g = jnp.dot(h, wg_ref[...], preferred_element_type=F32) + bg_ref[...]
    cum_ref[...], e_ref[...], emax_ref[...] = _gate_sums(g, tri_ref[...])

    pa = jnp.dot(h, wa_ref[...], preferred_element_type=F32)
    tm = x.shape[0]
    rrow = rrow_ref[0]
    cos, sa, sb = (jnp.broadcast_to(rrow[i][:, None, :], (tm // GRID_W, GRID_W, LANES)).reshape(tm, LANES)
                   + rcol_ref[i] for i in range(3))
    ones_bd = ones_ref[...]

    def norm_rope(y, w):
        sq = y * y
        hi = sq.astype(BF16)
        lo = (sq - hi.astype(F32)).astype(BF16)
        ss = (jnp.dot(hi, ones_bd, preferred_element_type=F32)
              + jnp.dot(lo, ones_bd, preferred_element_type=F32))
        yn = y * lax.rsqrt(ss * (1.0 / HEAD_DIM) + EPS) * w
        up = pltpu.roll(yn, LANES - HEAD_DIM // 4, axis=1)
        dn = pltpu.roll(yn, HEAD_DIM // 4, axis=1)
        return yn * cos + up * sa + dn * sb

    qscale = (HEAD_DIM ** -0.5) * math.log2(math.e)
    for c in range(A_WIDTH // LANES):
        q2 = norm_rope(pa[:, c * LANES:(c + 1) * LANES], qw_ref[...]) * qscale
        q2t = q2.T.astype(BF16)
        qt_ref[2 * c, 0] = q2t[0:HEAD_DIM]
        qt_ref[2 * c + 1, 0] = q2t[HEAD_DIM:2 * HEAD_DIM]
    k2 = norm_rope(pa[:, A_WIDTH:A_WIDTH + KV_WIDTH], kw_ref[...]).astype(BF16)
    k_ref[0, 0] = k2[:, 0:HEAD_DIM]
    k_ref[1, 0] = k2[:, HEAD_DIM:2 * HEAD_DIM]
    v2t = pa[:, A_WIDTH + KV_WIDTH:A_WIDTH + 2 * KV_WIDTH].T.astype(BF16)
    vt_ref[0, 0] = v2t[0:HEAD_DIM]
    vt_ref[1, 0] = v2t[HEAD_DIM:2 * HEAD_DIM]

    pm = jnp.dot(h, wm_ref[...], preferred_element_type=F32)

    def heads_t(cols0, ref):
        for c in range(M_WIDTH // LANES):
            t2 = pm[:, cols0 + c * LANES:cols0 + (c + 1) * LANES].T.astype(BF16)
            ref[2 * c, 0] = t2[0:HEAD_DIM]
            ref[2 * c + 1, 0] = t2[HEAD_DIM:2 * HEAD_DIM]

    heads_t(0, mq_ref)
    heads_t(2 * M_WIDTH, mv_ref)
    heads_t(3 * M_WIDTH, mo_ref)
    for hd in range(M_HEADS):
        c0 = M_WIDTH + hd * HEAD_DIM
        mk_ref[hd, 0] = (pm[:, c0:c0 + HEAD_DIM] * (HEAD_DIM ** -0.5)).astype(BF16)


def _inproj(x2, T, n1, wm, wg, wa, bg, qw2, kw2, rrow, rcol, ones_bd, tri, tm):
    N = x2.shape[0]
    tps = T // tm
    row = lambda i: (i, 0)
    out_shape = (
        jax.ShapeDtypeStruct((M_HEADS, N // tm, HEAD_DIM, tm), BF16),
        jax.ShapeDtypeStruct((M_HEADS, N // tm, tm, HEAD_DIM), BF16),
        jax.ShapeDtypeStruct((M_HEADS, N // tm, HEAD_DIM, tm), BF16),
        jax.ShapeDtypeStruct((M_HEADS, N // tm, HEAD_DIM, tm), BF16),
        jax.ShapeDtypeStruct((N, LANES), F32), jax.ShapeDtypeStruct((N, LANES), F32),
        jax.ShapeDtypeStruct((N, LANES), F32),
        jax.ShapeDtypeStruct((A_HEADS, N // tm, HEAD_DIM, tm), BF16),
        jax.ShapeDtypeStruct((A_KV_HEADS, N // tm, tm, HEAD_DIM), BF16),
        jax.ShapeDtypeStruct((A_KV_HEADS, N // tm, HEAD_DIM, tm), BF16),
    )
    return pl.pallas_call(
        _inproj_kernel,
        grid=(N // tm,),
        in_specs=[
            pl.BlockSpec((tm, D_MODEL), row),
            _const_spec(n1.shape), _const_spec(wm.shape), _const_spec(wg.shape), _const_spec(wa.shape),
            _const_spec(bg.shape), _const_spec(qw2.shape), _const_spec(kw2.shape),
            pl.BlockSpec((1,) + rrow.shape[1:], lambda i: (i % tps, 0, 0, 0)), _const_spec(rcol.shape),
            _const_spec(ones_bd.shape), _const_spec(tri.shape),
        ],
        out_specs=(
            pl.BlockSpec((M_HEADS, 1, HEAD_DIM, tm), lambda i: (0, i, 0, 0)),
            pl.BlockSpec((M_HEADS, 1, tm, HEAD_DIM), lambda i: (0, i, 0, 0)),
            pl.BlockSpec((M_HEADS, 1, HEAD_DIM, tm), lambda i: (0, i, 0, 0)),
            pl.BlockSpec((M_HEADS, 1, HEAD_DIM, tm), lambda i: (0, i, 0, 0)),
            pl.BlockSpec((tm, LANES), row), pl.BlockSpec((tm, LANES), row), pl.BlockSpec((tm, LANES), row),
            pl.BlockSpec((A_HEADS, 1, HEAD_DIM, tm), lambda i: (0, i, 0, 0)),
            pl.BlockSpec((A_KV_HEADS, 1, tm, HEAD_DIM), lambda i: (0, i, 0, 0)),
            pl.BlockSpec((A_KV_HEADS, 1, HEAD_DIM, tm), lambda i: (0, i, 0, 0)),
        ),
        out_shape=out_shape,
        compiler_params=_cparams(("parallel",)),
        name="inproj",
    )(x2, n1, wm, wg, wa, bg, qw2, kw2, rrow, rcol, ones_bd, tri)


def _log_sigmoid(x):
    return jnp.minimum(x, 0.0) - jnp.log1p(jnp.exp(-jnp.abs(x)))


def _split3(x):
    a = x.astype(BF16)
    r = x - a.astype(F32)
    b = r.astype(BF16)
    c = (r - b.astype(F32)).astype(BF16)
    return a, b, c


def _gate_sums(g, tri):
    tm = g.shape[0]
    logf = _log_sigmoid(g) * math.log2(math.e)
    a, b, c = _split3(logf)
    pre = (jnp.dot(tri, a, preferred_element_type=F32) + jnp.dot(tri, b, preferred_element_type=F32)
           + jnp.dot(tri, c, preferred_element_type=F32))
    tot = jnp.concatenate([jnp.broadcast_to(pre[r + MLSTM_CHUNK - 1:r + MLSTM_CHUNK], (MLSTM_CHUNK, LANES))
                           for r in range(0, tm, MLSTM_CHUNK)], axis=0)
    fwd = _is_fwd_lane((tm, LANES))
    cum = jnp.where(fwd, pre, tot - pre + logf)
    e = pltpu.roll(g, 2 * M_HEADS, axis=1) * math.log2(math.e) - cum

    row = lax.broadcasted_iota(jnp.int32, (tm, LANES), 0) & (MLSTM_CHUNK - 1)
    up, down = e, e
    shift = 1
    while shift < MLSTM_CHUNK:
        up = jnp.maximum(up, jnp.where(row >= shift, pltpu.roll(up, shift, axis=0), NEG))
        down = jnp.maximum(down, jnp.where(row < MLSTM_CHUNK - shift, pltpu.roll(down, tm - shift, axis=0), NEG))
        shift *= 2
    return cum, e, jnp.where(fwd, up, down)


def _is_fwd_lane(shape):
    return lax.broadcasted_iota(jnp.int32, shape, len(shape) - 1) < 3 * M_HEADS


def _mlstm_state_kernel(kf_ref, vtf_ref, cumf_ref, ef_ref, emaxf_ref, kb_ref, vtb_ref, cumb_ref, eb_ref, emaxb_ref,
                        sf_ref, mf_ref, sb_ref, mb_ref, c_ref, m_ref):
    L = MLSTM_CHUNK
    per = ef_ref.shape[0] // L

    @pl.when(pl.program_id(1) == 0)
    def _():
        c_ref[...] = jnp.zeros_like(c_ref)
        m_ref[...] = jnp.zeros_like(m_ref)

    fwd = _is_fwd_lane((1, LANES))
    ones = jnp.ones((MLSTM_VPAD, L), BF16)
    pairs = [(d, hd) for d in range(2) for hd in range(M_HEADS)]
    k_refs, vt_refs, s_refs = (kf_ref, kb_ref), (vtf_ref, vtb_ref), (sf_ref, sb_ref)
    lane = lambda d, hd: (2 + d) * M_HEADS + hd

    for j in range(per):
        rows = (slice(j * L, (j + 1) * L), slice((per - 1 - j) * L, (per - j) * L))
        end_f, end_b = rows[0].stop - 1, rows[1].start
        tot_f = cumf_ref[end_f:end_f + 1, :]
        tot_b = cumb_ref[end_b:end_b + 1, :]
        wend_f = tot_f + ef_ref[rows[0], :]
        wend_b = tot_b + eb_ref[rows[1], :]
        tot = jnp.where(fwd, tot_f, tot_b)
        m_prev = m_ref[...]
        m_new = jnp.maximum(tot + m_prev,
                            tot + jnp.where(fwd, emaxf_ref[end_f:end_f + 1, :], emaxb_ref[end_b:end_b + 1, :]))
        dec = jnp.exp2(tot + m_prev - m_new)
        m_ref[...] = m_new
        m_rows = jnp.broadcast_to(m_prev, (8, LANES))
        mf_ref[j] = m_rows
        mb_ref[per - 1 - j] = m_rows
        we_t = (jnp.exp2(wend_f - m_new).T, jnp.exp2(wend_b - m_new).T)

        wvs = {}
        for d, hd in pairs:
            vext = jnp.concatenate([vt_refs[d][hd, 0, :, rows[d]], ones], axis=0)
            wvs[d, hd] = (vext.astype(F32) * we_t[d][lane(d, hd):lane(d, hd) + 1, :]).astype(BF16)
        adds = {(d, hd): jnp.dot(wvs[d, hd], k_refs[d][hd, 0, rows[d], :], preferred_element_type=F32)
                for d, hd in pairs}
        for d, hd in pairs:
            cext = c_ref[d * M_HEADS + hd]
            s_refs[d][(j, per - 1 - j)[d], hd] = cext
            c_ref[d * M_HEADS + hd] = dec[:, lane(d, hd):lane(d, hd) + 1] * cext + adds[d, hd]


def _mlstm_state(mk, mvt, cum, e, emax, T):
    _, nblk, blk, _ = mk.shape
    N = nblk * blk
    L = MLSTM_CHUNK
    per = blk // L
    nt = T // blk
    B = N // T
    tf = lambda b, t: b * nt + t
    tb = lambda b, t: b * nt + (nt - 1 - t)

    def specs(ti):
        return [pl.BlockSpec((M_HEADS, 1, blk, HEAD_DIM), lambda b, t: (0, ti(b, t), 0, 0)),
                pl.BlockSpec((M_HEADS, 1, HEAD_DIM, blk), lambda b, t: (0, ti(b, t), 0, 0)),
                pl.BlockSpec((blk, LANES), lambda b, t: (ti(b, t), 0)),
                pl.BlockSpec((blk, LANES), lambda b, t: (ti(b, t), 0)),
                pl.BlockSpec((blk, LANES), lambda b, t: (ti(b, t), 0))]

    def outs(ti):
        return [pl.BlockSpec((per, M_HEADS, HEAD_DIM + MLSTM_VPAD, HEAD_DIM), lambda b, t: (ti(b, t), 0, 0, 0)),
                pl.BlockSpec((per, 8, LANES), lambda b, t: (ti(b, t), 0, 0))]

    s_shape = jax.ShapeDtypeStruct((N // L, M_HEADS, HEAD_DIM + MLSTM_VPAD, HEAD_DIM), F32)
    m_shape = jax.ShapeDtypeStruct((N // L, 8, LANES), F32)
    return pl.pallas_call(
        _mlstm_state_kernel,
        grid=(B, nt),
        in_specs=specs(tf) + specs(tb),
        out_specs=tuple(outs(tf) + outs(tb)),
        out_shape=(s_shape, m_shape, s_shape, m_shape),
        scratch_shapes=[pltpu.VMEM((2 * M_HEADS, HEAD_DIM + MLSTM_VPAD, HEAD_DIM), F32), pltpu.VMEM((1, LANES), F32)],
        compiler_params=_cparams(("parallel", "arbitrary")),
        name="mlstm_state",
    )(mk, mvt, cum, e, emax, mk, mvt, cum, e, emax)


def _mlstm_out_kernel(qt_ref, k_ref, vt_ref, mot_ref, cum_ref, e_ref, emax_ref, sf_ref, mf_ref, sb_ref, mb_ref,
                      nw_ref, out_ref):
    L = e_ref.shape[0]
    e = e_ref[...]
    m_in = jnp.where(_is_fwd_lane((1, LANES)), mf_ref[0, 0:1, :], mb_ref[0, 0:1, :])
    cum_t = cum_ref[...].T
    mx_t = jnp.maximum(emax_ref[...], m_in).T

    si = lax.broadcasted_iota(jnp.int32, (L, L), 0)
    li = lax.broadcasted_iota(jnp.int32, (L, L), 1)
    keep = (si <= li, si >= li)
    ones = jnp.ones((MLSTM_VPAD, L), BF16)

    pairs = [(hd, d) for hd in range(M_HEADS) for d in range(2)]
    s_refs = (sf_ref, sb_ref)
    lane = lambda hd, d: (2 + d) * M_HEADS + hd
    qts = [qt_ref[hd, 0] for hd in range(M_HEADS)]
    raws = [jnp.dot(k_ref[hd, 0], qts[hd], preferred_element_type=F32) for hd in range(M_HEADS)]
    carried = {(hd, d): jnp.dot(s_refs[d][0, hd].astype(BF16), qts[hd], preferred_element_type=F32)
               for hd, d in pairs}
    mxs = {(hd, d): mx_t[lane(hd, d):lane(hd, d) + 1, :] for hd, d in pairs}
    ws = {(hd, d): jnp.where(keep[d], raws[hd] * jnp.exp2(e[:, lane(hd, d):lane(hd, d) + 1] - mxs[hd, d]),
                             0.0).astype(BF16) for hd, d in pairs}
    vexts = [jnp.concatenate([vt_ref[hd, 0], ones], axis=0) for hd in range(M_HEADS)]
    nds = {(hd, d): (jnp.exp2(m_in[:, lane(hd, d):lane(hd, d) + 1] - mxs[hd, d]) * carried[hd, d]
                     + jnp.dot(vexts[hd], ws[hd, d], preferred_element_type=F32)) for hd, d in pairs}
    hhs = {(hd, d): nds[hd, d][0:HEAD_DIM] / jnp.maximum(
        jnp.abs(nds[hd, d][HEAD_DIM:HEAD_DIM + 1]),
        jnp.exp2(-(cum_t[lane(hd, d):lane(hd, d) + 1, :] + mxs[hd, d]))) for hd, d in pairs}
    outs = []
    for hd in range(M_HEADS):
        hs = hhs[hd, 0] + hhs[hd, 1]
        hn = hs * lax.rsqrt(jnp.mean(hs * hs, axis=0, keepdims=True) + EPS) * nw_ref[hd]
        mo = mot_ref[hd, 0].astype(F32)
        outs.append(hn / (1.0 + jnp.exp(-mo)))
    for jj in range(M_HEADS // 2):
        o2 = jnp.concatenate([outs[2 * jj], outs[2 * jj + 1]], axis=0)
        out_ref[:, jj * LANES:(jj + 1) * LANES] = o2.T.astype(out_ref.dtype)


def _mlstm_out(mqt, mk, mvt, mot, cum, e, emax, sf, mf, sb, mb, nwb):
    _, nblk, _, blk = mqt.shape
    N = nblk * blk
    L = MLSTM_CHUNK
    per = blk // L
    tspec = pl.BlockSpec((M_HEADS, 1, HEAD_DIM, L), lambda c: (0, c // per, 0, c % per))
    kspec = pl.BlockSpec((M_HEADS, 1, L, HEAD_DIM), lambda c: (0, c // per, c % per, 0))
    sspec = pl.BlockSpec((1, M_HEADS, HEAD_DIM + MLSTM_VPAD, HEAD_DIM), lambda c: (c, 0, 0, 0))
    mspec = pl.BlockSpec((1, 8, LANES), lambda c: (c, 0, 0))
    gspec = pl.BlockSpec((L, LANES), lambda c: (c, 0))
    return pl.pallas_call(
        _mlstm_out_kernel,
        grid=(N // L,),
        in_specs=[tspec, kspec, tspec, tspec, gspec, gspec, gspec, sspec, mspec, sspec, mspec,
                  _const_spec(nwb.shape)],
        out_specs=pl.BlockSpec((L, M_WIDTH), lambda c: (c, 0)),
        out_shape=jax.ShapeDtypeStruct((N, M_WIDTH), BF16),
        compiler_params=_cparams(("parallel",)),
        name="mlstm_out",
    )(mqt, mk, mvt, mot, cum, e, emax, sf, mf, sb, mb, nwb)


def _attn_kernel(qt_ref, k_ref, vt_ref, o_ref, m_ref, acc_ref, s0_ref, s1_ref):
    nkb = k_ref.shape[1]
    tk = k_ref.shape[2]
    tq = qt_ref.shape[3]
    units = [(j, c) for j in range(A_GROUP) for c in range(0, tq, ATTN_QW)]
    ones = jnp.ones((ATTN_VPAD, tk), BF16)

    m_ref[...] = jnp.full_like(m_ref, -jnp.inf)
    acc_ref[...] = jnp.zeros_like(acc_ref)

    def stage(cur, nxt):
        if nxt is not None:
            k = k_ref[0, nxt[0]]
        if cur is not None:
            vt = jnp.concatenate([vt_ref[0, cur[0]], ones], axis=0)
        for u, (j, c) in enumerate(units):
            cols = slice(c, c + ATTN_QW)
            if nxt is not None:
                nxt[1][u] = jnp.dot(k, qt_ref[j, 0, :, cols], preferred_element_type=F32)
            if cur is not None:
                st = cur[1][u]
                m_old = m_ref[j, :, cols]
                m_new = jnp.maximum(m_old, jnp.max(st, axis=0, keepdims=True))
                alpha = jnp.exp2(m_old - m_new)
                p = jnp.exp2(st - m_new)
                acc_ref[j, :, cols] = (alpha * acc_ref[j, :, cols]
                                       + jnp.dot(vt, p.astype(BF16), preferred_element_type=F32))
                m_ref[j, :, cols] = m_new

    stage(None, (0, s0_ref))

    def body(i, carry):
        kb = 2 * i
        stage((kb, s0_ref), (kb + 1, s1_ref))
        stage((kb + 1, s1_ref), (kb + 2, s0_ref))
        return carry

    lax.fori_loop(0, nkb // 2 - 1, body, 0)
    stage((nkb - 2, s0_ref), (nkb - 1, s1_ref))
    stage((nkb - 1, s1_ref), None)

    def head_out(j):
        return acc_ref[j, 0:HEAD_DIM, :] / acc_ref[j, HEAD_DIM:HEAD_DIM + 1, :]

    for jj in range(A_GROUP // 2):
        o2 = jnp.concatenate([head_out(2 * jj), head_out(2 * jj + 1)], axis=0)
        o_ref[:, jj * LANES:(jj + 1) * LANES] = o2.T.astype(o_ref.dtype)


def _attn(qt, k, vt, T):
    _, nblk, _, blk = qt.shape
    N = nblk * blk
    B = N // T
    nq = T // blk
    assert nq % 2 == 0, "attention pipelines key blocks in pairs"
    nu = A_GROUP * (blk // ATTN_QW)
    return pl.pallas_call(
        _attn_kernel,
        grid=(B, A_KV_HEADS, nq),
        in_specs=[
            pl.BlockSpec((A_GROUP, 1, HEAD_DIM, blk), lambda b, g, qi: (g, b * nq + qi, 0, 0)),
            pl.BlockSpec((1, nq, blk, HEAD_DIM), lambda b, g, qi: (g, b, 0, 0)),
            pl.BlockSpec((1, nq, HEAD_DIM, blk), lambda b, g, qi: (g, b, 0, 0)),
        ],
        out_specs=pl.BlockSpec((blk, A_GROUP * HEAD_DIM), lambda b, g, qi: (b * nq + qi, g)),
        out_shape=jax.ShapeDtypeStruct((N, A_WIDTH), BF16),
        scratch_shapes=[pltpu.VMEM((A_GROUP, 1, blk), F32),
                        pltpu.VMEM((A_GROUP, HEAD_DIM + ATTN_VPAD, blk), F32),
                        pltpu.VMEM((nu, blk, ATTN_QW), F32), pltpu.VMEM((nu, blk, ATTN_QW), F32)],
        compiler_params=_cparams(("parallel", "parallel", "arbitrary")),
        name="attn",
    )(qt, k, vt)


def _outproj_kernel(x_ref, m_ref, a_ref, wom_ref, woa_ref, n2_ref, x1_ref, h2_ref):
    x1 = (x_ref[...] + jnp.dot(m_ref[...], wom_ref[...], preferred_element_type=F32)
          + jnp.dot(a_ref[...], woa_ref[...], preferred_element_type=F32))
    x1_ref[...] = x1
    h2 = x1 * lax.rsqrt(jnp.mean(x1 * x1, axis=-1, keepdims=True) + EPS) * n2_ref[...]
    h2_ref[...] = h2.astype(BF16)


def _outproj(x2, m_out, a_out, wom, woa, n2, tm):
    N = x2.shape[0]
    row = lambda i: (i, 0)
    return pl.pallas_call(
        _outproj_kernel,
        grid=(N // tm,),
        in_specs=[pl.BlockSpec((tm, D_MODEL), row), pl.BlockSpec((tm, M_WIDTH), row),
                  pl.BlockSpec((tm, A_WIDTH), row), _const_spec(wom.shape), _const_spec(woa.shape),
                  _const_spec(n2.shape)],
        out_specs=(pl.BlockSpec((tm, D_MODEL), row), pl.BlockSpec((tm, D_MODEL), row)),
        out_shape=(jax.ShapeDtypeStruct((N, D_MODEL), F32), jax.ShapeDtypeStruct((N, D_MODEL), BF16)),
        compiler_params=_cparams(("parallel",)),
        name="outproj",
    )(x2, m_out, a_out, wom, woa, n2)


def _ffn_kernel(hp_ref, hm_ref, hn_ref, x1_ref, wup_ref, cw_ref, cb_ref, wdn_ref, fw_ref, y_ref,
                lhs_ref, u0_ref, u1_ref, acc_ref, *, tiles_per_seq):
    i = pl.program_id(0)
    tm = hm_ref.shape[0]
    H = CONV_HALO
    t = i % tiles_per_seq
    lhs_ref[0:H] = jnp.where(t == 0, jnp.zeros_like(hp_ref), hp_ref[...])
    lhs_ref[H:H + tm] = hm_ref[...]
    lhs_ref[H + tm:H + tm + H] = jnp.where(t == tiles_per_seq - 1, jnp.zeros_like(hn_ref), hn_ref[...])
    acc_ref[...] = jnp.zeros_like(acc_ref)
    nchunk = D_FF // FF_CHUNK

    def up(c, u_ref):
        u_ref[...] = jnp.dot(lhs_ref[...], wup_ref[c], preferred_element_type=F32)

    def gate_down(c, u_ref):
        w = cw_ref[c]
        conv = (u_ref[pl.ds(H - 1, tm), :] * w[0:1] + u_ref[pl.ds(H, tm), :] * w[1:2]
                + u_ref[pl.ds(H + 1, tm), :] * w[2:3] + cb_ref[c])
        a = conv[:, 0:FF_CHUNK]
        gt = conv[:, FF_CHUNK:2 * FF_CHUNK]
        act = (gt / (1.0 + jnp.exp(-gt))) * a
        acc_ref[...] += jnp.dot(act.astype(BF16), wdn_ref[c], preferred_element_type=F32)

    assert nchunk % 2 == 1
    up(0, u0_ref)

    def body(i, carry):
        c = 2 * i
        up(c + 1, u1_ref)
        gate_down(c, u0_ref)
        up(c + 2, u0_ref)
        gate_down(c + 1, u1_ref)
        return carry

    lax.fori_loop(0, nchunk // 2, body, 0)
    gate_down(nchunk - 1, u0_ref)
    x2 = x1_ref[...] + acc_ref[...]
    y_ref[...] = x2 * lax.rsqrt(jnp.mean(x2 * x2, axis=-1, keepdims=True) + EPS) * fw_ref[...]


def _ffn(h2, x1, T, wup, cw, cb, wdn, fw, tm):
    N = h2.shape[0]
    H = CONV_HALO
    tps = T // tm
    r = tm // H
    nblk = N // H
    row = lambda i: (i, 0)
    return pl.pallas_call(
        functools.partial(_ffn_kernel, tiles_per_seq=tps),
        grid=(N // tm,),
        in_specs=[
            pl.BlockSpec((H, D_MODEL), lambda i: (jnp.maximum(i * r - 1, 0), 0)),
            pl.BlockSpec((tm, D_MODEL), row),
            pl.BlockSpec((H, D_MODEL), lambda i: (jnp.minimum((i + 1) * r, nblk - 1), 0)),
            pl.BlockSpec((tm, D_MODEL), row),
            _const_spec(wup.shape), _const_spec(cw.shape), _const_spec(cb.shape), _const_spec(wdn.shape),
            _const_spec(fw.shape),
        ],
        out_specs=pl.BlockSpec((tm, D_MODEL), row),
        out_shape=jax.ShapeDtypeStruct((N, D_MODEL), F32),
        scratch_shapes=[pltpu.VMEM((tm + 2 * H, D_MODEL), BF16), pltpu.VMEM((tm + 2 * H, 2 * FF_CHUNK), F32),
                        pltpu.VMEM((tm + 2 * H, 2 * FF_CHUNK), F32),
                        pltpu.VMEM((tm, D_MODEL), F32)],
        compiler_params=_cparams(("parallel",)),
        name="ffn",
    )(h2, h2, h2, x1, wup, cw, cb, wdn, fw)


def _rope_tables(T, tm):
    nf = HEAD_DIM // 4
    inv = ROPE_THETA ** (-jnp.arange(nf, dtype=F32) / nf)
    zero = jnp.zeros((1, 2 * nf), F32)

    def parts(pos, row_part):
        ang = pos[:, None] * inv
        cos, sin = jnp.cos(ang), jnp.sin(ang)
        z = jnp.broadcast_to(zero, (pos.shape[0], 2 * nf))
        halves = lambda a, b: jnp.concatenate([a, b, z] if row_part else [z, a, b], axis=-1)
        tabs = (halves(cos, cos), halves(-sin, 0.0 * sin), halves(0.0 * sin, sin))
        return jnp.stack([jnp.concatenate([t, t], axis=-1) for t in tabs])

    rrow = parts(jnp.arange(T // GRID_W, dtype=F32), True)
    rrow = rrow.reshape(3, T // tm, tm // GRID_W, LANES).transpose(1, 0, 2, 3)
    rcol = parts(jnp.arange(GRID_W, dtype=F32), False)
    return rrow, jnp.tile(rcol, (1, tm // GRID_W, 1))


def _prep_weights(w_in, b_gates, mh_norm_w, q_norm_w, k_norm_w, w_out, norm1_w, norm2_w, w_up, conv_w, conv_b,
                  w_down, final_norm_w):
    gate0 = 4 * M_WIDTH
    wm = w_in[:, :gate0].astype(BF16)
    wg = jnp.pad(w_in[:, gate0:gate0 + N_GATES], ((0, 0), (0, LANES - N_GATES))).astype(BF16)
    wa = w_in[:, gate0 + N_GATES:].astype(BF16)
    bg = jnp.pad(b_gates, (0, LANES - N_GATES)).reshape(1, LANES)
    nchunk = D_FF // FF_CHUNK
    wua = w_up[:, :D_FF].reshape(D_MODEL, nchunk, FF_CHUNK)
    wug = w_up[:, D_FF:].reshape(D_MODEL, nchunk, FF_CHUNK)
    wup = jnp.concatenate([wua, wug], axis=-1).transpose(1, 0, 2).astype(BF16)
    cwa = conv_w[:, :D_FF].reshape(3, nchunk, FF_CHUNK)
    cwg = conv_w[:, D_FF:].reshape(3, nchunk, FF_CHUNK)
    cw = jnp.concatenate([cwa, cwg], axis=-1).transpose(1, 0, 2)
    cb = jnp.concatenate([conv_b[:D_FF].reshape(nchunk, 1, FF_CHUNK),
                          conv_b[D_FF:].reshape(nchunk, 1, FF_CHUNK)], axis=-1)
    wdn = w_down.reshape(nchunk, FF_CHUNK, D_MODEL).astype(BF16)
    return dict(
        wm=wm, wg=wg, wa=wa, bg=bg,
        n1=norm1_w.reshape(1, D_MODEL), n2=norm2_w.reshape(1, D_MODEL), fw=final_norm_w.reshape(1, D_MODEL),
        qw2=jnp.tile(q_norm_w, 2).reshape(1, LANES), kw2=jnp.tile(k_norm_w, 2).reshape(1, LANES),
        nw=mh_norm_w.reshape(1, M_WIDTH),
        wom=w_out[:M_WIDTH].astype(BF16), woa=w_out[M_WIDTH:].astype(BF16),
        wup=wup, cw=cw, cb=cb, wdn=wdn,
    )


def _tiles(T):
    return min(512, T // 2)


def _trunk(x, p):
    B, T, _ = x.shape
    N = B * T
    tm = _tiles(T)
    x2 = x.reshape(N, D_MODEL)
    rrow, rcol = _rope_tables(T, tm)
    blk = np.arange(LANES) // HEAD_DIM
    ones_bd = jnp.asarray(blk[:, None] == blk[None, :], BF16)
    idx = np.arange(tm)
    tri = jnp.asarray((idx[None, :] <= idx[:, None])
                      & (idx[None, :] // MLSTM_CHUNK == idx[:, None] // MLSTM_CHUNK), BF16)

    mq, mk, mv, mo, cum, e, emax, qt, k, vt = _inproj(x2, T, p["n1"], p["wm"], p["wg"], p["wa"], p["bg"], p["qw2"],
                                                      p["kw2"], rrow, rcol, ones_bd, tri, tm)
    nwb = jnp.broadcast_to(p["nw"].reshape(M_HEADS, HEAD_DIM, 1), (M_HEADS, HEAD_DIM, MLSTM_CHUNK))
    s_f, m_f, s_b, m_b = _mlstm_state(mk, mv, cum, e, emax, T)
    m_out = _mlstm_out(mq, mk, mv, mo, cum, e, emax, s_f, m_f, s_b, m_b, nwb)
    a_out = _attn(qt, k, vt, T)
    x1, h2 = _outproj(x2, m_out, a_out, p["wom"], p["woa"], p["n2"], tm)
    y = _ffn(h2, x1, T, p["wup"], p["cw"], p["cb"], p["wdn"], p["fw"], tm)
    return y.reshape(B, T, D_MODEL)


def kernel(x_prompt, x_sample, w_in, b_gates, mh_norm_w, q_norm_w, k_norm_w, w_out, norm1_w, norm2_w, w_up,
           conv_w, conv_b, w_down, final_norm_w):
    assert w_in.shape[0] == 1, "single-layer trunk"
    p = _prep_weights(w_in[0], b_gates[0], mh_norm_w[0], q_norm_w[0], k_norm_w[0], w_out[0], norm1_w[0],
                      norm2_w[0], w_up[0], conv_w[0], conv_b[0], w_down[0], final_norm_w)
    return (_trunk(x_prompt, p), _trunk(x_sample, p))
```

```python
import functools
import math

import jax
import jax.numpy as jnp
import numpy as np
from jax import lax
from jax.experimental import pallas as pl
from jax.experimental.pallas import tpu as pltpu

F32 = jnp.float32
BF16 = jnp.bfloat16

D_MODEL = 1024
HEAD_DIM = 64
M_HEADS = 8
M_WIDTH = M_HEADS * HEAD_DIM
A_HEADS = 8
A_KV_HEADS = 2
A_GROUP = A_HEADS // A_KV_HEADS
A_WIDTH = A_HEADS * HEAD_DIM
KV_WIDTH = A_KV_HEADS * HEAD_DIM
N_GATES = 4 * M_HEADS
D_FF = 2816
GRID_W = 64
ROPE_THETA = 10000.0
EPS = 1e-6

LANES = 128
VMEM_LIMIT = 56 * 1024 * 1024

MLSTM_CHUNK = 256
MLSTM_VPAD = 16
FF_CHUNK = 256
CONV_HALO = 16
NEG = -1e30
ATTN_QW = 256
ATTN_VPAD = 16


def _cparams(sem):
    return pltpu.CompilerParams(dimension_semantics=sem, vmem_limit_bytes=VMEM_LIMIT)


def _const_spec(shape):
    nd = len(shape)
    return pl.BlockSpec(shape, lambda *_: (0,) * nd, pipeline_mode=pl.Buffered(1))


def _inproj_kernel(x_ref, n1_ref, wm_ref, wg_ref, wa_ref, bg_ref, qw_ref, kw_ref, rrow_ref, rcol_ref,
                   ones_ref, tri_ref, mq_ref, mk_ref, mv_ref, mo_ref, cum_ref, e_ref, emax_ref, qt_ref, k_ref, vt_ref):
    x = x_ref[...]
    h = x * lax.rsqrt(jnp.mean(x * x, axis=-1, keepdims=True) + EPS) * n1_ref[...]
    h = h.astype(BF16)

    g = jnp.dot(h, wg_ref[...], preferred_element_type=F32) + bg_ref[...]
    cum_ref[...], e_ref[...], emax_ref[...] = _gate_sums(g, tri_ref[...])

    pa = jnp.dot(h, wa_ref[...], preferred_element_type=F32)
    tm = x.shape[0]
    rrow = rrow_ref[0]
    cos, sa, sb = (jnp.broadcast_to(rrow[i][:, None, :], (tm // GRID_W, GRID_W, LANES)).reshape(tm, LANES)
                   + rcol_ref[i] for i in range(3))
    ones_bd = ones_ref[...]

    def norm_rope(y, w):
        sq = y * y
        hi = sq.astype(BF16)
        lo = (sq - hi.astype(F32)).astype(BF16)
        ss = (jnp.dot(hi, ones_bd, preferred_element_type=F32)
              + jnp.dot(lo, ones_bd, preferred_element_type=F32))
        yn = y * lax.rsqrt(ss * (1.0 / HEAD_DIM) + EPS) * w
        up = pltpu.roll(yn, LANES - HEAD_DIM // 4, axis=1)
        dn = pltpu.roll(yn, HEAD_DIM // 4, axis=1)
        return yn * cos + up * sa + dn * sb

    qscale = (HEAD_DIM ** -0.5) * math.log2(math.e)
    for c in range(A_WIDTH // LANES):
        q2 = norm_rope(pa[:, c * LANES:(c + 1) * LANES], qw_ref[...]) * qscale
        q2t = q2.T.astype(BF16)
        qt_ref[2 * c, 0] = q2t[0:HEAD_DIM]
        qt_ref[2 * c + 1, 0] = q2t[HEAD_DIM:2 * HEAD_DIM]
    k2 = norm_rope(pa[:, A_WIDTH:A_WIDTH + KV_WIDTH], kw_ref[...]).astype(BF16)
    k_ref[0, 0] = k2[:, 0:HEAD_DIM]
    k_ref[1, 0] = k2[:, HEAD_DIM:2 * HEAD_DIM]
    v2t = pa[:, A_WIDTH + KV_WIDTH:A_WIDTH + 2 * KV_WIDTH].T.astype(BF16)
    vt_ref[0, 0] = v2t[0:HEAD_DIM]
    vt_ref[1, 0] = v2t[HEAD_DIM:2 * HEAD_DIM]

    pm = jnp.dot(h, wm_ref[...], preferred_element_type=F32)

    def heads_t(cols0, ref):
        for c in range(M_WIDTH // LANES):
            t2 = pm[:, cols0 + c * LANES:cols0 + (c + 1) * LANES].T.astype(BF16)
            ref[2 * c, 0] = t2[0:HEAD_DIM]
            ref[2 * c + 1, 0] = t2[HEAD_DIM:2 * HEAD_DIM]

    heads_t(0, mq_ref)
    heads_t(2 * M_WIDTH, mv_ref)
    heads_t(3 * M_WIDTH, mo_ref)
    for hd in range(M_HEADS):
        c0 = M_WIDTH + hd * HEAD_DIM
        mk_ref[hd, 0] = (pm[:, c0:c0 + HEAD_DIM] * (HEAD_DIM ** -0.5)).astype(BF16)


def _inproj(x2, T, n1, wm, wg, wa, bg, qw2, kw2, rrow, rcol, ones_bd, tri, tm):
    N = x2.shape[0]
    tps = T // tm
    row = lambda i: (i, 0)
    out_shape = (
        jax.ShapeDtypeStruct((M_HEADS, N // tm, HEAD_DIM, tm), BF16),
        jax.ShapeDtypeStruct((M_HEADS, N // tm, tm, HEAD_DIM), BF16),
        jax.ShapeDtypeStruct((M_HEADS, N // tm, HEAD_DIM, tm), BF16),
        jax.ShapeDtypeStruct((M_HEADS, N // tm, HEAD_DIM, tm), BF16),
        jax.ShapeDtypeStruct((N, LANES), F32), jax.ShapeDtypeStruct((N, LANES), F32),
        jax.ShapeDtypeStruct((N, LANES), F32),
        jax.ShapeDtypeStruct((A_HEADS, N // tm, HEAD_DIM, tm), BF16),
        jax.ShapeDtypeStruct((A_KV_HEADS, N // tm, tm, HEAD_DIM), BF16),
        jax.ShapeDtypeStruct((A_KV_HEADS, N // tm, HEAD_DIM, tm), BF16),
    )
    return pl.pallas_call(
        _inproj_kernel,
        grid=(N // tm,),
        in_specs=[
            pl.BlockSpec((tm, D_MODEL), row),
            _const_spec(n1.shape), _const_spec(wm.shape), _const_spec(wg.shape), _const_spec(wa.shape),
            _const_spec(bg.shape), _const_spec(qw2.shape), _const_spec(kw2.shape),
            pl.BlockSpec((1,) + rrow.shape[1:], lambda i: (i % tps, 0, 0, 0)), _const_spec(rcol.shape),
            _const_spec(ones_bd.shape), _const_spec(tri.shape),
        ],
        out_specs=(
            pl.BlockSpec((M_HEADS, 1, HEAD_DIM, tm), lambda i: (0, i, 0, 0)),
            pl.BlockSpec((M_HEADS, 1, tm, HEAD_DIM), lambda i: (0, i, 0, 0)),
            pl.BlockSpec((M_HEADS, 1, HEAD_DIM, tm), lambda i: (0, i, 0, 0)),
            pl.BlockSpec((M_HEADS, 1, HEAD_DIM, tm), lambda i: (0, i, 0, 0)),
            pl.BlockSpec((tm, LANES), row), pl.BlockSpec((tm, LANES), row), pl.BlockSpec((tm, LANES), row),
            pl.BlockSpec((A_HEADS, 1, HEAD_DIM, tm), lambda i: (0, i, 0, 0)),
            pl.BlockSpec((A_KV_HEADS, 1, tm, HEAD_DIM), lambda i: (0, i, 0, 0)),
            pl.BlockSpec((A_KV_HEADS, 1, HEAD_DIM, tm), lambda i: (0, i, 0, 0)),
        ),
        out_shape=out_shape,
        compiler_params=_cparams(("parallel",)),
        name="inproj",
    )(x2, n1, wm, wg, wa, bg, qw2, kw2, rrow, rcol, ones_bd, tri)


def _log_sigmoid(x):
    return jnp.minimum(x, 0.0) - jnp.log1p(jnp.exp(-jnp.abs(x)))


def _split3(x):
    a = x.astype(BF16)
    r = x - a.astype(F32)
    b = r.astype(BF16)
    c = (r - b.astype(F32)).astype(BF16)
    return a, b, c


def _gate_sums(g, tri):
    tm = g.shape[0]
    logf = _log_sigmoid(g) * math.log2(math.e)
    a, b, c = _split3(logf)
    pre = (jnp.dot(tri, a, preferred_element_type=F32) + jnp.dot(tri, b, preferred_element_type=F32)
           + jnp.dot(tri, c, preferred_element_type=F32))
    tot = jnp.concatenate([jnp.broadcast_to(pre[r + MLSTM_CHUNK - 1:r + MLSTM_CHUNK], (MLSTM_CHUNK, LANES))
                           for r in range(0, tm, MLSTM_CHUNK)], axis=0)
    fwd = _is_fwd_lane((tm, LANES))
    cum = jnp.where(fwd, pre, tot - pre + logf)
    e = pltpu.roll(g, 2 * M_HEADS, axis=1) * math.log2(math.e) - cum

    row = lax.broadcasted_iota(jnp.int32, (tm, LANES), 0) & (MLSTM_CHUNK - 1)
    up, down = e, e
    shift = 1
    while shift < MLSTM_CHUNK:
        up = jnp.maximum(up, jnp.where(row >= shift, pltpu.roll(up, shift, axis=0), NEG))
        down = jnp.maximum(down, jnp.where(row < MLSTM_CHUNK - shift, pltpu.roll(down, tm - shift, axis=0), NEG))
        shift *= 2
    return cum, e, jnp.where(fwd, up, down)


def _is_fwd_lane(shape):
    return lax.broadcasted_iota(jnp.int32, shape, len(shape) - 1) < 3 * M_HEADS


def _mlstm_state_kernel(kf_ref, vtf_ref, cumf_ref, ef_ref, emaxf_ref, kb_ref, vtb_ref, cumb_ref, eb_ref, emaxb_ref,
                        sf_ref, mf_ref, sb_ref, mb_ref, c_ref, m_ref):
    L = MLSTM_CHUNK
    per = ef_ref.shape[0] // L

    @pl.when(pl.program_id(1) == 0)
    def _():
        c_ref[...] = jnp.zeros_like(c_ref)
        m_ref[...] = jnp.zeros_like(m_ref)

    fwd = _is_fwd_lane((1, LANES))
    ones = jnp.ones((MLSTM_VPAD, L), BF16)
    pairs = [(d, hd) for d in range(2) for hd in range(M_HEADS)]
    k_refs, vt_refs, s_refs = (kf_ref, kb_ref), (vtf_ref, vtb_ref), (sf_ref, sb_ref)
    lane = lambda d, hd: (2 + d) * M_HEADS + hd

    for j in range(per):
        rows = (slice(j * L, (j + 1) * L), slice((per - 1 - j) * L, (per - j) * L))
        end_f, end_b = rows[0].stop - 1, rows[1].start
        tot_f = cumf_ref[end_f:end_f + 1, :]
        tot_b = cumb_ref[end_b:end_b + 1, :]
        wend_f = tot_f + ef_ref[rows[0], :]
        wend_b = tot_b + eb_ref[rows[1], :]
        tot = jnp.where(fwd, tot_f, tot_b)
        m_prev = m_ref[...]
        m_new = jnp.maximum(tot + m_prev,
                            tot + jnp.where(fwd, emaxf_ref[end_f:end_f + 1, :], emaxb_ref[end_b:end_b + 1, :]))
        dec = jnp.exp2(tot + m_prev - m_new)
        m_ref[...] = m_new
        m_rows = jnp.broadcast_to(m_prev, (8, LANES))
        mf_ref[j] = m_rows
        mb_ref[per - 1 - j] = m_rows
        we_t = (jnp.exp2(wend_f - m_new).T, jnp.exp2(wend_b - m_new).T)

        wvs = {}
        for d, hd in pairs:
            vext = jnp.concatenate([vt_refs[d][hd, 0, :, rows[d]], ones], axis=0)
            wvs[d, hd] = (vext.astype(F32) * we_t[d][lane(d, hd):lane(d, hd) + 1, :]).astype(BF16)
        adds = {(d, hd): jnp.dot(wvs[d, hd], k_refs[d][hd, 0, rows[d], :], preferred_element_type=F32)
                for d, hd in pairs}
        for d, hd in pairs:
            cext = c_ref[d * M_HEADS + hd]
            s_refs[d][(j, per - 1 - j)[d], hd] = cext
            c_ref[d * M_HEADS + hd] = dec[:, lane(d, hd):lane(d, hd) + 1] * cext + adds[d, hd]


def _mlstm_state(mk, mvt, cum, e, emax, T):
    _, nblk, blk, _ = mk.shape
    N = nblk * blk
    L = MLSTM_CHUNK
    per = blk // L
    nt = T // blk
    B = N // T
    tf = lambda b, t: b * nt + t
    tb = lambda b, t: b * nt + (nt - 1 - t)

    def specs(ti):
        return [pl.BlockSpec((M_HEADS, 1, blk, HEAD_DIM), lambda b, t: (0, ti(b, t), 0, 0)),
                pl.BlockSpec((M_HEADS, 1, HEAD_DIM, blk), lambda b, t: (0, ti(b, t), 0, 0)),
                pl.BlockSpec((blk, LANES), lambda b, t: (ti(b, t), 0)),
                pl.BlockSpec((blk, LANES), lambda b, t: (ti(b, t), 0)),
                pl.BlockSpec((blk, LANES), lambda b, t: (ti(b, t), 0))]

    def outs(ti):
        return [pl.BlockSpec((per, M_HEADS, HEAD_DIM + MLSTM_VPAD, HEAD_DIM), lambda b, t: (ti(b, t), 0, 0, 0)),
                pl.BlockSpec((per, 8, LANES), lambda b, t: (ti(b, t), 0, 0))]

    s_shape = jax.ShapeDtypeStruct((N // L, M_HEADS, HEAD_DIM + MLSTM_VPAD, HEAD_DIM), F32)
    m_shape = jax.ShapeDtypeStruct((N // L, 8, LANES), F32)
    return pl.pallas_call(
        _mlstm_state_kernel,
        grid=(B, nt),
        in_specs=specs(tf) + specs(tb),
        out_specs=tuple(outs(tf) + outs(tb)),
        out_shape=(s_shape, m_shape, s_shape, m_shape),
        scratch_shapes=[pltpu.VMEM((2 * M_HEADS, HEAD_DIM + MLSTM_VPAD, HEAD_DIM), F32), pltpu.VMEM((1, LANES), F32)],
        compiler_params=_cparams(("parallel", "arbitrary")),
        name="mlstm_state",
    )(mk, mvt, cum, e, emax, mk, mvt, cum, e, emax)


def _mlstm_out_kernel(qt_ref, k_ref, vt_ref, mot_ref, cum_ref, e_ref, emax_ref, sf_ref, mf_ref, sb_ref, mb_ref,
                      nw_ref, out_ref):
    L = e_ref.shape[0]
    e = e_ref[...]
    m_in = jnp.where(_is_fwd_lane((1, LANES)), mf_ref[0, 0:1, :], mb_ref[0, 0:1, :])
    cum_t = cum_ref[...].T
    mx_t = jnp.maximum(emax_ref[...], m_in).T

    si = lax.broadcasted_iota(jnp.int32, (L, L), 0)
    li = lax.broadcasted_iota(jnp.int32, (L, L), 1)
    keep = (si <= li, si >= li)
    ones = jnp.ones((MLSTM_VPAD, L), BF16)

    pairs = [(hd, d) for hd in range(M_HEADS) for d in range(2)]
    s_refs = (sf_ref, sb_ref)
    lane = lambda hd, d: (2 + d) * M_HEADS + hd
    qts = [qt_ref[hd, 0] for hd in range(M_HEADS)]
    raws = [jnp.dot(k_ref[hd, 0], qts[hd], preferred_element_type=F32) for hd in range(M_HEADS)]
    carried = {(hd, d): jnp.dot(s_refs[d][0, hd].astype(BF16), qts[hd], preferred_element_type=F32)
               for hd, d in pairs}
    mxs = {(hd, d): mx_t[lane(hd, d):lane(hd, d) + 1, :] for hd, d in pairs}
    ws = {(hd, d): jnp.where(keep[d], raws[hd] * jnp.exp2(e[:, lane(hd, d):lane(hd, d) + 1] - mxs[hd, d]),
                             0.0).astype(BF16) for hd, d in pairs}
    vexts = [jnp.concatenate([vt_ref[hd, 0], ones], axis=0) for hd in range(M_HEADS)]
    nds = {(hd, d): (jnp.exp2(m_in[:, lane(hd, d):lane(hd, d) + 1] - mxs[hd, d]) * carried[hd, d]
                     + jnp.dot(vexts[hd], ws[hd, d], preferred_element_type=F32)) for hd, d in pairs}
    hhs = {(hd, d): nds[hd, d][0:HEAD_DIM] / jnp.maximum(
        jnp.abs(nds[hd, d][HEAD_DIM:HEAD_DIM + 1]),
        jnp.exp2(-(cum_t[lane(hd, d):lane(hd, d) + 1, :] + mxs[hd, d]))) for hd, d in pairs}
    outs = []
    for hd in range(M_HEADS):
        hs = hhs[hd, 0] + hhs[hd, 1]
        hn = hs * lax.rsqrt(jnp.mean(hs * hs, axis=0, keepdims=True) + EPS) * nw_ref[hd]
        mo = mot_ref[hd, 0].astype(F32)
        outs.append(hn / (1.0 + jnp.exp(-mo)))
    for jj in range(M_HEADS // 2):
        o2 = jnp.concatenate([outs[2 * jj], outs[2 * jj + 1]], axis=0)
        out_ref[:, jj * LANES:(jj + 1) * LANES] = o2.T.astype(out_ref.dtype)


def _mlstm_out(mqt, mk, mvt, mot, cum, e, emax, sf, mf, sb, mb, nwb):
    _, nblk, _, blk = mqt.shape
    N = nblk * blk
    L = MLSTM_CHUNK
    per = blk // L
    tspec = pl.BlockSpec((M_HEADS, 1, HEAD_DIM, L), lambda c: (0, c // per, 0, c % per))
    kspec = pl.BlockSpec((M_HEADS, 1, L, HEAD_DIM), lambda c: (0, c // per, c % per, 0))
    sspec = pl.BlockSpec((1, M_HEADS, HEAD_DIM + MLSTM_VPAD, HEAD_DIM), lambda c: (c, 0, 0, 0))
    mspec = pl.BlockSpec((1, 8, LANES), lambda c: (c, 0, 0))
    gspec = pl.BlockSpec((L, LANES), lambda c: (c, 0))
    return pl.pallas_call(
        _mlstm_out_kernel,
        grid=(N // L,),
        in_specs=[tspec, kspec, tspec, tspec, gspec, gspec, gspec, sspec, mspec, sspec, mspec,
                  _const_spec(nwb.shape)],
        out_specs=pl.BlockSpec((L, M_WIDTH), lambda c: (c, 0)),
        out_shape=jax.ShapeDtypeStruct((N, M_WIDTH), BF16),
        compiler_params=_cparams(("parallel",)),
        name="mlstm_out",
    )(mqt, mk, mvt, mot, cum, e, emax, sf, mf, sb, mb, nwb)


def _attn_kernel(qt_ref, k_ref, vt_ref, o_ref, m_ref, acc_ref, s0_ref, s1_ref, x0_ref, x1_ref):
    nkb = k_ref.shape[1]
    tk = k_ref.shape[2]
    tq = qt_ref.shape[3]
    units = [(j, c) for j in range(A_GROUP) for c in range(0, tq, ATTN_QW)]
    ones = jnp.ones((ATTN_VPAD, tk), BF16)

    m_ref[...] = jnp.full_like(m_ref, -jnp.inf)
    acc_ref[...] = jnp.zeros_like(acc_ref)

    def stage(cur, nxt):
        if nxt is not None:
            k = k_ref[0, nxt[0]]
        if cur is not None:
            vt = jnp.concatenate([vt_ref[0, cur[0]], ones], axis=0)
        for u, (j, c) in enumerate(units):
            cols = slice(c, c + ATTN_QW)
            if nxt is not None:
                st = jnp.dot(k, qt_ref[j, 0, :, cols], preferred_element_type=F32)
                nxt[1][u] = st
                nxt[2][u] = jnp.max(st, axis=0, keepdims=True)
            if cur is not None:
                st = cur[1][u]
                m_old = m_ref[j, :, cols]
                m_new = jnp.maximum(m_old, cur[2][u])
                alpha = jnp.exp2(m_old - m_new)
                p = jnp.exp2(st - m_new)
                acc_ref[j, :, cols] = (alpha * acc_ref[j, :, cols]
                                       + jnp.dot(vt, p.astype(BF16), preferred_element_type=F32))
                m_ref[j, :, cols] = m_new

    stage(None, (0, s0_ref, x0_ref))

    def body(i, carry):
        kb = 2 * i
        stage((kb, s0_ref, x0_ref), (kb + 1, s1_ref, x1_ref))
        stage((kb + 1, s1_ref, x1_ref), (kb + 2, s0_ref, x0_ref))
        return carry

    lax.fori_loop(0, nkb // 2 - 1, body, 0)
    stage((nkb - 2, s0_ref, x0_ref), (nkb - 1, s1_ref, x1_ref))
    stage((nkb - 1, s1_ref, x1_ref), None)

    def head_out(j):
        return acc_ref[j, 0:HEAD_DIM, :] / acc_ref[j, HEAD_DIM:HEAD_DIM + 1, :]

    for jj in range(A_GROUP // 2):
        o2 = jnp.concatenate([head_out(2 * jj), head_out(2 * jj + 1)], axis=0)
        o_ref[:, jj * LANES:(jj + 1) * LANES] = o2.T.astype(o_ref.dtype)


def _attn(qt, k, vt, T):
    _, nblk, _, blk = qt.shape
    N = nblk * blk
    B = N // T
    nq = T // blk
    assert nq % 2 == 0, "attention pipelines key blocks in pairs"
    nu = A_GROUP * (blk // ATTN_QW)
    return pl.pallas_call(
        _attn_kernel,
        grid=(B, A_KV_HEADS, nq),
        in_specs=[
            pl.BlockSpec((A_GROUP, 1, HEAD_DIM, blk), lambda b, g, qi: (g, b * nq + qi, 0, 0)),
            pl.BlockSpec((1, nq, blk, HEAD_DIM), lambda b, g, qi: (g, b, 0, 0)),
            pl.BlockSpec((1, nq, HEAD_DIM, blk), lambda b, g, qi: (g, b, 0, 0)),
        ],
        out_specs=pl.BlockSpec((blk, A_GROUP * HEAD_DIM), lambda b, g, qi: (b * nq + qi, g)),
        out_shape=jax.ShapeDtypeStruct((N, A_WIDTH), BF16),
        scratch_shapes=[pltpu.VMEM((A_GROUP, 1, blk), F32),
                        pltpu.VMEM((A_GROUP, HEAD_DIM + ATTN_VPAD, blk), F32),
                        pltpu.VMEM((nu, blk, ATTN_QW), F32), pltpu.VMEM((nu, blk, ATTN_QW), F32),
                        pltpu.VMEM((nu, 1, ATTN_QW), F32), pltpu.VMEM((nu, 1, ATTN_QW), F32)],
        compiler_params=_cparams(("parallel", "parallel", "arbitrary")),
        name="attn",
    )(qt, k, vt)


def _outproj_kernel(x_ref, m_ref, a_ref, wom_ref, woa_ref, n2_ref, x1_ref, h2_ref):
    x1 = (x_ref[...] + jnp.dot(m_ref[...], wom_ref[...], preferred_element_type=F32)
          + jnp.dot(a_ref[...], woa_ref[...], preferred_element_type=F32))
    x1_ref[...] = x1
    h2 = x1 * lax.rsqrt(jnp.mean(x1 * x1, axis=-1, keepdims=True) + EPS) * n2_ref[...]
    h2_ref[...] = h2.astype(BF16)


def _outproj(x2, m_out, a_out, wom, woa, n2, tm):
    N = x2.shape[0]
    row = lambda i: (i, 0)
    return pl.pallas_call(
        _outproj_kernel,
        grid=(N // tm,),
        in_specs=[pl.BlockSpec((tm, D_MODEL), row), pl.BlockSpec((tm, M_WIDTH), row),
                  pl.BlockSpec((tm, A_WIDTH), row), _const_spec(wom.shape), _const_spec(woa.shape),
                  _const_spec(n2.shape)],
        out_specs=(pl.BlockSpec((tm, D_MODEL), row), pl.BlockSpec((tm, D_MODEL), row)),
        out_shape=(jax.ShapeDtypeStruct((N, D_MODEL), F32), jax.ShapeDtypeStruct((N, D_MODEL), BF16)),
        compiler_params=_cparams(("parallel",)),
        name="outproj",
    )(x2, m_out, a_out, wom, woa, n2)


def _ffn_kernel(hp_ref, hm_ref, hn_ref, x1_ref, wup_ref, cw_ref, cb_ref, wdn_ref, fw_ref, y_ref,
                lhs_ref, u0_ref, u1_ref, acc_ref, *, tiles_per_seq):
    i = pl.program_id(0)
    tm = hm_ref.shape[0]
    H = CONV_HALO
    t = i % tiles_per_seq
    lhs_ref[0:H] = jnp.where(t == 0, jnp.zeros_like(hp_ref), hp_ref[...])
    lhs_ref[H:H + tm] = hm_ref[...]
    lhs_ref[H + tm:H + tm + H] = jnp.where(t == tiles_per_seq - 1, jnp.zeros_like(hn_ref), hn_ref[...])
    acc_ref[...] = jnp.zeros_like(acc_ref)
    nchunk = D_FF // FF_CHUNK

    def up(c, u_ref):
        u_ref[...] = jnp.dot(lhs_ref[...], wup_ref[c], preferred_element_type=F32)

    def gate_down(c, u_ref):
        w = cw_ref[c]
        conv = (u_ref[pl.ds(H - 1, tm), :] * w[0:1] + u_ref[pl.ds(H, tm), :] * w[1:2]
                + u_ref[pl.ds(H + 1, tm), :] * w[2:3] + cb_ref[c])
        a = conv[:, 0:FF_CHUNK]
        gt = conv[:, FF_CHUNK:2 * FF_CHUNK]
        act = (gt / (1.0 + jnp.exp(-gt))) * a
        acc_ref[...] += jnp.dot(act.astype(BF16), wdn_ref[c], preferred_element_type=F32)

    assert nchunk % 2 == 1
    up(0, u0_ref)

    def body(i, carry):
        c = 2 * i
        up(c + 1, u1_ref)
        gate_down(c, u0_ref)
        up(c + 2, u0_ref)
        gate_down(c + 1, u1_ref)
        return carry

    lax.fori_loop(0, nchunk // 2, body, 0)
    gate_down(nchunk - 1, u0_ref)
    x2 = x1_ref[...] + acc_ref[...]
    y_ref[...] = x2 * lax.rsqrt(jnp.mean(x2 * x2, axis=-1, keepdims=True) + EPS) * fw_ref[...]


def _ffn(h2, x1, T, wup, cw, cb, wdn, fw, tm):
    N = h2.shape[0]
    H = CONV_HALO
    tps = T // tm
    r = tm // H
    nblk = N // H
    row = lambda i: (i, 0)
    return pl.pallas_call(
        functools.partial(_ffn_kernel, tiles_per_seq=tps),
        grid=(N // tm,),
        in_specs=[
            pl.BlockSpec((H, D_MODEL), lambda i: (jnp.maximum(i * r - 1, 0), 0)),
            pl.BlockSpec((tm, D_MODEL), row),
            pl.BlockSpec((H, D_MODEL), lambda i: (jnp.minimum((i + 1) * r, nblk - 1), 0)),
            pl.BlockSpec((tm, D_MODEL), row),
            _const_spec(wup.shape), _const_spec(cw.shape), _const_spec(cb.shape), _const_spec(wdn.shape),
            _const_spec(fw.shape),
        ],
        out_specs=pl.BlockSpec((tm, D_MODEL), row),
        out_shape=jax.ShapeDtypeStruct((N, D_MODEL), F32),
        scratch_shapes=[pltpu.VMEM((tm + 2 * H, D_MODEL), BF16), pltpu.VMEM((tm + 2 * H, 2 * FF_CHUNK), F32),
                        pltpu.VMEM((tm + 2 * H, 2 * FF_CHUNK), F32),
                        pltpu.VMEM((tm, D_MODEL), F32)],
        compiler_params=_cparams(("parallel",)),
        name="ffn",
    )(h2, h2, h2, x1, wup, cw, cb, wdn, fw)


def _rope_tables(T, tm):
    nf = HEAD_DIM // 4
    inv = ROPE_THETA ** (-jnp.arange(nf, dtype=F32) / nf)
    zero = jnp.zeros((1, 2 * nf), F32)

    def parts(pos, row_part):
        ang = pos[:, None] * inv
        cos, sin = jnp.cos(ang), jnp.sin(ang)
        z = jnp.broadcast_to(zero, (pos.shape[0], 2 * nf))
        halves = lambda a, b: jnp.concatenate([a, b, z] if row_part else [z, a, b], axis=-1)
        tabs = (halves(cos, cos), halves(-sin, 0.0 * sin), halves(0.0 * sin, sin))
        return jnp.stack([jnp.concatenate([t, t], axis=-1) for t in tabs])

    rrow = parts(jnp.arange(T // GRID_W, dtype=F32), True)
    rrow = rrow.reshape(3, T // tm, tm // GRID_W, LANES).transpose(1, 0, 2, 3)
    rcol = parts(jnp.arange(GRID_W, dtype=F32), False)
    return rrow, jnp.tile(rcol, (1, tm // GRID_W, 1))


def _prep_weights(w_in, b_gates, mh_norm_w, q_norm_w, k_norm_w, w_out, norm1_w, norm2_w, w_up, conv_w, conv_b,
                  w_down, final_norm_w):
    gate0 = 4 * M_WIDTH
    wm = w_in[:, :gate0].astype(BF16)
    wg = jnp.pad(w_in[:, gate0:gate0 + N_GATES], ((0, 0), (0, LANES - N_GATES))).astype(BF16)
    wa = w_in[:, gate0 + N_GATES:].astype(BF16)
    bg = jnp.pad(b_gates, (0, LANES - N_GATES)).reshape(1, LANES)
    nchunk = D_FF // FF_CHUNK
    wua = w_up[:, :D_FF].reshape(D_MODEL, nchunk, FF_CHUNK)
    wug = w_up[:, D_FF:].reshape(D_MODEL, nchunk, FF_CHUNK)
    wup = jnp.concatenate([wua, wug], axis=-1).transpose(1, 0, 2).astype(BF16)
    cwa = conv_w[:, :D_FF].reshape(3, nchunk, FF_CHUNK)
    cwg = conv_w[:, D_FF:].reshape(3, nchunk, FF_CHUNK)
    cw = jnp.concatenate([cwa, cwg], axis=-1).transpose(1, 0, 2)
    cb = jnp.concatenate([conv_b[:D_FF].reshape(nchunk, 1, FF_CHUNK),
                          conv_b[D_FF:].reshape(nchunk, 1, FF_CHUNK)], axis=-1)
    wdn = w_down.reshape(nchunk, FF_CHUNK, D_MODEL).astype(BF16)
    return dict(
        wm=wm, wg=wg, wa=wa, bg=bg,
        n1=norm1_w.reshape(1, D_MODEL), n2=norm2_w.reshape(1, D_MODEL), fw=final_norm_w.reshape(1, D_MODEL),
        qw2=jnp.tile(q_norm_w, 2).reshape(1, LANES), kw2=jnp.tile(k_norm_w, 2).reshape(1, LANES),
        nw=mh_norm_w.reshape(1, M_WIDTH),
        wom=w_out[:M_WIDTH].astype(BF16), woa=w_out[M_WIDTH:].astype(BF16),
        wup=wup, cw=cw, cb=cb, wdn=wdn,
    )


def _tiles(T):
    return min(512, T // 2)


def _trunk(x, p):
    B, T, _ = x.shape
    N = B * T
    tm = _tiles(T)
    x2 = x.reshape(N, D_MODEL)
    rrow, rcol = _rope_tables(T, tm)
    blk = np.arange(LANES) // HEAD_DIM
    ones_bd = jnp.asarray(blk[:, None] == blk[None, :], BF16)
    idx = np.arange(tm)
    tri = jnp.asarray((idx[None, :] <= idx[:, None])
                      & (idx[None, :] // MLSTM_CHUNK == idx[:, None] // MLSTM_CHUNK), BF16)

    mq, mk, mv, mo, cum, e, emax, qt, k, vt = _inproj(x2, T, p["n1"], p["wm"], p["wg"], p["wa"], p["bg"], p["qw2"],
                                                      p["kw2"], rrow, rcol, ones_bd, tri, tm)
    nwb = jnp.broadcast_to(p["nw"].reshape(M_HEADS, HEAD_DIM, 1), (M_HEADS, HEAD_DIM, MLSTM_CHUNK))
    s_f, m_f, s_b, m_b = _mlstm_state(mk, mv, cum, e, emax, T)
    m_out = _mlstm_out(mq, mk, mv, mo, cum, e, emax, s_f, m_f, s_b, m_b, nwb)
    a_out = _attn(qt, k, vt, T)
    x1, h2 = _outproj(x2, m_out, a_out, p["wom"], p["woa"], p["n2"], tm)
    y = _ffn(h2, x1, T, p["wup"], p["cw"], p["cb"], p["wdn"], p["fw"], tm)
    return y.reshape(B, T, D_MODEL)


def kernel(x_prompt, x_sample, w_in, b_gates, mh_norm_w, q_norm_w, k_norm_w, w_out, norm1_w, norm2_w, w_up,
           conv_w, conv_b, w_down, final_norm_w):
    assert w_in.shape[0] == 1, "single-layer trunk"
    p = _prep_weights(w_in[0], b_gates[0], mh_norm_w[0], q_norm_w[0], k_norm_w[0], w_out[0], norm1_w[0],
                      norm2_w[0], w_up[0], conv_w[0], conv_b[0], w_down[0], final_norm_w)
    return (_trunk(x_prompt, p), _trunk(x_sample, p))
```

```python
import functools
import math

import jax
import jax.numpy as jnp
import numpy as np
from jax import lax
from jax.experimental import pallas as pl
from jax.experimental.pallas import tpu as pltpu

F32 = jnp.float32
BF16 = jnp.bfloat16

D_MODEL = 1024
HEAD_DIM = 64
M_HEADS = 8
M_WIDTH = M_HEADS * HEAD_DIM
A_HEADS = 8
A_KV_HEADS = 2
A_GROUP = A_HEADS // A_KV_HEADS
A_WIDTH = A_HEADS * HEAD_DIM
KV_WIDTH = A_KV_HEADS * HEAD_DIM
N_GATES = 4 * M_HEADS
D_FF = 2816
GRID_W = 64
ROPE_THETA = 10000.0
EPS = 1e-6

LANES = 128
VMEM_LIMIT = 56 * 1024 * 1024

MLSTM_CHUNK = 256
MLSTM_VPAD = 16
FF_CHUNK = 256
CONV_HALO = 16
NEG = -1e30
ATTN_QW = 256
ATTN_VPAD = 16


def _cparams(sem):
    return pltpu.CompilerParams(dimension_semantics=sem, vmem_limit_bytes=VMEM_LIMIT)


def _const_spec(shape):
    nd = len(shape)
    return pl.BlockSpec(shape, lambda *_: (0,) * nd, pipeline_mode=pl.Buffered(1))


def _inproj_kernel(x_ref, n1_ref, wm_ref, wg_ref, wa_ref, bg_ref, qw_ref, kw_ref, rrow_ref, rcol_ref,
                   ones_ref, tri_ref, mq_ref, mk_ref, mv_ref, mo_ref, cum_ref, e_ref, emax_ref, ends_ref,
                   qt_ref, k_ref, vt_ref):
    x = x_ref[...]
    h = x * lax.rsqrt(jnp.mean(x * x, axis=-1, keepdims=True) + EPS) * n1_ref[...]
    h = h.astype(BF16)

    g = jnp.dot(h, wg_ref[...], preferred_element_type=F32) + bg_ref[...]
    cum_ref[...], e_ref[...], emax_ref[...], ends_ref[...] = _gate_sums(g, tri_ref[...])

    pa = jnp.dot(h, wa_ref[...], preferred_element_type=F32)
    tm = x.shape[0]
    rrow = rrow_ref[0]
    cos, sa, sb = (jnp.broadcast_to(rrow[i][:, None, :], (tm // GRID_W, GRID_W, LANES)).reshape(tm, LANES)
                   + rcol_ref[i] for i in range(3))
    ones_bd = ones_ref[...]

    def norm_rope(y, w):
        sq = y * y
        hi = sq.astype(BF16)
        lo = (sq - hi.astype(F32)).astype(BF16)
        ss = (jnp.dot(hi, ones_bd, preferred_element_type=F32)
              + jnp.dot(lo, ones_bd, preferred_element_type=F32))
        yn = y * lax.rsqrt(ss * (1.0 / HEAD_DIM) + EPS) * w
        up = pltpu.roll(yn, LANES - HEAD_DIM // 4, axis=1)
        dn = pltpu.roll(yn, HEAD_DIM // 4, axis=1)
        return yn * cos + up * sa + dn * sb

    qscale = (HEAD_DIM ** -0.5) * math.log2(math.e)
    for c in range(A_WIDTH // LANES):
        q2 = norm_rope(pa[:, c * LANES:(c + 1) * LANES], qw_ref[...]) * qscale
        q2t = q2.T.astype(BF16)
        qt_ref[2 * c, 0] = q2t[0:HEAD_DIM]
        qt_ref[2 * c + 1, 0] = q2t[HEAD_DIM:2 * HEAD_DIM]
    k2 = norm_rope(pa[:, A_WIDTH:A_WIDTH + KV_WIDTH], kw_ref[...]).astype(BF16)
    k_ref[0, 0] = k2[:, 0:HEAD_DIM]
    k_ref[1, 0] = k2[:, HEAD_DIM:2 * HEAD_DIM]
    v2t = pa[:, A_WIDTH + KV_WIDTH:A_WIDTH + 2 * KV_WIDTH].T.astype(BF16)
    vt_ref[0, 0] = v2t[0:HEAD_DIM]
    vt_ref[1, 0] = v2t[HEAD_DIM:2 * HEAD_DIM]

    pm = jnp.dot(h, wm_ref[...], preferred_element_type=F32)

    def heads_t(cols0, ref):
        for c in range(M_WIDTH // LANES):
            t2 = pm[:, cols0 + c * LANES:cols0 + (c + 1) * LANES].T.astype(BF16)
            ref[2 * c, 0] = t2[0:HEAD_DIM]
            ref[2 * c + 1, 0] = t2[HEAD_DIM:2 * HEAD_DIM]

    heads_t(0, mq_ref)
    heads_t(2 * M_WIDTH, mv_ref)
    heads_t(3 * M_WIDTH, mo_ref)
    for hd in range(M_HEADS):
        c0 = M_WIDTH + hd * HEAD_DIM
        mk_ref[hd, 0] = (pm[:, c0:c0 + HEAD_DIM] * (HEAD_DIM ** -0.5)).astype(BF16)


def _inproj(x2, T, n1, wm, wg, wa, bg, qw2, kw2, rrow, rcol, ones_bd, tri, tm):
    N = x2.shape[0]
    tps = T // tm
    row = lambda i: (i, 0)
    out_shape = (
        jax.ShapeDtypeStruct((M_HEADS, N // tm, HEAD_DIM, tm), BF16),
        jax.ShapeDtypeStruct((M_HEADS, N // tm, tm, HEAD_DIM), BF16),
        jax.ShapeDtypeStruct((M_HEADS, N // tm, HEAD_DIM, tm), BF16),
        jax.ShapeDtypeStruct((M_HEADS, N // tm, HEAD_DIM, tm), BF16),
        jax.ShapeDtypeStruct((N, LANES), F32), jax.ShapeDtypeStruct((N, LANES), F32),
        jax.ShapeDtypeStruct((N, LANES), F32), jax.ShapeDtypeStruct((N // MLSTM_CHUNK, 8, LANES), F32),
        jax.ShapeDtypeStruct((A_HEADS, N // tm, HEAD_DIM, tm), BF16),
        jax.ShapeDtypeStruct((A_KV_HEADS, N // tm, tm, HEAD_DIM), BF16),
        jax.ShapeDtypeStruct((A_KV_HEADS, N // tm, HEAD_DIM, tm), BF16),
    )
    return pl.pallas_call(
        _inproj_kernel,
        grid=(N // tm,),
        in_specs=[
            pl.BlockSpec((tm, D_MODEL), row),
            _const_spec(n1.shape), _const_spec(wm.shape), _const_spec(wg.shape), _const_spec(wa.shape),
            _const_spec(bg.shape), _const_spec(qw2.shape), _const_spec(kw2.shape),
            pl.BlockSpec((1,) + rrow.shape[1:], lambda i: (i % tps, 0, 0, 0)), _const_spec(rcol.shape),
            _const_spec(ones_bd.shape), _const_spec(tri.shape),
        ],
        out_specs=(
            pl.BlockSpec((M_HEADS, 1, HEAD_DIM, tm), lambda i: (0, i, 0, 0)),
            pl.BlockSpec((M_HEADS, 1, tm, HEAD_DIM), lambda i: (0, i, 0, 0)),
            pl.BlockSpec((M_HEADS, 1, HEAD_DIM, tm), lambda i: (0, i, 0, 0)),
            pl.BlockSpec((M_HEADS, 1, HEAD_DIM, tm), lambda i: (0, i, 0, 0)),
            pl.BlockSpec((tm, LANES), row), pl.BlockSpec((tm, LANES), row), pl.BlockSpec((tm, LANES), row),
            pl.BlockSpec((tm // MLSTM_CHUNK, 8, LANES), lambda i: (i, 0, 0)),
            pl.BlockSpec((A_HEADS, 1, HEAD_DIM, tm), lambda i: (0, i, 0, 0)),
            pl.BlockSpec((A_KV_HEADS, 1, tm, HEAD_DIM), lambda i: (0, i, 0, 0)),
            pl.BlockSpec((A_KV_HEADS, 1, HEAD_DIM, tm), lambda i: (0, i, 0, 0)),
        ),
        out_shape=out_shape,
        compiler_params=_cparams(("parallel",)),
        name="inproj",
    )(x2, n1, wm, wg, wa, bg, qw2, kw2, rrow, rcol, ones_bd, tri)


def _log_sigmoid(x):
    return jnp.minimum(x, 0.0) - jnp.log1p(jnp.exp(-jnp.abs(x)))


def _split3(x):
    a = x.astype(BF16)
    r = x - a.astype(F32)
    b = r.astype(BF16)
    c = (r - b.astype(F32)).astype(BF16)
    return a, b, c


def _gate_sums(g, tri):
    tm = g.shape[0]
    logf = _log_sigmoid(g) * math.log2(math.e)
    a, b, c = _split3(logf)
    pre = (jnp.dot(tri, a, preferred_element_type=F32) + jnp.dot(tri, b, preferred_element_type=F32)
           + jnp.dot(tri, c, preferred_element_type=F32))
    tot = jnp.concatenate([jnp.broadcast_to(pre[r + MLSTM_CHUNK - 1:r + MLSTM_CHUNK], (MLSTM_CHUNK, LANES))
                           for r in range(0, tm, MLSTM_CHUNK)], axis=0)
    fwd = _is_fwd_lane((tm, LANES))
    cum = jnp.where(fwd, pre, tot - pre + logf)
    e = pltpu.roll(g, 2 * M_HEADS, axis=1) * math.log2(math.e) - cum

    row = lax.broadcasted_iota(jnp.int32, (tm, LANES), 0) & (MLSTM_CHUNK - 1)
    up, down = e, e
    shift = 1
    while shift < MLSTM_CHUNK:
        up = jnp.maximum(up, jnp.where(row >= shift, pltpu.roll(up, shift, axis=0), NEG))
        down = jnp.maximum(down, jnp.where(row < MLSTM_CHUNK - shift, pltpu.roll(down, tm - shift, axis=0), NEG))
        shift *= 2
    ends = [jnp.concatenate([pre[r + MLSTM_CHUNK - 1:r + MLSTM_CHUNK],
                             jnp.where(fwd[0:1], up[r + MLSTM_CHUNK - 1:r + MLSTM_CHUNK], down[r:r + 1]),
                             jnp.zeros((6, LANES), F32)], axis=0) for r in range(0, tm, MLSTM_CHUNK)]
    return cum, e, jnp.where(fwd, up, down), jnp.stack(ends)


def _is_fwd_lane(shape):
    return lax.broadcasted_iota(jnp.int32, shape, len(shape) - 1) < 3 * M_HEADS


def _mlstm_state_kernel(kf_ref, vtf_ref, ef_ref, endf_ref, kb_ref, vtb_ref, eb_ref, endb_ref,
                        sf_ref, mf_ref, sb_ref, mb_ref, c_ref, m_ref):
    L = MLSTM_CHUNK
    per = ef_ref.shape[0] // L

    @pl.when(pl.program_id(1) == 0)
    def _():
        c_ref[...] = jnp.zeros_like(c_ref)
        m_ref[...] = jnp.zeros_like(m_ref)

    fwd = _is_fwd_lane((1, LANES))
    ones = jnp.ones((MLSTM_VPAD, L), BF16)
    pairs = [(d, hd) for d in range(2) for hd in range(M_HEADS)]
    k_refs, vt_refs, s_refs = (kf_ref, kb_ref), (vtf_ref, vtb_ref), (sf_ref, sb_ref)
    lane = lambda d, hd: (2 + d) * M_HEADS + hd

    for j in range(per):
        jb = per - 1 - j
        rows = (slice(j * L, (j + 1) * L), slice(jb * L, (jb + 1) * L))
        tot_f = endf_ref[j, 0:1, :]
        tot_b = endb_ref[jb, 0:1, :]
        wend_f = tot_f + ef_ref[rows[0], :]
        wend_b = tot_b + eb_ref[rows[1], :]
        tot = jnp.where(fwd, tot_f, tot_b)
        m_prev = m_ref[...]
        m_new = jnp.maximum(tot + m_prev, tot + jnp.where(fwd, endf_ref[j, 1:2, :], endb_ref[jb, 1:2, :]))
        dec = jnp.exp2(tot + m_prev - m_new)
        m_ref[...] = m_new
        m_rows = jnp.broadcast_to(m_prev, (8, LANES))
        mf_ref[j] = m_rows
        mb_ref[jb] = m_rows
        we_t = (jnp.exp2(wend_f - m_new).T, jnp.exp2(wend_b - m_new).T)

        wvs = {}
        for d, hd in pairs:
            vext = jnp.concatenate([vt_refs[d][hd, 0, :, rows[d]], ones], axis=0)
            wvs[d, hd] = (vext.astype(F32) * we_t[d][lane(d, hd):lane(d, hd) + 1, :]).astype(BF16)
        adds = {(d, hd): jnp.dot(wvs[d, hd], k_refs[d][hd, 0, rows[d], :], preferred_element_type=F32)
                for d, hd in pairs}
        for d, hd in pairs:
            cext = c_ref[d * M_HEADS + hd]
            s_refs[d][(j, jb)[d], hd] = cext
            c_ref[d * M_HEADS + hd] = dec[:, lane(d, hd):lane(d, hd) + 1] * cext + adds[d, hd]


def _mlstm_state(mk, mvt, e, ends, T):
    _, nblk, blk, _ = mk.shape
    N = nblk * blk
    L = MLSTM_CHUNK
    per = blk // L
    nt = T // blk
    B = N // T
    tf = lambda b, t: b * nt + t
    tb = lambda b, t: b * nt + (nt - 1 - t)

    def specs(ti):
        return [pl.BlockSpec((M_HEADS, 1, blk, HEAD_DIM), lambda b, t: (0, ti(b, t), 0, 0)),
                pl.BlockSpec((M_HEADS, 1, HEAD_DIM, blk), lambda b, t: (0, ti(b, t), 0, 0)),
                pl.BlockSpec((blk, LANES), lambda b, t: (ti(b, t), 0)),
                pl.BlockSpec((per, 8, LANES), lambda b, t: (ti(b, t), 0, 0))]

    def outs(ti):
        return [pl.BlockSpec((per, M_HEADS, HEAD_DIM + MLSTM_VPAD, HEAD_DIM), lambda b, t: (ti(b, t), 0, 0, 0)),
                pl.BlockSpec((per, 8, LANES), lambda b, t: (ti(b, t), 0, 0))]

    s_shape = jax.ShapeDtypeStruct((N // L, M_HEADS, HEAD_DIM + MLSTM_VPAD, HEAD_DIM), F32)
    m_shape = jax.ShapeDtypeStruct((N // L, 8, LANES), F32)
    return pl.pallas_call(
        _mlstm_state_kernel,
        grid=(B, nt),
        in_specs=specs(tf) + specs(tb),
        out_specs=tuple(outs(tf) + outs(tb)),
        out_shape=(s_shape, m_shape, s_shape, m_shape),
        scratch_shapes=[pltpu.VMEM((2 * M_HEADS, HEAD_DIM + MLSTM_VPAD, HEAD_DIM), F32), pltpu.VMEM((1, LANES), F32)],
        compiler_params=_cparams(("parallel", "arbitrary")),
        name="mlstm_state",
    )(mk, mvt, e, ends, mk, mvt, e, ends)


def _mlstm_out_kernel(qt_ref, k_ref, vt_ref, mot_ref, cum_ref, e_ref, emax_ref, sf_ref, mf_ref, sb_ref, mb_ref,
                      nw_ref, out_ref):
    L = e_ref.shape[0]
    e = e_ref[...]
    m_in = jnp.where(_is_fwd_lane((1, LANES)), mf_ref[0, 0:1, :], mb_ref[0, 0:1, :])
    cum_t = cum_ref[...].T
    mx_t = jnp.maximum(emax_ref[...], m_in).T

    si = lax.broadcasted_iota(jnp.int32, (L, L), 0)
    li = lax.broadcasted_iota(jnp.int32, (L, L), 1)
    keep = (si <= li, si >= li)
    ones = jnp.ones((MLSTM_VPAD, L), BF16)

    pairs = [(hd, d) for hd in range(M_HEADS) for d in range(2)]
    s_refs = (sf_ref, sb_ref)
    lane = lambda hd, d: (2 + d) * M_HEADS + hd
    qts = [qt_ref[hd, 0] for hd in range(M_HEADS)]
    raws = [jnp.dot(k_ref[hd, 0], qts[hd], preferred_element_type=F32) for hd in range(M_HEADS)]
    carried = {(hd, d): jnp.dot(s_refs[d][0, hd].astype(BF16), qts[hd], preferred_element_type=F32)
               for hd, d in pairs}
    mxs = {(hd, d): mx_t[lane(hd, d):lane(hd, d) + 1, :] for hd, d in pairs}
    ws = {(hd, d): jnp.where(keep[d], raws[hd] * jnp.exp2(e[:, lane(hd, d):lane(hd, d) + 1] - mxs[hd, d]),
                             0.0).astype(BF16) for hd, d in pairs}
    vexts = [jnp.concatenate([vt_ref[hd, 0], ones], axis=0) for hd in range(M_HEADS)]
    nds = {(hd, d): (jnp.exp2(m_in[:, lane(hd, d):lane(hd, d) + 1] - mxs[hd, d]) * carried[hd, d]
                     + jnp.dot(vexts[hd], ws[hd, d], preferred_element_type=F32)) for hd, d in pairs}
    hhs = {(hd, d): nds[hd, d][0:HEAD_DIM] / jnp.maximum(
        jnp.abs(nds[hd, d][HEAD_DIM:HEAD_DIM + 1]),
        jnp.exp2(-(cum_t[lane(hd, d):lane(hd, d) + 1, :] + mxs[hd, d]))) for hd, d in pairs}
    outs = []
    for hd in range(M_HEADS):
        hs = hhs[hd, 0] + hhs[hd, 1]
        hn = hs * lax.rsqrt(jnp.mean(hs * hs, axis=0, keepdims=True) + EPS) * nw_ref[hd]
        mo = mot_ref[hd, 0].astype(F32)
        outs.append(hn / (1.0 + jnp.exp(-mo)))
    for jj in range(M_HEADS // 2):
        o2 = jnp.concatenate([outs[2 * jj], outs[2 * jj + 1]], axis=0)
        out_ref[:, jj * LANES:(jj + 1) * LANES] = o2.T.astype(out_ref.dtype)


def _mlstm_out(mqt, mk, mvt, mot, cum, e, emax, sf, mf, sb, mb, nwb):
    _, nblk, _, blk = mqt.shape
    N = nblk * blk
    L = MLSTM_CHUNK
    per = blk // L
    tspec = pl.BlockSpec((M_HEADS, 1, HEAD_DIM, L), lambda c: (0, c // per, 0, c % per))
    kspec = pl.BlockSpec((M_HEADS, 1, L, HEAD_DIM), lambda c: (0, c // per, c % per, 0))
    sspec = pl.BlockSpec((1, M_HEADS, HEAD_DIM + MLSTM_VPAD, HEAD_DIM), lambda c: (c, 0, 0, 0))
    mspec = pl.BlockSpec((1, 8, LANES), lambda c: (c, 0, 0))
    gspec = pl.BlockSpec((L, LANES), lambda c: (c, 0))
    return pl.pallas_call(
        _mlstm_out_kernel,
        grid=(N // L,),
        in_specs=[tspec, kspec, tspec, tspec, gspec, gspec, gspec, sspec, mspec, sspec, mspec,
                  _const_spec(nwb.shape)],
        out_specs=pl.BlockSpec((L, M_WIDTH), lambda c: (c, 0)),
        out_shape=jax.ShapeDtypeStruct((N, M_WIDTH), BF16),
        compiler_params=_cparams(("parallel",)),
        name="mlstm_out",
    )(mqt, mk, mvt, mot, cum, e, emax, sf, mf, sb, mb, nwb)


def _attn_kernel(qt_ref, k_ref, vt_ref, o_ref, m_ref, acc_ref, s0_ref, s1_ref, x0_ref, x1_ref):
    nkb = k_ref.shape[1]
    tk = k_ref.shape[2]
    tq = qt_ref.shape[3]
    units = [(j, c) for j in range(A_GROUP) for c in range(0, tq, ATTN_QW)]
    ones = jnp.ones((ATTN_VPAD, tk), BF16)

    m_ref[...] = jnp.full_like(m_ref, -jnp.inf)
    acc_ref[...] = jnp.zeros_like(acc_ref)

    def stage(cur, nxt):
        if nxt is not None:
            k = k_ref[0, nxt[0]]
        if cur is not None:
            vt = jnp.concatenate([vt_ref[0, cur[0]], ones], axis=0)
        for u, (j, c) in enumerate(units):
            cols = slice(c, c + ATTN_QW)
            if nxt is not None:
                st = jnp.dot(k, qt_ref[j, 0, :, cols], preferred_element_type=F32)
                nxt[1][u] = st
                nxt[2][u] = jnp.max(st, axis=0, keepdims=True)
            if cur is not None:
                st = cur[1][u]
                m_old = m_ref[j, :, cols]
                m_new = jnp.maximum(m_old, cur[2][u])
                alpha = jnp.exp2(m_old - m_new)
                p = jnp.exp2(st - m_new)
                acc_ref[j, :, cols] = (alpha * acc_ref[j, :, cols]
                                       + jnp.dot(vt, p.astype(BF16), preferred_element_type=F32))
                m_ref[j, :, cols] = m_new

    stage(None, (0, s0_ref, x0_ref))

    def body(i, carry):
        kb = 2 * i
        stage((kb, s0_ref, x0_ref), (kb + 1, s1_ref, x1_ref))
        stage((kb + 1, s1_ref, x1_ref), (kb + 2, s0_ref, x0_ref))
        return carry

    lax.fori_loop(0, nkb // 2 - 1, body, 0)
    stage((nkb - 2, s0_ref, x0_ref), (nkb - 1, s1_ref, x1_ref))
    stage((nkb - 1, s1_ref, x1_ref), None)

    def head_out(j):
        return acc_ref[j, 0:HEAD_DIM, :] / acc_ref[j, HEAD_DIM:HEAD_DIM + 1, :]

    for jj in range(A_GROUP // 2):
        o2 = jnp.concatenate([head_out(2 * jj), head_out(2 * jj + 1)], axis=0)
        o_ref[:, jj * LANES:(jj + 1) * LANES] = o2.T.astype(o_ref.dtype)


def _attn(qt, k, vt, T):
    _, nblk, _, blk = qt.shape
    N = nblk * blk
    B = N // T
    nq = T // blk
    assert nq % 2 == 0, "attention pipelines key blocks in pairs"
    nu = A_GROUP * (blk // ATTN_QW)
    return pl.pallas_call(
        _attn_kernel,
        grid=(B, A_KV_HEADS, nq),
        in_specs=[
            pl.BlockSpec((A_GROUP, 1, HEAD_DIM, blk), lambda b, g, qi: (g, b * nq + qi, 0, 0)),
            pl.BlockSpec((1, nq, blk, HEAD_DIM), lambda b, g, qi: (g, b, 0, 0)),
            pl.BlockSpec((1, nq, HEAD_DIM, blk), lambda b, g, qi: (g, b, 0, 0)),
        ],
        out_specs=pl.BlockSpec((blk, A_GROUP * HEAD_DIM), lambda b, g, qi: (b * nq + qi, g)),
        out_shape=jax.ShapeDtypeStruct((N, A_WIDTH), BF16),
        scratch_shapes=[pltpu.VMEM((A_GROUP, 1, blk), F32),
                        pltpu.VMEM((A_GROUP, HEAD_DIM + ATTN_VPAD, blk), F32),
                        pltpu.VMEM((nu, blk, ATTN_QW), F32), pltpu.VMEM((nu, blk, ATTN_QW), F32),
                        pltpu.VMEM((nu, 1, ATTN_QW), F32), pltpu.VMEM((nu, 1, ATTN_QW), F32)],
        compiler_params=_cparams(("parallel", "parallel", "arbitrary")),
        name="attn",
    )(qt, k, vt)


def _ffn_kernel(xp_ref, xm_ref, xn_ref, mp_ref, mm_ref, mn_ref, ap_ref, am_ref, an_ref, wom_ref, woa_ref, n2_ref,
                wup_ref, cw_ref, cb_ref, wdn_ref, fw_ref, y_ref, lhs_ref, x1_ref, u0_ref, u1_ref, acc_ref,
                *, tiles_per_seq):
    i = pl.program_id(0)
    tm = xm_ref.shape[0]
    H = CONV_HALO
    t = i % tiles_per_seq
    xe = jnp.concatenate([xp_ref[...], xm_ref[...], xn_ref[...]], axis=0)
    me = jnp.concatenate([mp_ref[...], mm_ref[...], mn_ref[...]], axis=0)
    ae = jnp.concatenate([ap_ref[...], am_ref[...], an_ref[...]], axis=0)
    x1e = (xe + jnp.dot(me, wom_ref[...], preferred_element_type=F32)
           + jnp.dot(ae, woa_ref[...], preferred_element_type=F32))
    h2e = (x1e * lax.rsqrt(jnp.mean(x1e * x1e, axis=-1, keepdims=True) + EPS) * n2_ref[...]).astype(BF16)
    x1_ref[...] = x1e[H:H + tm]
    lhs_ref[0:H] = jnp.where(t == 0, jnp.zeros((H, D_MODEL), BF16), h2e[0:H])
    lhs_ref[H:H + tm] = h2e[H:H + tm]
    lhs_ref[H + tm:H + tm + H] = jnp.where(t == tiles_per_seq - 1, jnp.zeros((H, D_MODEL), BF16),
                                           h2e[H + tm:H + tm + H])
    acc_ref[...] = jnp.zeros_like(acc_ref)
    nchunk = D_FF // FF_CHUNK

    def up(c, u_ref):
        u_ref[...] = jnp.dot(lhs_ref[...], wup_ref[c], preferred_element_type=F32)

    def gate_down(c, u_ref):
        w = cw_ref[c]
        conv = (u_ref[pl.ds(H - 1, tm), :] * w[0:1] + u_ref[pl.ds(H, tm), :] * w[1:2]
                + u_ref[pl.ds(H + 1, tm), :] * w[2:3] + cb_ref[c])
        a = conv[:, 0:FF_CHUNK]
        gt = conv[:, FF_CHUNK:2 * FF_CHUNK]
        act = (gt / (1.0 + jnp.exp(-gt))) * a
        acc_ref[...] += jnp.dot(act.astype(BF16), wdn_ref[c], preferred_element_type=F32)

    assert nchunk % 2 == 1
    up(0, u0_ref)

    def body(i, carry):
        c = 2 * i
        up(c + 1, u1_ref)
        gate_down(c, u0_ref)
        up(c + 2, u0_ref)
        gate_down(c + 1, u1_ref)
        return carry

    lax.fori_loop(0, nchunk // 2, body, 0)
    gate_down(nchunk - 1, u0_ref)
    x2 = x1_ref[...] + acc_ref[...]
    y_ref[...] = x2 * lax.rsqrt(jnp.mean(x2 * x2, axis=-1, keepdims=True) + EPS) * fw_ref[...]


def _ffn(x2, m_out, a_out, T, wom, woa, n2, wup, cw, cb, wdn, fw, tm):
    N = x2.shape[0]
    H = CONV_HALO
    tps = T // tm
    r = tm // H
    nblk = N // H
    row = lambda i: (i, 0)
    prev = lambda i: (jnp.maximum(i * r - 1, 0), 0)
    nxt = lambda i: (jnp.minimum((i + 1) * r, nblk - 1), 0)

    def halo3(width):
        return [pl.BlockSpec((H, width), prev), pl.BlockSpec((tm, width), row), pl.BlockSpec((H, width), nxt)]

    return pl.pallas_call(
        functools.partial(_ffn_kernel, tiles_per_seq=tps),
        grid=(N // tm,),
        in_specs=halo3(D_MODEL) + halo3(M_WIDTH) + halo3(A_WIDTH) + [
            _const_spec(wom.shape), _const_spec(woa.shape), _const_spec(n2.shape),
            _const_spec(wup.shape), _const_spec(cw.shape), _const_spec(cb.shape), _const_spec(wdn.shape),
            _const_spec(fw.shape),
        ],
        out_specs=pl.BlockSpec((tm, D_MODEL), row),
        out_shape=jax.ShapeDtypeStruct((N, D_MODEL), F32),
        scratch_shapes=[pltpu.VMEM((tm + 2 * H, D_MODEL), BF16), pltpu.VMEM((tm, D_MODEL), F32),
                        pltpu.VMEM((tm + 2 * H, 2 * FF_CHUNK), F32), pltpu.VMEM((tm + 2 * H, 2 * FF_CHUNK), F32),
                        pltpu.VMEM((tm, D_MODEL), F32)],
        compiler_params=_cparams(("parallel",)),
        name="ffn",
    )(x2, x2, x2, m_out, m_out, m_out, a_out, a_out, a_out, wom, woa, n2, wup, cw, cb, wdn, fw)


def _rope_tables(T, tm):
    nf = HEAD_DIM // 4
    inv = ROPE_THETA ** (-jnp.arange(nf, dtype=F32) / nf)
    zero = jnp.zeros((1, 2 * nf), F32)

    def parts(pos, row_part):
        ang = pos[:, None] * inv
        cos, sin = jnp.cos(ang), jnp.sin(ang)
        z = jnp.broadcast_to(zero, (pos.shape[0], 2 * nf))
        halves = lambda a, b: jnp.concatenate([a, b, z] if row_part else [z, a, b], axis=-1)
        tabs = (halves(cos, cos), halves(-sin, 0.0 * sin), halves(0.0 * sin, sin))
        return jnp.stack([jnp.concatenate([t, t], axis=-1) for t in tabs])

    rrow = parts(jnp.arange(T // GRID_W, dtype=F32), True)
    rrow = rrow.reshape(3, T // tm, tm // GRID_W, LANES).transpose(1, 0, 2, 3)
    rcol = parts(jnp.arange(GRID_W, dtype=F32), False)
    return rrow, jnp.tile(rcol, (1, tm // GRID_W, 1))


def _prep_weights(w_in, b_gates, mh_norm_w, q_norm_w, k_norm_w, w_out, norm1_w, norm2_w, w_up, conv_w, conv_b,
                  w_down, final_norm_w):
    gate0 = 4 * M_WIDTH
    wm = w_in[:, :gate0].astype(BF16)
    wg = jnp.pad(w_in[:, gate0:gate0 + N_GATES], ((0, 0), (0, LANES - N_GATES))).astype(BF16)
    wa = w_in[:, gate0 + N_GATES:].astype(BF16)
    bg = jnp.pad(b_gates, (0, LANES - N_GATES)).reshape(1, LANES)
    nchunk = D_FF // FF_CHUNK
    wua = w_up[:, :D_FF].reshape(D_MODEL, nchunk, FF_CHUNK)
    wug = w_up[:, D_FF:].reshape(D_MODEL, nchunk, FF_CHUNK)
    wup = jnp.concatenate([wua, wug], axis=-1).transpose(1, 0, 2).astype(BF16)
    cwa = conv_w[:, :D_FF].reshape(3, nchunk, FF_CHUNK)
    cwg = conv_w[:, D_FF:].reshape(3, nchunk, FF_CHUNK)
    cw = jnp.concatenate([cwa, cwg], axis=-1).transpose(1, 0, 2)
    cb = jnp.concatenate([conv_b[:D_FF].reshape(nchunk, 1, FF_CHUNK),
                          conv_b[D_FF:].reshape(nchunk, 1, FF_CHUNK)], axis=-1)
    wdn = w_down.reshape(nchunk, FF_CHUNK, D_MODEL).astype(BF16)
    return dict(
        wm=wm, wg=wg, wa=wa, bg=bg,
        n1=norm1_w.reshape(1, D_MODEL), n2=norm2_w.reshape(1, D_MODEL), fw=final_norm_w.reshape(1, D_MODEL),
        qw2=jnp.tile(q_norm_w, 2).reshape(1, LANES), kw2=jnp.tile(k_norm_w, 2).reshape(1, LANES),
        nw=mh_norm_w.reshape(1, M_WIDTH),
        wom=w_out[:M_WIDTH].astype(BF16), woa=w_out[M_WIDTH:].astype(BF16),
        wup=wup, cw=cw, cb=cb, wdn=wdn,
    )


def _tiles(T):
    return min(512, T // 2)


def _trunk(x, p):
    B, T, _ = x.shape
    N = B * T
    tm = _tiles(T)
    x2 = x.reshape(N, D_MODEL)
    rrow, rcol = _rope_tables(T, tm)
    blk = np.arange(LANES) // HEAD_DIM
    ones_bd = jnp.asarray(blk[:, None] == blk[None, :], BF16)
    idx = np.arange(tm)
    tri = jnp.asarray((idx[None, :] <= idx[:, None])
                      & (idx[None, :] // MLSTM_CHUNK == idx[:, None] // MLSTM_CHUNK), BF16)

    mq, mk, mv, mo, cum, e, emax, ends, qt, k, vt = _inproj(
        x2, T, p["n1"], p["wm"], p["wg"], p["wa"], p["bg"], p["qw2"], p["kw2"], rrow, rcol, ones_bd, tri, tm)
    nwb = jnp.broadcast_to(p["nw"].reshape(M_HEADS, HEAD_DIM, 1), (M_HEADS, HEAD_DIM, MLSTM_CHUNK))
    s_f, m_f, s_b, m_b = _mlstm_state(mk, mv, e, ends, T)
    m_out = _mlstm_out(mq, mk, mv, mo, cum, e, emax, s_f, m_f, s_b, m_b, nwb)
    a_out = _attn(qt, k, vt, T)
    y = _ffn(x2, m_out, a_out, T, p["wom"], p["woa"], p["n2"], p["wup"], p["cw"], p["cb"], p["wdn"], p["fw"], tm)
    return y.reshape(B, T, D_MODEL)


def kernel(x_prompt, x_sample, w_in, b_gates, mh_norm_w, q_norm_w, k_norm_w, w_out, norm1_w, norm2_w, w_up,
           conv_w, conv_b, w_down, final_norm_w):
    assert w_in.shape[0] == 1, "single-layer trunk"
    p = _prep_weights(w_in[0], b_gates[0], mh_norm_w[0], q_norm_w[0], k_norm_w[0], w_out[0], norm1_w[0],
                      norm2_w[0], w_up[0], conv_w[0], conv_b[0], w_down[0], final_norm_w)
    return (_trunk(x_prompt, p), _trunk(x_sample, p))
```

```python
import functools
import math

import jax
import jax.numpy as jnp
import numpy as np
from jax import lax
from jax.experimental import pallas as pl
from jax.experimental.pallas import tpu as pltpu

F32 = jnp.float32
BF16 = jnp.bfloat16

D_MODEL = 1024
HEAD_DIM = 64
M_HEADS = 8
M_WIDTH = M_HEADS * HEAD_DIM
A_HEADS = 8
A_KV_HEADS = 2
A_GROUP = A_HEADS // A_KV_HEADS
A_WIDTH = A_HEADS * HEAD_DIM
KV_WIDTH = A_KV_HEADS * HEAD_DIM
N_GATES = 4 * M_HEADS
D_FF = 2816
GRID_W = 64
ROPE_THETA = 10000.0
EPS = 1e-6

LANES = 128
VMEM_LIMIT = 56 * 1024 * 1024

MLSTM_CHUNK = 256
MLSTM_VPAD = 16
FF_CHUNK = 256
CONV_HALO = 16
NEG = -1e30
ATTN_QW = 256
ATTN_VPAD = 16


def _cparams(sem):
    return pltpu.CompilerParams(dimension_semantics=sem, vmem_limit_bytes=VMEM_LIMIT)


def _const_spec(shape):
    nd = len(shape)
    return pl.BlockSpec(shape, lambda *_: (0,) * nd, pipeline_mode=pl.Buffered(1))


def _inproj_kernel(x_ref, n1_ref, wm_ref, wg_ref, wa_ref, bg_ref, qw_ref, kw_ref, rrow_ref, rcol_ref,
                   ones_ref, tri_ref, mq_ref, mk_ref, mv_ref, mo_ref, cum_ref, e_ref, emax_ref, ends_ref,
                   qt_ref, k_ref, vt_ref):
    x = x_ref[...]
    h = x * lax.rsqrt(jnp.mean(x * x, axis=-1, keepdims=True) + EPS) * n1_ref[...]
    h = h.astype(BF16)

    g = jnp.dot(h, wg_ref[...], preferred_element_type=F32) + bg_ref[...]
    cum_ref[...], e_ref[...], emax_ref[...], ends_ref[...] = _gate_sums(g, tri_ref[...])

    pa = jnp.dot(h, wa_ref[...], preferred_element_type=F32)
    tm = x.shape[0]
    rrow = rrow_ref[0]
    cos, sa, sb = (jnp.broadcast_to(rrow[i][:, None, :], (tm // GRID_W, GRID_W, LANES)).reshape(tm, LANES)
                   + rcol_ref[i] for i in range(3))
    ones_bd = ones_ref[...]

    def norm_rope(y, w):
        sq = y * y
        hi = sq.astype(BF16)
        lo = (sq - hi.astype(F32)).astype(BF16)
        ss = (jnp.dot(hi, ones_bd, preferred_element_type=F32)
              + jnp.dot(lo, ones_bd, preferred_element_type=F32))
        yn = y * lax.rsqrt(ss * (1.0 / HEAD_DIM) + EPS) * w
        up = pltpu.roll(yn, LANES - HEAD_DIM // 4, axis=1)
        dn = pltpu.roll(yn, HEAD_DIM // 4, axis=1)
        return yn * cos + up * sa + dn * sb

    qscale = (HEAD_DIM ** -0.5) * math.log2(math.e)
    for c in range(A_WIDTH // LANES):
        q2 = norm_rope(pa[:, c * LANES:(c + 1) * LANES], qw_ref[...]) * qscale
        q2t = q2.T.astype(BF16)
        qt_ref[2 * c, 0] = q2t[0:HEAD_DIM]
        qt_ref[2 * c + 1, 0] = q2t[HEAD_DIM:2 * HEAD_DIM]
    k2 = norm_rope(pa[:, A_WIDTH:A_WIDTH + KV_WIDTH], kw_ref[...]).astype(BF16)
    k_ref[0, 0] = k2[:, 0:HEAD_DIM]
    k_ref[1, 0] = k2[:, HEAD_DIM:2 * HEAD_DIM]
    v2t = pa[:, A_WIDTH + KV_WIDTH:A_WIDTH + 2 * KV_WIDTH].T.astype(BF16)
    vt_ref[0, 0] = v2t[0:HEAD_DIM]
    vt_ref[1, 0] = v2t[HEAD_DIM:2 * HEAD_DIM]

    pm = jnp.dot(h, wm_ref[...], preferred_element_type=F32)

    def heads_t(cols0, ref):
        for c in range(M_WIDTH // LANES):
            t2 = pm[:, cols0 + c * LANES:cols0 + (c + 1) * LANES].T.astype(BF16)
            ref[2 * c, 0] = t2[0:HEAD_DIM]
            ref[2 * c + 1, 0] = t2[HEAD_DIM:2 * HEAD_DIM]

    heads_t(0, mq_ref)
    heads_t(2 * M_WIDTH, mv_ref)
    heads_t(3 * M_WIDTH, mo_ref)
    for hd in range(M_HEADS):
        c0 = M_WIDTH + hd * HEAD_DIM
        mk_ref[hd, 0] = (pm[:, c0:c0 + HEAD_DIM] * (HEAD_DIM ** -0.5)).astype(BF16)


def _inproj(x2, T, n1, wm, wg, wa, bg, qw2, kw2, rrow, rcol, ones_bd, tri, tm):
    N = x2.shape[0]
    tps = T // tm
    row = lambda i: (i, 0)
    out_shape = (
        jax.ShapeDtypeStruct((M_HEADS, N // tm, HEAD_DIM, tm), BF16),
        jax.ShapeDtypeStruct((M_HEADS, N // tm, tm, HEAD_DIM), BF16),
        jax.ShapeDtypeStruct((M_HEADS, N // tm, HEAD_DIM, tm), BF16),
        jax.ShapeDtypeStruct((M_HEADS, N // tm, HEAD_DIM, tm), BF16),
        jax.ShapeDtypeStruct((N, LANES), F32), jax.ShapeDtypeStruct((N, LANES), F32),
        jax.ShapeDtypeStruct((N, LANES), F32), jax.ShapeDtypeStruct((N // MLSTM_CHUNK, 8, LANES), F32),
        jax.ShapeDtypeStruct((A_HEADS, N // tm, HEAD_DIM, tm), BF16),
        jax.ShapeDtypeStruct((A_KV_HEADS, N // tm, tm, HEAD_DIM), BF16),
        jax.ShapeDtypeStruct((A_KV_HEADS, N // tm, HEAD_DIM, tm), BF16),
    )
    return pl.pallas_call(
        _inproj_kernel,
        grid=(N // tm,),
        in_specs=[
            pl.BlockSpec((tm, D_MODEL), row),
            _const_spec(n1.shape), _const_spec(wm.shape), _const_spec(wg.shape), _const_spec(wa.shape),
            _const_spec(bg.shape), _const_spec(qw2.shape), _const_spec(kw2.shape),
            pl.BlockSpec((1,) + rrow.shape[1:], lambda i: (i % tps, 0, 0, 0)), _const_spec(rcol.shape),
            _const_spec(ones_bd.shape), _const_spec(tri.shape),
        ],
        out_specs=(
            pl.BlockSpec((M_HEADS, 1, HEAD_DIM, tm), lambda i: (0, i, 0, 0)),
            pl.BlockSpec((M_HEADS, 1, tm, HEAD_DIM), lambda i: (0, i, 0, 0)),
            pl.BlockSpec((M_HEADS, 1, HEAD_DIM, tm), lambda i: (0, i, 0, 0)),
            pl.BlockSpec((M_HEADS, 1, HEAD_DIM, tm), lambda i: (0, i, 0, 0)),
            pl.BlockSpec((tm, LANES), row), pl.BlockSpec((tm, LANES), row), pl.BlockSpec((tm, LANES), row),
            pl.BlockSpec((tm // MLSTM_CHUNK, 8, LANES), lambda i: (i, 0, 0)),
            pl.BlockSpec((A_HEADS, 1, HEAD_DIM, tm), lambda i: (0, i, 0, 0)),
            pl.BlockSpec((A_KV_HEADS, 1, tm, HEAD_DIM), lambda i: (0, i, 0, 0)),
            pl.BlockSpec((A_KV_HEADS, 1, HEAD_DIM, tm), lambda i: (0, i, 0, 0)),
        ),
        out_shape=out_shape,
        compiler_params=_cparams(("parallel",)),
        name="inproj",
    )(x2, n1, wm, wg, wa, bg, qw2, kw2, rrow, rcol, ones_bd, tri)


def _log_sigmoid(x):
    return jnp.minimum(x, 0.0) - jnp.log1p(jnp.exp(-jnp.abs(x)))


def _split3(x):
    a = x.astype(BF16)
    r = x - a.astype(F32)
    b = r.astype(BF16)
    c = (r - b.astype(F32)).astype(BF16)
    return a, b, c


def _gate_sums(g, tri):
    tm = g.shape[0]
    logf = _log_sigmoid(g) * math.log2(math.e)
    a, b, c = _split3(logf)
    pre = (jnp.dot(tri, a, preferred_element_type=F32) + jnp.dot(tri, b, preferred_element_type=F32)
           + jnp.dot(tri, c, preferred_element_type=F32))
    tot = jnp.concatenate([jnp.broadcast_to(pre[r + MLSTM_CHUNK - 1:r + MLSTM_CHUNK], (MLSTM_CHUNK, LANES))
                           for r in range(0, tm, MLSTM_CHUNK)], axis=0)
    fwd = _is_fwd_lane((tm, LANES))
    cum = jnp.where(fwd, pre, tot - pre + logf)
    e = pltpu.roll(g, 2 * M_HEADS, axis=1) * math.log2(math.e) - cum

    row = lax.broadcasted_iota(jnp.int32, (tm, LANES), 0) & (MLSTM_CHUNK - 1)
    up, down = e, e
    shift = 1
    while shift < MLSTM_CHUNK:
        up = jnp.maximum(up, jnp.where(row >= shift, pltpu.roll(up, shift, axis=0), NEG))
        down = jnp.maximum(down, jnp.where(row < MLSTM_CHUNK - shift, pltpu.roll(down, tm - shift, axis=0), NEG))
        shift *= 2
    ends = [jnp.concatenate([pre[r + MLSTM_CHUNK - 1:r + MLSTM_CHUNK],
                             jnp.where(fwd[0:1], up[r + MLSTM_CHUNK - 1:r + MLSTM_CHUNK], down[r:r + 1]),
                             jnp.zeros((6, LANES), F32)], axis=0) for r in range(0, tm, MLSTM_CHUNK)]
    return cum, e, jnp.where(fwd, up, down), jnp.stack(ends)


def _is_fwd_lane(shape):
    return lax.broadcasted_iota(jnp.int32, shape, len(shape) - 1) < 3 * M_HEADS


def _mlstm_state_kernel(kf_ref, vtf_ref, ef_ref, endf_ref, kb_ref, vtb_ref, eb_ref, endb_ref,
                        sf_ref, mf_ref, sb_ref, mb_ref, c_ref, m_ref):
    L = MLSTM_CHUNK
    per = ef_ref.shape[0] // L

    @pl.when(pl.program_id(1) == 0)
    def _():
        c_ref[...] = jnp.zeros_like(c_ref)
        m_ref[...] = jnp.zeros_like(m_ref)

    fwd = _is_fwd_lane((1, LANES))
    ones = jnp.ones((MLSTM_VPAD, L), BF16)
    pairs = [(d, hd) for d in range(2) for hd in range(M_HEADS)]
    k_refs, vt_refs, s_refs = (kf_ref, kb_ref), (vtf_ref, vtb_ref), (sf_ref, sb_ref)
    lane = lambda d, hd: (2 + d) * M_HEADS + hd

    for j in range(per):
        jb = per - 1 - j
        rows = (slice(j * L, (j + 1) * L), slice(jb * L, (jb + 1) * L))
        tot_f = endf_ref[j, 0:1, :]
        tot_b = endb_ref[jb, 0:1, :]
        wend_f = tot_f + ef_ref[rows[0], :]
        wend_b = tot_b + eb_ref[rows[1], :]
        tot = jnp.where(fwd, tot_f, tot_b)
        m_prev = m_ref[...]
        m_new = jnp.maximum(tot + m_prev, tot + jnp.where(fwd, endf_ref[j, 1:2, :], endb_ref[jb, 1:2, :]))
        dec = jnp.exp2(tot + m_prev - m_new)
        m_ref[...] = m_new
        m_rows = jnp.broadcast_to(m_prev, (8, LANES))
        mf_ref[j] = m_rows
        mb_ref[jb] = m_rows
        we_t = (jnp.exp2(wend_f - m_new).T, jnp.exp2(wend_b - m_new).T)

        wvs = {}
        for d, hd in pairs:
            vext = jnp.concatenate([vt_refs[d][hd, 0, :, rows[d]], ones], axis=0)
            wvs[d, hd] = (vext.astype(F32) * we_t[d][lane(d, hd):lane(d, hd) + 1, :]).astype(BF16)
        adds = {(d, hd): jnp.dot(wvs[d, hd], k_refs[d][hd, 0, rows[d], :], preferred_element_type=F32)
                for d, hd in pairs}
        for d, hd in pairs:
            cext = c_ref[d * M_HEADS + hd]
            s_refs[d][(j, jb)[d], hd] = cext
            c_ref[d * M_HEADS + hd] = dec[:, lane(d, hd):lane(d, hd) + 1] * cext + adds[d, hd]


def _mlstm_state(mk, mvt, e, ends, T):
    _, nblk, blk, _ = mk.shape
    N = nblk * blk
    L = MLSTM_CHUNK
    per = blk // L
    nt = T // blk
    B = N // T
    tf = lambda b, t: b * nt + t
    tb = lambda b, t: b * nt + (nt - 1 - t)

    def specs(ti):
        return [pl.BlockSpec((M_HEADS, 1, blk, HEAD_DIM), lambda b, t: (0, ti(b, t), 0, 0)),
                pl.BlockSpec((M_HEADS, 1, HEAD_DIM, blk), lambda b, t: (0, ti(b, t), 0, 0)),
                pl.BlockSpec((blk, LANES), lambda b, t: (ti(b, t), 0)),
                pl.BlockSpec((per, 8, LANES), lambda b, t: (ti(b, t), 0, 0))]

    def outs(ti):
        return [pl.BlockSpec((per, M_HEADS, HEAD_DIM + MLSTM_VPAD, HEAD_DIM), lambda b, t: (ti(b, t), 0, 0, 0)),
                pl.BlockSpec((per, 8, LANES), lambda b, t: (ti(b, t), 0, 0))]

    s_shape = jax.ShapeDtypeStruct((N // L, M_HEADS, HEAD_DIM + MLSTM_VPAD, HEAD_DIM), F32)
    m_shape = jax.ShapeDtypeStruct((N // L, 8, LANES), F32)
    return pl.pallas_call(
        _mlstm_state_kernel,
        grid=(B, nt),
        in_specs=specs(tf) + specs(tb),
        out_specs=tuple(outs(tf) + outs(tb)),
        out_shape=(s_shape, m_shape, s_shape, m_shape),
        scratch_shapes=[pltpu.VMEM((2 * M_HEADS, HEAD_DIM + MLSTM_VPAD, HEAD_DIM), F32), pltpu.VMEM((1, LANES), F32)],
        compiler_params=_cparams(("parallel", "arbitrary")),
        name="mlstm_state",
    )(mk, mvt, e, ends, mk, mvt, e, ends)


def _mlstm_out_kernel(qt_ref, k_ref, vt_ref, mot_ref, cum_ref, e_ref, emax_ref, sf_ref, mf_ref, sb_ref, mb_ref,
                      nw_ref, out_ref):
    L = e_ref.shape[0]
    e = e_ref[...]
    m_in = jnp.where(_is_fwd_lane((1, LANES)), mf_ref[0, 0:1, :], mb_ref[0, 0:1, :])
    cum_t = cum_ref[...].T
    mx_t = jnp.maximum(emax_ref[...], m_in).T

    si = lax.broadcasted_iota(jnp.int32, (L, L), 0)
    li = lax.broadcasted_iota(jnp.int32, (L, L), 1)
    keep = (si <= li, si >= li)
    ones = jnp.ones((MLSTM_VPAD, L), BF16)

    pairs = [(hd, d) for hd in range(M_HEADS) for d in range(2)]
    s_refs = (sf_ref, sb_ref)
    lane = lambda hd, d: (2 + d) * M_HEADS + hd
    qts = [qt_ref[hd, 0] for hd in range(M_HEADS)]
    raws = [jnp.dot(k_ref[hd, 0], qts[hd], preferred_element_type=F32) for hd in range(M_HEADS)]
    carried = {(hd, d): jnp.dot(s_refs[d][0, hd].astype(BF16), qts[hd], preferred_element_type=F32)
               for hd, d in pairs}
    mxs = {(hd, d): mx_t[lane(hd, d):lane(hd, d) + 1, :] for hd, d in pairs}
    ws = {(hd, d): jnp.where(keep[d], raws[hd] * jnp.exp2(e[:, lane(hd, d):lane(hd, d) + 1] - mxs[hd, d]),
                             0.0).astype(BF16) for hd, d in pairs}
    vexts = [jnp.concatenate([vt_ref[hd, 0], ones], axis=0) for hd in range(M_HEADS)]
    nds = {(hd, d): (jnp.exp2(m_in[:, lane(hd, d):lane(hd, d) + 1] - mxs[hd, d]) * carried[hd, d]
                     + jnp.dot(vexts[hd], ws[hd, d], preferred_element_type=F32)) for hd, d in pairs}
    hhs = {(hd, d): nds[hd, d][0:HEAD_DIM] / jnp.maximum(
        jnp.abs(nds[hd, d][HEAD_DIM:HEAD_DIM + 1]),
        jnp.exp2(-(cum_t[lane(hd, d):lane(hd, d) + 1, :] + mxs[hd, d]))) for hd, d in pairs}
    outs = []
    for hd in range(M_HEADS):
        hs = hhs[hd, 0] + hhs[hd, 1]
        hn = hs * lax.rsqrt(jnp.mean(hs * hs, axis=0, keepdims=True) + EPS) * nw_ref[hd]
        mo = mot_ref[hd, 0].astype(F32)
        outs.append(hn / (1.0 + jnp.exp(-mo)))
    for jj in range(M_HEADS // 2):
        o2 = jnp.concatenate([outs[2 * jj], outs[2 * jj + 1]], axis=0)
        out_ref[:, jj * LANES:(jj + 1) * LANES] = o2.T.astype(out_ref.dtype)


def _mlstm_out(mqt, mk, mvt, mot, cum, e, emax, sf, mf, sb, mb, nwb):
    _, nblk, _, blk = mqt.shape
    N = nblk * blk
    L = MLSTM_CHUNK
    per = blk // L
    tspec = pl.BlockSpec((M_HEADS, 1, HEAD_DIM, L), lambda c: (0, c // per, 0, c % per))
    kspec = pl.BlockSpec((M_HEADS, 1, L, HEAD_DIM), lambda c: (0, c // per, c % per, 0))
    sspec = pl.BlockSpec((1, M_HEADS, HEAD_DIM + MLSTM_VPAD, HEAD_DIM), lambda c: (c, 0, 0, 0))
    mspec = pl.BlockSpec((1, 8, LANES), lambda c: (c, 0, 0))
    gspec = pl.BlockSpec((L, LANES), lambda c: (c, 0))
    return pl.pallas_call(
        _mlstm_out_kernel,
        grid=(N // L,),
        in_specs=[tspec, kspec, tspec, tspec, gspec, gspec, gspec, sspec, mspec, sspec, mspec,
                  _const_spec(nwb.shape)],
        out_specs=pl.BlockSpec((L, M_WIDTH), lambda c: (c, 0)),
        out_shape=jax.ShapeDtypeStruct((N, M_WIDTH), BF16),
        compiler_params=_cparams(("parallel",)),
        name="mlstm_out",
    )(mqt, mk, mvt, mot, cum, e, emax, sf, mf, sb, mb, nwb)


def _attn_kernel(qt_ref, k_ref, vt_ref, o_ref, m_ref, acc_ref, s0_ref, s1_ref, x0_ref, x1_ref):
    nkb = k_ref.shape[1]
    tk = k_ref.shape[2]
    tq = qt_ref.shape[3]
    units = [(j, c) for j in range(A_GROUP) for c in range(0, tq, ATTN_QW)]
    ones = jnp.ones((ATTN_VPAD, tk), BF16)

    m_ref[...] = jnp.full_like(m_ref, -jnp.inf)
    acc_ref[...] = jnp.zeros_like(acc_ref)

    def stage(cur, nxt):
        if nxt is not None:
            k = k_ref[0, nxt[0]]
        if cur is not None:
            vt = jnp.concatenate([vt_ref[0, cur[0]], ones], axis=0)
        for u, (j, c) in enumerate(units):
            cols = slice(c, c + ATTN_QW)
            if nxt is not None:
                st = jnp.dot(k, qt_ref[j, 0, :, cols], preferred_element_type=F32)
                nxt[1][u] = st
                nxt[2][u] = jnp.max(st, axis=0, keepdims=True)
            if cur is not None:
                st = cur[1][u]
                m_old = m_ref[j, :, cols]
                m_new = jnp.maximum(m_old, cur[2][u])
                alpha = jnp.exp2(m_old - m_new)
                p = jnp.exp2(st - m_new)
                acc_ref[j, :, cols] = (alpha * acc_ref[j, :, cols]
                                       + jnp.dot(vt, p.astype(BF16), preferred_element_type=F32))
                m_ref[j, :, cols] = m_new

    stage(None, (0, s0_ref, x0_ref))

    def body(i, carry):
        kb = 2 * i
        stage((kb, s0_ref, x0_ref), (kb + 1, s1_ref, x1_ref))
        stage((kb + 1, s1_ref, x1_ref), (kb + 2, s0_ref, x0_ref))
        return carry

    lax.fori_loop(0, nkb // 2 - 1, body, 0)
    stage((nkb - 2, s0_ref, x0_ref), (nkb - 1, s1_ref, x1_ref))
    stage((nkb - 1, s1_ref, x1_ref), None)

    def head_out(j):
        return acc_ref[j, 0:HEAD_DIM, :] / acc_ref[j, HEAD_DIM:HEAD_DIM + 1, :]

    for jj in range(A_GROUP // 2):
        o2 = jnp.concatenate([head_out(2 * jj), head_out(2 * jj + 1)], axis=0)
        o_ref[:, jj * LANES:(jj + 1) * LANES] = o2.T.astype(o_ref.dtype)


def _attn(qt, k, vt, T):
    _, nblk, _, blk = qt.shape
    N = nblk * blk
    B = N // T
    nq = T // blk
    assert nq % 2 == 0, "attention pipelines key blocks in pairs"
    nu = A_GROUP * (blk // ATTN_QW)
    return pl.pallas_call(
        _attn_kernel,
        grid=(B, A_KV_HEADS, nq),
        in_specs=[
            pl.BlockSpec((A_GROUP, 1, HEAD_DIM, blk), lambda b, g, qi: (g, b * nq + qi, 0, 0)),
            pl.BlockSpec((1, nq, blk, HEAD_DIM), lambda b, g, qi: (g, b, 0, 0)),
            pl.BlockSpec((1, nq, HEAD_DIM, blk), lambda b, g, qi: (g, b, 0, 0)),
        ],
        out_specs=pl.BlockSpec((blk, A_GROUP * HEAD_DIM), lambda b, g, qi: (b * nq + qi, g)),
        out_shape=jax.ShapeDtypeStruct((N, A_WIDTH), BF16),
        scratch_shapes=[pltpu.VMEM((A_GROUP, 1, blk), F32),
                        pltpu.VMEM((A_GROUP, HEAD_DIM + ATTN_VPAD, blk), F32),
                        pltpu.VMEM((nu, blk, ATTN_QW), F32), pltpu.VMEM((nu, blk, ATTN_QW), F32),
                        pltpu.VMEM((nu, 1, ATTN_QW), F32), pltpu.VMEM((nu, 1, ATTN_QW), F32)],
        compiler_params=_cparams(("parallel", "parallel", "arbitrary")),
        name="attn",
    )(qt, k, vt)


def _ffn_kernel(xp_ref, xm_ref, xn_ref, mp_ref, mm_ref, mn_ref, ap_ref, am_ref, an_ref, wom_ref, woa_ref, n2_ref,
                wup_ref, cw_ref, cb_ref, wdn_ref, fw_ref, y_ref, lhs_ref, x1_ref, u0_ref, u1_ref, acc_ref,
                *, tiles_per_seq):
    i = pl.program_id(0)
    tm = xm_ref.shape[0]
    H = CONV_HALO
    t = i % tiles_per_seq
    xe = jnp.concatenate([xp_ref[...], xm_ref[...], xn_ref[...]], axis=0)
    me = jnp.concatenate([mp_ref[...], mm_ref[...], mn_ref[...]], axis=0)
    ae = jnp.concatenate([ap_ref[...], am_ref[...], an_ref[...]], axis=0)
    x1e = (xe + jnp.dot(me, wom_ref[...], preferred_element_type=F32)
           + jnp.dot(ae, woa_ref[...], preferred_element_type=F32))
    h2e = (x1e * lax.rsqrt(jnp.mean(x1e * x1e, axis=-1, keepdims=True) + EPS) * n2_ref[...]).astype(BF16)
    x1_ref[...] = x1e[H:H + tm]
    lhs_ref[0:H] = jnp.where(t == 0, jnp.zeros((H, D_MODEL), BF16), h2e[0:H])
    lhs_ref[H:H + tm] = h2e[H:H + tm]
    lhs_ref[H + tm:H + tm + H] = jnp.where(t == tiles_per_seq - 1, jnp.zeros((H, D_MODEL), BF16),
                                           h2e[H + tm:H + tm + H])
    acc_ref[...] = jnp.zeros_like(acc_ref)
    nchunk = D_FF // FF_CHUNK

    def cols(ref, c):
        off = c * FF_CHUNK if isinstance(c, int) else pl.multiple_of(c * FF_CHUNK, FF_CHUNK)
        return ref[:, pl.ds(off, FF_CHUNK)], ref[:, pl.ds(D_FF + off, FF_CHUNK)]

    def up(c, u_ref):
        lhs = lhs_ref[...]
        wa, wg = cols(wup_ref, c)
        u_ref[:, 0:FF_CHUNK] = jnp.dot(lhs, wa, preferred_element_type=F32)
        u_ref[:, FF_CHUNK:] = jnp.dot(lhs, wg, preferred_element_type=F32)

    def gate_down(c, u_ref):
        w = jnp.concatenate(cols(cw_ref, c), axis=1)
        b = jnp.concatenate(cols(cb_ref, c), axis=1)
        conv = (u_ref[pl.ds(H - 1, tm), :] * w[0:1] + u_ref[pl.ds(H, tm), :] * w[1:2]
                + u_ref[pl.ds(H + 1, tm), :] * w[2:3] + b)
        a = conv[:, 0:FF_CHUNK]
        gt = conv[:, FF_CHUNK:2 * FF_CHUNK]
        act = (gt / (1.0 + jnp.exp(-gt))) * a
        row0 = c * FF_CHUNK if isinstance(c, int) else pl.multiple_of(c * FF_CHUNK, FF_CHUNK)
        acc_ref[...] += jnp.dot(act.astype(BF16), wdn_ref[pl.ds(row0, FF_CHUNK), :], preferred_element_type=F32)

    assert nchunk % 2 == 1
    up(0, u0_ref)

    def body(i, carry):
        c = 2 * i
        up(c + 1, u1_ref)
        gate_down(c, u0_ref)
        up(c + 2, u0_ref)
        gate_down(c + 1, u1_ref)
        return carry

    lax.fori_loop(0, nchunk // 2, body, 0)
    gate_down(nchunk - 1, u0_ref)
    x2 = x1_ref[...] + acc_ref[...]
    y_ref[...] = x2 * lax.rsqrt(jnp.mean(x2 * x2, axis=-1, keepdims=True) + EPS) * fw_ref[...]


def _ffn(x2, m_out, a_out, T, wom, woa, n2, wup, cw, cb, wdn, fw, tm):
    N = x2.shape[0]
    H = CONV_HALO
    tps = T // tm
    r = tm // H
    nblk = N // H
    row = lambda i: (i, 0)
    prev = lambda i: (jnp.maximum(i * r - 1, 0), 0)
    nxt = lambda i: (jnp.minimum((i + 1) * r, nblk - 1), 0)

    def halo3(width):
        return [pl.BlockSpec((H, width), prev), pl.BlockSpec((tm, width), row), pl.BlockSpec((H, width), nxt)]

    return pl.pallas_call(
        functools.partial(_ffn_kernel, tiles_per_seq=tps),
        grid=(N // tm,),
        in_specs=halo3(D_MODEL) + halo3(M_WIDTH) + halo3(A_WIDTH) + [
            _const_spec(wom.shape), _const_spec(woa.shape), _const_spec(n2.shape),
            _const_spec(wup.shape), _const_spec(cw.shape), _const_spec(cb.shape), _const_spec(wdn.shape),
            _const_spec(fw.shape),
        ],
        out_specs=pl.BlockSpec((tm, D_MODEL), row),
        out_shape=jax.ShapeDtypeStruct((N, D_MODEL), F32),
        scratch_shapes=[pltpu.VMEM((tm + 2 * H, D_MODEL), BF16), pltpu.VMEM((tm, D_MODEL), F32),
                        pltpu.VMEM((tm + 2 * H, 2 * FF_CHUNK), F32), pltpu.VMEM((tm + 2 * H, 2 * FF_CHUNK), F32),
                        pltpu.VMEM((tm, D_MODEL), F32)],
        compiler_params=_cparams(("parallel",)),
        name="ffn",
    )(x2, x2, x2, m_out, m_out, m_out, a_out, a_out, a_out, wom, woa, n2, wup, cw, cb, wdn, fw)


def _rope_tables(T, tm):
    nf = HEAD_DIM // 4
    inv = ROPE_THETA ** (-jnp.arange(nf, dtype=F32) / nf)
    zero = jnp.zeros((1, 2 * nf), F32)

    def parts(pos, row_part):
        ang = pos[:, None] * inv
        cos, sin = jnp.cos(ang), jnp.sin(ang)
        z = jnp.broadcast_to(zero, (pos.shape[0], 2 * nf))
        halves = lambda a, b: jnp.concatenate([a, b, z] if row_part else [z, a, b], axis=-1)
        tabs = (halves(cos, cos), halves(-sin, 0.0 * sin), halves(0.0 * sin, sin))
        return jnp.stack([jnp.concatenate([t, t], axis=-1) for t in tabs])

    rrow = parts(jnp.arange(T // GRID_W, dtype=F32), True)
    rrow = rrow.reshape(3, T // tm, tm // GRID_W, LANES).transpose(1, 0, 2, 3)
    rcol = parts(jnp.arange(GRID_W, dtype=F32), False)
    return rrow, jnp.tile(rcol, (1, tm // GRID_W, 1))


def _prep_weights(w_in, b_gates, mh_norm_w, q_norm_w, k_norm_w, w_out, norm1_w, norm2_w, w_up, conv_w, conv_b,
                  w_down, final_norm_w):
    gate0 = 4 * M_WIDTH
    wm = w_in[:, :gate0].astype(BF16)
    wg = jnp.pad(w_in[:, gate0:gate0 + N_GATES], ((0, 0), (0, LANES - N_GATES))).astype(BF16)
    wa = w_in[:, gate0 + N_GATES:].astype(BF16)
    bg = jnp.pad(b_gates, (0, LANES - N_GATES)).reshape(1, LANES)
    return dict(
        wm=wm, wg=wg, wa=wa, bg=bg,
        n1=norm1_w.reshape(1, D_MODEL), n2=norm2_w.reshape(1, D_MODEL), fw=final_norm_w.reshape(1, D_MODEL),
        qw2=jnp.tile(q_norm_w, 2).reshape(1, LANES), kw2=jnp.tile(k_norm_w, 2).reshape(1, LANES),
        nw=mh_norm_w.reshape(1, M_WIDTH),
        wom=w_out[:M_WIDTH].astype(BF16), woa=w_out[M_WIDTH:].astype(BF16),
        wup=w_up.astype(BF16), cw=conv_w, cb=conv_b.reshape(1, 2 * D_FF), wdn=w_down.astype(BF16),
    )


def _tiles(T):
    return min(512, T // 2)


def _trunk(x, p):
    B, T, _ = x.shape
    N = B * T
    tm = _tiles(T)
    x2 = x.reshape(N, D_MODEL)
    rrow, rcol = _rope_tables(T, tm)
    blk = np.arange(LANES) // HEAD_DIM
    ones_bd = jnp.asarray(blk[:, None] == blk[None, :], BF16)
    idx = np.arange(tm)
    tri = jnp.asarray((idx[None, :] <= idx[:, None])
                      & (idx[None, :] // MLSTM_CHUNK == idx[:, None] // MLSTM_CHUNK), BF16)

    mq, mk, mv, mo, cum, e, emax, ends, qt, k, vt = _inproj(
        x2, T, p["n1"], p["wm"], p["wg"], p["wa"], p["bg"], p["qw2"], p["kw2"], rrow, rcol, ones_bd, tri, tm)
    nwb = jnp.broadcast_to(p["nw"].reshape(M_HEADS, HEAD_DIM, 1), (M_HEADS, HEAD_DIM, MLSTM_CHUNK))
    s_f, m_f, s_b, m_b = _mlstm_state(mk, mv, e, ends, T)
    m_out = _mlstm_out(mq, mk, mv, mo, cum, e, emax, s_f, m_f, s_b, m_b, nwb)
    a_out = _attn(qt, k, vt, T)
    y = _ffn(x2, m_out, a_out, T, p["wom"], p["woa"], p["n2"], p["wup"], p["cw"], p["cb"], p["wdn"], p["fw"], tm)
    return y.reshape(B, T, D_MODEL)


def kernel(x_prompt, x_sample, w_in, b_gates, mh_norm_w, q_norm_w, k_norm_w, w_out, norm1_w, norm2_w, w_up,
           conv_w, conv_b, w_down, final_norm_w):
    assert w_in.shape[0] == 1, "single-layer trunk"
    p = _prep_weights(w_in[0], b_gates[0], mh_norm_w[0], q_norm_w[0], k_norm_w[0], w_out[0], norm1_w[0],
                      norm2_w[0], w_up[0], conv_w[0], conv_b[0], w_down[0], final_norm_w)
    return (_trunk(x_prompt, p), _trunk(x_sample, p))
```

```python
import functools
import math

import jax
import jax.numpy as jnp
import numpy as np
from jax import lax
from jax.experimental import pallas as pl
from jax.experimental.pallas import tpu as pltpu

F32 = jnp.float32
BF16 = jnp.bfloat16

D_MODEL = 1024
HEAD_DIM = 64
M_HEADS = 8
M_WIDTH = M_HEADS * HEAD_DIM
A_HEADS = 8
A_KV_HEADS = 2
A_GROUP = A_HEADS // A_KV_HEADS
A_WIDTH = A_HEADS * HEAD_DIM
KV_WIDTH = A_KV_HEADS * HEAD_DIM
N_GATES = 4 * M_HEADS
D_FF = 2816
GRID_W = 64
ROPE_THETA = 10000.0
EPS = 1e-6

LANES = 128
SUBLANES = 8
VMEM_LIMIT = 56 * 1024 * 1024

MLSTM_CHUNK = 256
MLSTM_VPAD = 16
MLSTM_STATE_TILES = 2
FF_CHUNK = 256
CONV_HALO = 16
NEG = -1e30
ATTN_QW = 256
ATTN_VPAD = 16


def _cparams(sem):
    return pltpu.CompilerParams(dimension_semantics=sem, vmem_limit_bytes=VMEM_LIMIT)


def _const_spec(shape):
    nd = len(shape)
    return pl.BlockSpec(shape, lambda *_: (0,) * nd, pipeline_mode=pl.Buffered(1))


def _inproj_kernel(x_ref, n1_ref, wm_ref, wg_ref, wa_ref, bg_ref, qw_ref, kw_ref, rrow_ref, rcol_ref,
                   ones_ref, tri_ref, mq_ref, mk_ref, mv_ref, mo_ref, cum_ref, e_ref, emax_ref, ends_ref,
                   qt_ref, k_ref, vt_ref):
    x = x_ref[...]
    h = x * lax.rsqrt(jnp.mean(x * x, axis=-1, keepdims=True) + EPS) * n1_ref[...]
    h = h.astype(BF16)

    g = jnp.dot(h, wg_ref[...], preferred_element_type=F32) + bg_ref[...]
    cum_ref[...], e_ref[...], emax_ref[...], ends_ref[...] = _gate_sums(g, tri_ref[...])

    pa = jnp.dot(h, wa_ref[...], preferred_element_type=F32)
    tm = x.shape[0]
    rrow = rrow_ref[0]
    cos, sa, sb = (jnp.broadcast_to(rrow[i][:, None, :], (tm // GRID_W, GRID_W, LANES)).reshape(tm, LANES)
                   + rcol_ref[i] for i in range(3))
    ones_bd = ones_ref[...]

    def norm_rope(y, w):
        sq = y * y
        hi = sq.astype(BF16)
        lo = (sq - hi.astype(F32)).astype(BF16)
        ss = (jnp.dot(hi, ones_bd, preferred_element_type=F32)
              + jnp.dot(lo, ones_bd, preferred_element_type=F32))
        yn = y * lax.rsqrt(ss * (1.0 / HEAD_DIM) + EPS) * w
        up = pltpu.roll(yn, LANES - HEAD_DIM // 4, axis=1)
        dn = pltpu.roll(yn, HEAD_DIM // 4, axis=1)
        return yn * cos + up * sa + dn * sb

    qscale = (HEAD_DIM ** -0.5) * math.log2(math.e)
    for c in range(A_WIDTH // LANES):
        q2 = norm_rope(pa[:, c * LANES:(c + 1) * LANES], qw_ref[...]) * qscale
        q2t = q2.T.astype(BF16)
        qt_ref[2 * c, 0] = q2t[0:HEAD_DIM]
        qt_ref[2 * c + 1, 0] = q2t[HEAD_DIM:2 * HEAD_DIM]
    k2 = norm_rope(pa[:, A_WIDTH:A_WIDTH + KV_WIDTH], kw_ref[...]).astype(BF16)
    k_ref[0, 0] = k2[:, 0:HEAD_DIM]
    k_ref[1, 0] = k2[:, HEAD_DIM:2 * HEAD_DIM]
    v2t = pa[:, A_WIDTH + KV_WIDTH:A_WIDTH + 2 * KV_WIDTH].T.astype(BF16)
    vt_ref[0, 0] = v2t[0:HEAD_DIM]
    vt_ref[1, 0] = v2t[HEAD_DIM:2 * HEAD_DIM]

    pm = jnp.dot(h, wm_ref[...], preferred_element_type=F32)

    def heads_t(cols0, ref):
        for c in range(M_WIDTH // LANES):
            t2 = pm[:, cols0 + c * LANES:cols0 + (c + 1) * LANES].T.astype(BF16)
            ref[2 * c, 0] = t2[0:HEAD_DIM]
            ref[2 * c + 1, 0] = t2[HEAD_DIM:2 * HEAD_DIM]

    heads_t(0, mq_ref)
    heads_t(2 * M_WIDTH, mv_ref)
    heads_t(3 * M_WIDTH, mo_ref)
    for hd in range(M_HEADS):
        c0 = M_WIDTH + hd * HEAD_DIM
        mk_ref[hd, 0] = (pm[:, c0:c0 + HEAD_DIM] * (HEAD_DIM ** -0.5)).astype(BF16)


def _inproj(x2, T, n1, wm, wg, wa, bg, qw2, kw2, rrow, rcol, ones_bd, tri, tm):
    N = x2.shape[0]
    tps = T // tm
    row = lambda i: (i, 0)
    out_shape = (
        jax.ShapeDtypeStruct((M_HEADS, N // tm, HEAD_DIM, tm), BF16),
        jax.ShapeDtypeStruct((M_HEADS, N // tm, tm, HEAD_DIM), BF16),
        jax.ShapeDtypeStruct((M_HEADS, N // tm, HEAD_DIM, tm), BF16),
        jax.ShapeDtypeStruct((M_HEADS, N // tm, HEAD_DIM, tm), BF16),
        jax.ShapeDtypeStruct((N, LANES), F32), jax.ShapeDtypeStruct((N, LANES), F32),
        jax.ShapeDtypeStruct((N, LANES), F32), jax.ShapeDtypeStruct((N // MLSTM_CHUNK, SUBLANES, LANES), F32),
        jax.ShapeDtypeStruct((A_HEADS, N // tm, HEAD_DIM, tm), BF16),
        jax.ShapeDtypeStruct((A_KV_HEADS, N // tm, tm, HEAD_DIM), BF16),
        jax.ShapeDtypeStruct((A_KV_HEADS, N // tm, HEAD_DIM, tm), BF16),
    )
    return pl.pallas_call(
        _inproj_kernel,
        grid=(N // tm,),
        in_specs=[
            pl.BlockSpec((tm, D_MODEL), row),
            _const_spec(n1.shape), _const_spec(wm.shape), _const_spec(wg.shape), _const_spec(wa.shape),
            _const_spec(bg.shape), _const_spec(qw2.shape), _const_spec(kw2.shape),
            pl.BlockSpec((1,) + rrow.shape[1:], lambda i: (i % tps, 0, 0, 0)), _const_spec(rcol.shape),
            _const_spec(ones_bd.shape), _const_spec(tri.shape),
        ],
        out_specs=(
            pl.BlockSpec((M_HEADS, 1, HEAD_DIM, tm), lambda i: (0, i, 0, 0)),
            pl.BlockSpec((M_HEADS, 1, tm, HEAD_DIM), lambda i: (0, i, 0, 0)),
            pl.BlockSpec((M_HEADS, 1, HEAD_DIM, tm), lambda i: (0, i, 0, 0)),
            pl.BlockSpec((M_HEADS, 1, HEAD_DIM, tm), lambda i: (0, i, 0, 0)),
            pl.BlockSpec((tm, LANES), row), pl.BlockSpec((tm, LANES), row), pl.BlockSpec((tm, LANES), row),
            pl.BlockSpec((tm // MLSTM_CHUNK, SUBLANES, LANES), lambda i: (i, 0, 0)),
            pl.BlockSpec((A_HEADS, 1, HEAD_DIM, tm), lambda i: (0, i, 0, 0)),
            pl.BlockSpec((A_KV_HEADS, 1, tm, HEAD_DIM), lambda i: (0, i, 0, 0)),
            pl.BlockSpec((A_KV_HEADS, 1, HEAD_DIM, tm), lambda i: (0, i, 0, 0)),
        ),
        out_shape=out_shape,
        compiler_params=_cparams(("parallel",)),
        name="inproj",
    )(x2, n1, wm, wg, wa, bg, qw2, kw2, rrow, rcol, ones_bd, tri)


def _log_sigmoid(x):
    return jnp.minimum(x, 0.0) - jnp.log1p(jnp.exp(-jnp.abs(x)))


def _split3(x):
    a = x.astype(BF16)
    r = x - a.astype(F32)
    b = r.astype(BF16)
    c = (r - b.astype(F32)).astype(BF16)
    return a, b, c


def _gate_sums(g, tri):
    tm = g.shape[0]
    logf = _log_sigmoid(g) * math.log2(math.e)
    a, b, c = _split3(logf)
    pre = (jnp.dot(tri, a, preferred_element_type=F32) + jnp.dot(tri, b, preferred_element_type=F32)
           + jnp.dot(tri, c, preferred_element_type=F32))
    tot = jnp.concatenate([jnp.broadcast_to(pre[r + MLSTM_CHUNK - 1:r + MLSTM_CHUNK], (MLSTM_CHUNK, LANES))
                           for r in range(0, tm, MLSTM_CHUNK)], axis=0)
    fwd = _is_fwd_lane((tm, LANES))
    cum = jnp.where(fwd, pre, tot - pre + logf)
    e = pltpu.roll(g, 2 * M_HEADS, axis=1) * math.log2(math.e) - cum

    row = lax.broadcasted_iota(jnp.int32, (tm, LANES), 0) & (MLSTM_CHUNK - 1)
    up, down = e, e
    shift = 1
    while shift < MLSTM_CHUNK:
        up = jnp.maximum(up, jnp.where(row >= shift, pltpu.roll(up, shift, axis=0), NEG))
        down = jnp.maximum(down, jnp.where(row < MLSTM_CHUNK - shift, pltpu.roll(down, tm - shift, axis=0), NEG))
        shift *= 2
    ends = [jnp.concatenate([pre[r + MLSTM_CHUNK - 1:r + MLSTM_CHUNK],
                             jnp.where(fwd[0:1], up[r + MLSTM_CHUNK - 1:r + MLSTM_CHUNK], down[r:r + 1]),
                             jnp.zeros((SUBLANES - 2, LANES), F32)], axis=0) for r in range(0, tm, MLSTM_CHUNK)]
    return cum, e, jnp.where(fwd, up, down), jnp.stack(ends)


def _is_fwd_lane(shape):
    return lax.broadcasted_iota(jnp.int32, shape, len(shape) - 1) < 3 * M_HEADS


def _mlstm_state_kernel(kf_ref, vtf_ref, ef_ref, endf_ref, kb_ref, vtb_ref, eb_ref, endb_ref,
                        sf_ref, mf_ref, sb_ref, mb_ref, c_ref, m_ref):
    L = MLSTM_CHUNK
    per = ef_ref.shape[0] // L
    cpt = kf_ref.shape[2] // L

    @pl.when(pl.program_id(1) == 0)
    def _():
        c_ref[...] = jnp.zeros_like(c_ref)
        m_ref[...] = jnp.zeros_like(m_ref)

    fwd = _is_fwd_lane((1, LANES))
    ones = jnp.ones((MLSTM_VPAD, L), BF16)
    pairs = [(d, hd) for d in range(2) for hd in range(M_HEADS)]
    k_refs, vt_refs, s_refs = (kf_ref, kb_ref), (vtf_ref, vtb_ref), (sf_ref, sb_ref)
    lane = lambda d, hd: (2 + d) * M_HEADS + hd

    for j in range(per):
        jb = per - 1 - j
        rows = (slice(j * L, (j + 1) * L), slice(jb * L, (jb + 1) * L))
        tile = (j // cpt, jb // cpt)
        trow = tuple(slice((c % cpt) * L, (c % cpt + 1) * L) for c in (j, jb))
        tot_f = endf_ref[j, 0:1, :]
        tot_b = endb_ref[jb, 0:1, :]
        wend_f = tot_f + ef_ref[rows[0], :]
        wend_b = tot_b + eb_ref[rows[1], :]
        tot = jnp.where(fwd, tot_f, tot_b)
        m_prev = m_ref[...]
        m_new = jnp.maximum(tot + m_prev, tot + jnp.where(fwd, endf_ref[j, 1:2, :], endb_ref[jb, 1:2, :]))
        dec = jnp.exp2(tot + m_prev - m_new)
        m_ref[...] = m_new
        m_rows = jnp.broadcast_to(m_prev, (SUBLANES, LANES))
        mf_ref[j] = m_rows
        mb_ref[jb] = m_rows
        we_t = (jnp.exp2(wend_f - m_new).T, jnp.exp2(wend_b - m_new).T)

        wvs = {}
        for d, hd in pairs:
            vext = jnp.concatenate([vt_refs[d][hd, tile[d], :, trow[d]], ones], axis=0)
            wvs[d, hd] = (vext.astype(F32) * we_t[d][lane(d, hd):lane(d, hd) + 1, :]).astype(BF16)
        adds = {(d, hd): jnp.dot(wvs[d, hd], k_refs[d][hd, tile[d], trow[d], :], preferred_element_type=F32)
                for d, hd in pairs}
        for d, hd in pairs:
            cext = c_ref[d * M_HEADS + hd]
            s_refs[d][(j, jb)[d], hd] = cext
            c_ref[d * M_HEADS + hd] = dec[:, lane(d, hd):lane(d, hd) + 1] * cext + adds[d, hd]


def _mlstm_state(mk, mvt, e, ends, T):
    _, nblk, blk, _ = mk.shape
    N = nblk * blk
    L = MLSTM_CHUNK
    g = MLSTM_STATE_TILES
    per = g * blk // L
    assert (T // blk) % g == 0
    nt = T // (g * blk)
    B = N // T
    tf = lambda b, t: b * nt + t
    tb = lambda b, t: b * nt + (nt - 1 - t)

    def specs(ti):
        return [pl.BlockSpec((M_HEADS, g, blk, HEAD_DIM), lambda b, t: (0, ti(b, t), 0, 0)),
                pl.BlockSpec((M_HEADS, g, HEAD_DIM, blk), lambda b, t: (0, ti(b, t), 0, 0)),
                pl.BlockSpec((g * blk, LANES), lambda b, t: (ti(b, t), 0)),
                pl.BlockSpec((per, SUBLANES, LANES), lambda b, t: (ti(b, t), 0, 0))]

    def outs(ti):
        return [pl.BlockSpec((per, M_HEADS, HEAD_DIM + MLSTM_VPAD, HEAD_DIM), lambda b, t: (ti(b, t), 0, 0, 0)),
                pl.BlockSpec((per, SUBLANES, LANES), lambda b, t: (ti(b, t), 0, 0))]

    s_shape = jax.ShapeDtypeStruct((N // L, M_HEADS, HEAD_DIM + MLSTM_VPAD, HEAD_DIM), F32)
    m_shape = jax.ShapeDtypeStruct((N // L, SUBLANES, LANES), F32)
    return pl.pallas_call(
        _mlstm_state_kernel,
        grid=(B, nt),
        in_specs=specs(tf) + specs(tb),
        out_specs=tuple(outs(tf) + outs(tb)),
        out_shape=(s_shape, m_shape, s_shape, m_shape),
        scratch_shapes=[pltpu.VMEM((2 * M_HEADS, HEAD_DIM + MLSTM_VPAD, HEAD_DIM), F32), pltpu.VMEM((1, LANES), F32)],
        compiler_params=_cparams(("parallel", "arbitrary")),
        name="mlstm_state",
    )(mk, mvt, e, ends, mk, mvt, e, ends)


def _mlstm_out_kernel(qt_ref, k_ref, vt_ref, mot_ref, cum_ref, e_ref, emax_ref, sf_ref, mf_ref, sb_ref, mb_ref,
                      nw_ref, out_ref):
    L = MLSTM_CHUNK
    for j in range(e_ref.shape[0] // L):
        tok = pl.ds(j * L, L)
        _mlstm_out_chunk(qt_ref.at[:, :, :, tok], k_ref.at[:, :, tok, :], vt_ref.at[:, :, :, tok],
                         mot_ref.at[:, :, :, tok], cum_ref.at[tok], e_ref.at[tok], emax_ref.at[tok],
                         sf_ref.at[j:j + 1], mf_ref.at[j:j + 1], sb_ref.at[j:j + 1], mb_ref.at[j:j + 1],
                         nw_ref, out_ref.at[tok])


def _mlstm_out_chunk(qt_ref, k_ref, vt_ref, mot_ref, cum_ref, e_ref, emax_ref, sf_ref, mf_ref, sb_ref, mb_ref,
                     nw_ref, out_ref):
    L = e_ref.shape[0]
    e = e_ref[...]
    m_in = jnp.where(_is_fwd_lane((1, LANES)), mf_ref[0, 0:1, :], mb_ref[0, 0:1, :])
    cum_t = cum_ref[...].T
    mx_t = jnp.maximum(emax_ref[...], m_in).T

    si = lax.broadcasted_iota(jnp.int32, (L, L), 0)
    li = lax.broadcasted_iota(jnp.int32, (L, L), 1)
    keep = (si <= li, si >= li)
    ones = jnp.ones((MLSTM_VPAD, L), BF16)

    pairs = [(hd, d) for hd in range(M_HEADS) for d in range(2)]
    s_refs = (sf_ref, sb_ref)
    lane = lambda hd, d: (2 + d) * M_HEADS + hd
    qts = [qt_ref[hd, 0] for hd in range(M_HEADS)]
    raws = [jnp.dot(k_ref[hd, 0], qts[hd], preferred_element_type=F32) for hd in range(M_HEADS)]
    carried = {(hd, d): jnp.dot(s_refs[d][0, hd].astype(BF16), qts[hd], preferred_element_type=F32)
               for hd, d in pairs}
    mxs = {(hd, d): mx_t[lane(hd, d):lane(hd, d) + 1, :] for hd, d in pairs}
    ws = {(hd, d): jnp.where(keep[d], raws[hd] * jnp.exp2(e[:, lane(hd, d):lane(hd, d) + 1] - mxs[hd, d]),
                             0.0).astype(BF16) for hd, d in pairs}
    vexts = [jnp.concatenate([vt_ref[hd, 0], ones], axis=0) for hd in range(M_HEADS)]
    nds = {(hd, d): (jnp.exp2(m_in[:, lane(hd, d):lane(hd, d) + 1] - mxs[hd, d]) * carried[hd, d]
                     + jnp.dot(vexts[hd], ws[hd, d], preferred_element_type=F32)) for hd, d in pairs}
    hhs = {(hd, d): nds[hd, d][0:HEAD_DIM] / jnp.maximum(
        jnp.abs(nds[hd, d][HEAD_DIM:HEAD_DIM + 1]),
        jnp.exp2(-(cum_t[lane(hd, d):lane(hd, d) + 1, :] + mxs[hd, d]))) for hd, d in pairs}
    outs = []
    for hd in range(M_HEADS):
        hs = hhs[hd, 0] + hhs[hd, 1]
        hn = hs * lax.rsqrt(jnp.mean(hs * hs, axis=0, keepdims=True) + EPS) * nw_ref[hd]
        mo = mot_ref[hd, 0].astype(F32)
        outs.append(hn / (1.0 + jnp.exp(-mo)))
    for jj in range(M_HEADS // 2):
        o2 = jnp.concatenate([outs[2 * jj], outs[2 * jj + 1]], axis=0)
        out_ref[:, jj * LANES:(jj + 1) * LANES] = o2.T.astype(out_ref.dtype)


def _mlstm_out(mqt, mk, mvt, mot, cum, e, emax, sf, mf, sb, mb, nwb):
    _, nblk, _, blk = mqt.shape
    N = nblk * blk
    L = MLSTM_CHUNK
    per = blk // L
    tspec = pl.BlockSpec((M_HEADS, 1, HEAD_DIM, blk), lambda t: (0, t, 0, 0))
    kspec = pl.BlockSpec((M_HEADS, 1, blk, HEAD_DIM), lambda t: (0, t, 0, 0))
    sspec = pl.BlockSpec((per, M_HEADS, HEAD_DIM + MLSTM_VPAD, HEAD_DIM), lambda t: (t, 0, 0, 0))
    mspec = pl.BlockSpec((per, SUBLANES, LANES), lambda t: (t, 0, 0))
    gspec = pl.BlockSpec((blk, LANES), lambda t: (t, 0))
    return pl.pallas_call(
        _mlstm_out_kernel,
        grid=(nblk,),
        in_specs=[tspec, kspec, tspec, tspec, gspec, gspec, gspec, sspec, mspec, sspec, mspec,
                  _const_spec(nwb.shape)],
        out_specs=pl.BlockSpec((blk, M_WIDTH), lambda t: (t, 0)),
        out_shape=jax.ShapeDtypeStruct((N, M_WIDTH), BF16),
        compiler_params=_cparams(("parallel",)),
        name="mlstm_out",
    )(mqt, mk, mvt, mot, cum, e, emax, sf, mf, sb, mb, nwb)


def _attn_kernel(qt_ref, k_ref, vt_ref, o_ref, m_ref, acc_ref, s0_ref, s1_ref, x0_ref, x1_ref):
    nkb = k_ref.shape[1]
    tk = k_ref.shape[2]
    tq = qt_ref.shape[3]
    units = [(j, c) for j in range(A_GROUP) for c in range(0, tq, ATTN_QW)]
    ones = jnp.ones((ATTN_VPAD, tk), BF16)

    m_ref[...] = jnp.full_like(m_ref, -jnp.inf)
    acc_ref[...] = jnp.zeros_like(acc_ref)

    def stage(cur, nxt):
        if nxt is not None:
            k = k_ref[0, nxt[0]]
        if cur is not None:
            vt = jnp.concatenate([vt_ref[0, cur[0]], ones], axis=0)
        for u, (j, c) in enumerate(units):
            cols = slice(c, c + ATTN_QW)
            if nxt is not None:
                st = jnp.dot(k, qt_ref[j, 0, :, cols], preferred_element_type=F32)
                nxt[1][u] = st
                nxt[2][u] = jnp.max(st, axis=0, keepdims=True)
            if cur is not None:
                st = cur[1][u]
                m_old = m_ref[j, :, cols]
                m_new = jnp.maximum(m_old, cur[2][u])
                alpha = jnp.exp2(m_old - m_new)
                p = jnp.exp2(st - m_new)
                acc_ref[j, :, cols] = (alpha * acc_ref[j, :, cols]
                                       + jnp.dot(vt, p.astype(BF16), preferred_element_type=F32))
                m_ref[j, :, cols] = m_new

    stage(None, (0, s0_ref, x0_ref))

    def body(i, carry):
        kb = 2 * i
        stage((kb, s0_ref, x0_ref), (kb + 1, s1_ref, x1_ref))
        stage((kb + 1, s1_ref, x1_ref), (kb + 2, s0_ref, x0_ref))
        return carry

    lax.fori_loop(0, nkb // 2 - 1, body, 0)
    stage((nkb - 2, s0_ref, x0_ref), (nkb - 1, s1_ref, x1_ref))
    stage((nkb - 1, s1_ref, x1_ref), None)

    def head_out(j):
        return acc_ref[j, 0:HEAD_DIM, :] / acc_ref[j, HEAD_DIM:HEAD_DIM + 1, :]

    for jj in range(A_GROUP // 2):
        o2 = jnp.concatenate([head_out(2 * jj), head_out(2 * jj + 1)], axis=0)
        o_ref[:, jj * LANES:(jj + 1) * LANES] = o2.T.astype(o_ref.dtype)


def _attn(qt, k, vt, T):
    _, nblk, _, blk = qt.shape
    N = nblk * blk
    B = N // T
    nq = T // blk
    assert nq % 2 == 0, "attention pipelines key blocks in pairs"
    nu = A_GROUP * (blk // ATTN_QW)
    return pl.pallas_call(
        _attn_kernel,
        grid=(B, A_KV_HEADS, nq),
        in_specs=[
            pl.BlockSpec((A_GROUP, 1, HEAD_DIM, blk), lambda b, g, qi: (g, b * nq + qi, 0, 0)),
            pl.BlockSpec((1, nq, blk, HEAD_DIM), lambda b, g, qi: (g, b, 0, 0)),
            pl.BlockSpec((1, nq, HEAD_DIM, blk), lambda b, g, qi: (g, b, 0, 0)),
        ],
        out_specs=pl.BlockSpec((blk, A_GROUP * HEAD_DIM), lambda b, g, qi: (b * nq + qi, g)),
        out_shape=jax.ShapeDtypeStruct((N, A_WIDTH), BF16),
        scratch_shapes=[pltpu.VMEM((A_GROUP, 1, blk), F32),
                        pltpu.VMEM((A_GROUP, HEAD_DIM + ATTN_VPAD, blk), F32),
                        pltpu.VMEM((nu, blk, ATTN_QW), F32), pltpu.VMEM((nu, blk, ATTN_QW), F32),
                        pltpu.VMEM((nu, 1, ATTN_QW), F32), pltpu.VMEM((nu, 1, ATTN_QW), F32)],
        compiler_params=_cparams(("parallel", "parallel", "arbitrary")),
        name="attn",
    )(qt, k, vt)


def _ffn_kernel(xp_ref, xm_ref, xn_ref, mp_ref, mm_ref, mn_ref, ap_ref, am_ref, an_ref, wom_ref, woa_ref, n2_ref,
                wup_ref, cw_ref, cb_ref, wdn_ref, fw_ref, y_ref, lhs_ref, x1_ref, u0_ref, u1_ref, acc_ref,
                *, tiles_per_seq):
    i = pl.program_id(0)
    tm = xm_ref.shape[0]
    H = CONV_HALO
    t = i % tiles_per_seq
    xe = jnp.concatenate([xp_ref[...], xm_ref[...], xn_ref[...]], axis=0)
    me = jnp.concatenate([mp_ref[...], mm_ref[...], mn_ref[...]], axis=0)
    ae = jnp.concatenate([ap_ref[...], am_ref[...], an_ref[...]], axis=0)
    x1e = (xe + jnp.dot(me, wom_ref[...], preferred_element_type=F32)
           + jnp.dot(ae, woa_ref[...], preferred_element_type=F32))
    h2e = (x1e * lax.rsqrt(jnp.mean(x1e * x1e, axis=-1, keepdims=True) + EPS) * n2_ref[...]).astype(BF16)
    x1_ref[...] = x1e[H:H + tm]
    lhs_ref[0:H] = jnp.where(t == 0, jnp.zeros((H, D_MODEL), BF16), h2e[0:H])
    lhs_ref[H:H + tm] = h2e[H:H + tm]
    lhs_ref[H + tm:H + tm + H] = jnp.where(t == tiles_per_seq - 1, jnp.zeros((H, D_MODEL), BF16),
                                           h2e[H + tm:H + tm + H])
    acc_ref[...] = jnp.zeros_like(acc_ref)
    nchunk = D_FF // FF_CHUNK

    def cols(ref, c):
        off = c * FF_CHUNK if isinstance(c, int) else pl.multiple_of(c * FF_CHUNK, FF_CHUNK)
        return ref[:, pl.ds(off, FF_CHUNK)], ref[:, pl.ds(D_FF + off, FF_CHUNK)]

    def up(c, u_ref):
        lhs = lhs_ref[...]
        wa, wg = cols(wup_ref, c)
        u_ref[:, 0:FF_CHUNK] = jnp.dot(lhs, wa, preferred_element_type=F32)
        u_ref[:, FF_CHUNK:] = jnp.dot(lhs, wg, preferred_element_type=F32)

    def gate_down(c, u_ref):
        w = jnp.concatenate(cols(cw_ref, c), axis=1)
        b = jnp.concatenate(cols(cb_ref, c), axis=1)
        conv = (u_ref[pl.ds(H - 1, tm), :] * w[0:1] + u_ref[pl.ds(H, tm), :] * w[1:2]
                + u_ref[pl.ds(H + 1, tm), :] * w[2:3] + b)
        a = conv[:, 0:FF_CHUNK]
        gt = conv[:, FF_CHUNK:2 * FF_CHUNK]
        act = (gt / (1.0 + jnp.exp(-gt))) * a
        row0 = c * FF_CHUNK if isinstance(c, int) else pl.multiple_of(c * FF_CHUNK, FF_CHUNK)
        acc_ref[...] += jnp.dot(act.astype(BF16), wdn_ref[pl.ds(row0, FF_CHUNK), :], preferred_element_type=F32)

    assert nchunk % 2 == 1
    up(0, u0_ref)

    def body(i, carry):
        c = 2 * i
        up(c + 1, u1_ref)
        gate_down(c, u0_ref)
        up(c + 2, u0_ref)
        gate_down(c + 1, u1_ref)
        return carry

    lax.fori_loop(0, nchunk // 2, body, 0)
    gate_down(nchunk - 1, u0_ref)
    x2 = x1_ref[...] + acc_ref[...]
    y_ref[...] = x2 * lax.rsqrt(jnp.mean(x2 * x2, axis=-1, keepdims=True) + EPS) * fw_ref[...]


def _ffn(x2, m_out, a_out, T, wom, woa, n2, wup, cw, cb, wdn, fw, tm):
    N = x2.shape[0]
    H = CONV_HALO
    tps = T // tm
    r = tm // H
    nblk = N // H
    row = lambda i: (i, 0)
    prev = lambda i: (jnp.maximum(i * r - 1, 0), 0)
    nxt = lambda i: (jnp.minimum((i + 1) * r, nblk - 1), 0)

    def halo3(width):
        return [pl.BlockSpec((H, width), prev), pl.BlockSpec((tm, width), row), pl.BlockSpec((H, width), nxt)]

    return pl.pallas_call(
        functools.partial(_ffn_kernel, tiles_per_seq=tps),
        grid=(N // tm,),
        in_specs=halo3(D_MODEL) + halo3(M_WIDTH) + halo3(A_WIDTH) + [
            _const_spec(wom.shape), _const_spec(woa.shape), _const_spec(n2.shape),
            _const_spec(wup.shape), _const_spec(cw.shape), _const_spec(cb.shape), _const_spec(wdn.shape),
            _const_spec(fw.shape),
        ],
        out_specs=pl.BlockSpec((tm, D_MODEL), row),
        out_shape=jax.ShapeDtypeStruct((N, D_MODEL), F32),
        scratch_shapes=[pltpu.VMEM((tm + 2 * H, D_MODEL), BF16), pltpu.VMEM((tm, D_MODEL), F32),
                        pltpu.VMEM((tm + 2 * H, 2 * FF_CHUNK), F32), pltpu.VMEM((tm + 2 * H, 2 * FF_CHUNK), F32),
                        pltpu.VMEM((tm, D_MODEL), F32)],
        compiler_params=_cparams(("parallel",)),
        name="ffn",
    )(x2, x2, x2, m_out, m_out, m_out, a_out, a_out, a_out, wom, woa, n2, wup, cw, cb, wdn, fw)


def _rope_tables(T, tm):
    nf = HEAD_DIM // 4
    inv = ROPE_THETA ** (-jnp.arange(nf, dtype=F32) / nf)
    zero = jnp.zeros((1, 2 * nf), F32)

    def parts(pos, row_part):
        ang = pos[:, None] * inv
        cos, sin = jnp.cos(ang), jnp.sin(ang)
        z = jnp.broadcast_to(zero, (pos.shape[0], 2 * nf))
        halves = lambda a, b: jnp.concatenate([a, b, z] if row_part else [z, a, b], axis=-1)
        tabs = (halves(cos, cos), halves(-sin, 0.0 * sin), halves(0.0 * sin, sin))
        return jnp.stack([jnp.concatenate([t, t], axis=-1) for t in tabs])

    rrow = parts(jnp.arange(T // GRID_W, dtype=F32), True)
    rrow = rrow.reshape(3, T // tm, tm // GRID_W, LANES).transpose(1, 0, 2, 3)
    rcol = parts(jnp.arange(GRID_W, dtype=F32), False)
    return rrow, jnp.tile(rcol, (1, tm // GRID_W, 1))


def _prep_weights(w_in, b_gates, mh_norm_w, q_norm_w, k_norm_w, w_out, norm1_w, norm2_w, w_up, conv_w, conv_b,
                  w_down, final_norm_w):
    gate0 = 4 * M_WIDTH
    wm = w_in[:, :gate0].astype(BF16)
    wg = jnp.pad(w_in[:, gate0:gate0 + N_GATES], ((0, 0), (0, LANES - N_GATES))).astype(BF16)
    wa = w_in[:, gate0 + N_GATES:].astype(BF16)
    bg = jnp.pad(b_gates, (0, LANES - N_GATES)).reshape(1, LANES)
    return dict(
        wm=wm, wg=wg, wa=wa, bg=bg,
        n1=norm1_w.reshape(1, D_MODEL), n2=norm2_w.reshape(1, D_MODEL), fw=final_norm_w.reshape(1, D_MODEL),
        qw2=jnp.tile(q_norm_w, 2).reshape(1, LANES), kw2=jnp.tile(k_norm_w, 2).reshape(1, LANES),
        nw=mh_norm_w.reshape(1, M_WIDTH),
        wom=w_out[:M_WIDTH].astype(BF16), woa=w_out[M_WIDTH:].astype(BF16),
        wup=w_up.astype(BF16), cw=conv_w, cb=conv_b.reshape(1, 2 * D_FF), wdn=w_down.astype(BF16),
    )


def _tiles(T):
    return min(512, T // 2)


def _trunk(x, p):
    B, T, _ = x.shape
    N = B * T
    tm = _tiles(T)
    x2 = x.reshape(N, D_MODEL)
    rrow, rcol = _rope_tables(T, tm)
    blk = np.arange(LANES) // HEAD_DIM
    ones_bd = jnp.asarray(blk[:, None] == blk[None, :], BF16)
    idx = np.arange(tm)
    tri = jnp.asarray((idx[None, :] <= idx[:, None])
                      & (idx[None, :] // MLSTM_CHUNK == idx[:, None] // MLSTM_CHUNK), BF16)

    mq, mk, mv, mo, cum, e, emax, ends, qt, k, vt = _inproj(
        x2, T, p["n1"], p["wm"], p["wg"], p["wa"], p["bg"], p["qw2"], p["kw2"], rrow, rcol, ones_bd, tri, tm)
    nwb = jnp.broadcast_to(p["nw"].reshape(M_HEADS, HEAD_DIM, 1), (M_HEADS, HEAD_DIM, MLSTM_CHUNK))
    s_f, m_f, s_b, m_b = _mlstm_state(mk, mv, e, ends, T)
    m_out = _mlstm_out(mq, mk, mv, mo, cum, e, emax, s_f, m_f, s_b, m_b, nwb)
    a_out = _attn(qt, k, vt, T)
    y = _ffn(x2, m_out, a_out, T, p["wom"], p["woa"], p["n2"], p["wup"], p["cw"], p["cb"], p["wdn"], p["fw"], tm)
    return y.reshape(B, T, D_MODEL)


def kernel(x_prompt, x_sample, w_in, b_gates, mh_norm_w, q_norm_w, k_norm_w, w_out, norm1_w, norm2_w, w_up,
           conv_w, conv_b, w_down, final_norm_w):
    assert w_in.shape[0] == 1, "single-layer trunk"
    p = _prep_weights(w_in[0], b_gates[0], mh_norm_w[0], q_norm_w[0], k_norm_w[0], w_out[0], norm1_w[0],
                      norm2_w[0], w_up[0], conv_w[0], conv_b[0], w_down[0], final_norm_w)
    return (_trunk(x_prompt, p), _trunk(x_sample, p))
```

```python
import functools
import math

import jax
import jax.numpy as jnp
import numpy as np
from jax import lax
from jax.experimental import pallas as pl
from jax.experimental.pallas import tpu as pltpu

F32 = jnp.float32
BF16 = jnp.bfloat16

D_MODEL = 1024
HEAD_DIM = 64
M_HEADS = 8
M_WIDTH = M_HEADS * HEAD_DIM
A_HEADS = 8
A_KV_HEADS = 2
A_GROUP = A_HEADS // A_KV_HEADS
A_WIDTH = A_HEADS * HEAD_DIM
KV_WIDTH = A_KV_HEADS * HEAD_DIM
N_GATES = 4 * M_HEADS
D_FF = 2816
GRID_W = 64
ROPE_THETA = 10000.0
EPS = 1e-6

LANES = 128
SUBLANES = 8
VMEM_LIMIT = 56 * 1024 * 1024

MLSTM_CHUNK = 256
MLSTM_VPAD = 16
MLSTM_STATE_TILES = 2
FF_CHUNK = 256
CONV_HALO = 16
NEG = -1e30
ATTN_QW = 256
ATTN_VPAD = 16


def _cparams(sem):
    return pltpu.CompilerParams(dimension_semantics=sem, vmem_limit_bytes=VMEM_LIMIT)


def _const_spec(shape):
    nd = len(shape)
    return pl.BlockSpec(shape, lambda *_: (0,) * nd, pipeline_mode=pl.Buffered(1))


def _inproj_kernel(x_ref, n1_ref, wm_ref, wg_ref, wa_ref, bg_ref, qw_ref, kw_ref, rrow_ref, rcol_ref,
                   ones_ref, tri_ref, mq_ref, mk_ref, mv_ref, mo_ref, cum_ref, e_ref, emax_ref, ends_ref,
                   qt_ref, k_ref, vt_ref):
    x = x_ref[...]
    h = x * lax.rsqrt(jnp.mean(x * x, axis=-1, keepdims=True) + EPS) * n1_ref[...]
    h = h.astype(BF16)

    g = jnp.dot(h, wg_ref[...], preferred_element_type=F32) + bg_ref[...]
    cum_ref[...], e_ref[...], emax_ref[...], ends_ref[...] = _gate_sums(g, tri_ref[...])

    pa = jnp.dot(h, wa_ref[...], preferred_element_type=F32)
    tm = x.shape[0]
    rrow = rrow_ref[0]
    cos, sa, sb = (jnp.broadcast_to(rrow[i][:, None, :], (tm // GRID_W, GRID_W, LANES)).reshape(tm, LANES)
                   + rcol_ref[i] for i in range(3))
    ones_bd = ones_ref[...]

    def norm_rope(y, w):
        sq = y * y
        hi = sq.astype(BF16)
        lo = (sq - hi.astype(F32)).astype(BF16)
        ss = (jnp.dot(hi, ones_bd, preferred_element_type=F32)
              + jnp.dot(lo, ones_bd, preferred_element_type=F32))
        yn = y * lax.rsqrt(ss * (1.0 / HEAD_DIM) + EPS) * w
        up = pltpu.roll(yn, LANES - HEAD_DIM // 4, axis=1)
        dn = pltpu.roll(yn, HEAD_DIM // 4, axis=1)
        return yn * cos + up * sa + dn * sb

    qscale = (HEAD_DIM ** -0.5) * math.log2(math.e)
    for c in range(A_WIDTH // LANES):
        q2 = norm_rope(pa[:, c * LANES:(c + 1) * LANES], qw_ref[...]) * qscale
        q2t = q2.T.astype(BF16)
        qt_ref[2 * c, 0] = q2t[0:HEAD_DIM]
        qt_ref[2 * c + 1, 0] = q2t[HEAD_DIM:2 * HEAD_DIM]
    k2 = norm_rope(pa[:, A_WIDTH:A_WIDTH + KV_WIDTH], kw_ref[...]).astype(BF16)
    k_ref[0, 0] = k2[:, 0:HEAD_DIM]
    k_ref[1, 0] = k2[:, HEAD_DIM:2 * HEAD_DIM]
    v2t = pa[:, A_WIDTH + KV_WIDTH:A_WIDTH + 2 * KV_WIDTH].T.astype(BF16)
    vt_ref[0, 0] = v2t[0:HEAD_DIM]
    vt_ref[1, 0] = v2t[HEAD_DIM:2 * HEAD_DIM]

    pm = jnp.dot(h, wm_ref[...], preferred_element_type=F32)

    def heads_t(cols0, ref):
        for c in range(M_WIDTH // LANES):
            t2 = pm[:, cols0 + c * LANES:cols0 + (c + 1) * LANES].T.astype(BF16)
            ref[2 * c, 0] = t2[0:HEAD_DIM]
            ref[2 * c + 1, 0] = t2[HEAD_DIM:2 * HEAD_DIM]

    heads_t(0, mq_ref)
    heads_t(2 * M_WIDTH, mv_ref)
    heads_t(3 * M_WIDTH, mo_ref)
    for hd in range(M_HEADS):
        c0 = M_WIDTH + hd * HEAD_DIM
        mk_ref[hd, 0] = (pm[:, c0:c0 + HEAD_DIM] * (HEAD_DIM ** -0.5)).astype(BF16)


def _inproj(x2, T, n1, wm, wg, wa, bg, qw2, kw2, rrow, rcol, ones_bd, tri, tm):
    N = x2.shape[0]
    tps = T // tm
    row = lambda i: (i, 0)
    out_shape = (
        jax.ShapeDtypeStruct((M_HEADS, N // tm, HEAD_DIM, tm), BF16),
        jax.ShapeDtypeStruct((M_HEADS, N // tm, tm, HEAD_DIM), BF16),
        jax.ShapeDtypeStruct((M_HEADS, N // tm, HEAD_DIM, tm), BF16),
        jax.ShapeDtypeStruct((M_HEADS, N // tm, HEAD_DIM, tm), BF16),
        jax.ShapeDtypeStruct((N, LANES), F32), jax.ShapeDtypeStruct((N, LANES), F32),
        jax.ShapeDtypeStruct((N, LANES), F32), jax.ShapeDtypeStruct((N // MLSTM_CHUNK, SUBLANES, LANES), F32),
        jax.ShapeDtypeStruct((A_HEADS, N // tm, HEAD_DIM, tm), BF16),
        jax.ShapeDtypeStruct((A_KV_HEADS, N // tm, tm, HEAD_DIM), BF16),
        jax.ShapeDtypeStruct((A_KV_HEADS, N // tm, HEAD_DIM, tm), BF16),
    )
    return pl.pallas_call(
        _inproj_kernel,
        grid=(N // tm,),
        in_specs=[
            pl.BlockSpec((tm, D_MODEL), row),
            _const_spec(n1.shape), _const_spec(wm.shape), _const_spec(wg.shape), _const_spec(wa.shape),
            _const_spec(bg.shape), _const_spec(qw2.shape), _const_spec(kw2.shape),
            pl.BlockSpec((1,) + rrow.shape[1:], lambda i: (i % tps, 0, 0, 0)), _const_spec(rcol.shape),
            _const_spec(ones_bd.shape), _const_spec(tri.shape),
        ],
        out_specs=(
            pl.BlockSpec((M_HEADS, 1, HEAD_DIM, tm), lambda i: (0, i, 0, 0)),
            pl.BlockSpec((M_HEADS, 1, tm, HEAD_DIM), lambda i: (0, i, 0, 0)),
            pl.BlockSpec((M_HEADS, 1, HEAD_DIM, tm), lambda i: (0, i, 0, 0)),
            pl.BlockSpec((M_HEADS, 1, HEAD_DIM, tm), lambda i: (0, i, 0, 0)),
            pl.BlockSpec((tm, LANES), row), pl.BlockSpec((tm, LANES), row), pl.BlockSpec((tm, LANES), row),
            pl.BlockSpec((tm // MLSTM_CHUNK, SUBLANES, LANES), lambda i: (i, 0, 0)),
            pl.BlockSpec((A_HEADS, 1, HEAD_DIM, tm), lambda i: (0, i, 0, 0)),
            pl.BlockSpec((A_KV_HEADS, 1, tm, HEAD_DIM), lambda i: (0, i, 0, 0)),
            pl.BlockSpec((A_KV_HEADS, 1, HEAD_DIM, tm), lambda i: (0, i, 0, 0)),
        ),
        out_shape=out_shape,
        compiler_params=_cparams(("parallel",)),
        name="inproj",
    )(x2, n1, wm, wg, wa, bg, qw2, kw2, rrow, rcol, ones_bd, tri)


def _log_sigmoid(x):
    return jnp.minimum(x, 0.0) - jnp.log1p(jnp.exp(-jnp.abs(x)))


def _split3(x):
    a = x.astype(BF16)
    r = x - a.astype(F32)
    b = r.astype(BF16)
    c = (r - b.astype(F32)).astype(BF16)
    return a, b, c


def _gate_sums(g, tri):
    tm = g.shape[0]
    logf = _log_sigmoid(g) * math.log2(math.e)
    a, b, c = _split3(logf)
    pre = (jnp.dot(tri, a, preferred_element_type=F32) + jnp.dot(tri, b, preferred_element_type=F32)
           + jnp.dot(tri, c, preferred_element_type=F32))
    tot = jnp.concatenate([jnp.broadcast_to(pre[r + MLSTM_CHUNK - 1:r + MLSTM_CHUNK], (MLSTM_CHUNK, LANES))
                           for r in range(0, tm, MLSTM_CHUNK)], axis=0)
    fwd = _is_fwd_lane((tm, LANES))
    cum = jnp.where(fwd, pre, tot - pre + logf)
    e = pltpu.roll(g, 2 * M_HEADS, axis=1) * math.log2(math.e) - cum

    row = lax.broadcasted_iota(jnp.int32, (tm, LANES), 0) & (MLSTM_CHUNK - 1)
    up, down = e, e
    shift = 1
    while shift < MLSTM_CHUNK:
        up = jnp.maximum(up, jnp.where(row >= shift, pltpu.roll(up, shift, axis=0), NEG))
        down = jnp.maximum(down, jnp.where(row < MLSTM_CHUNK - shift, pltpu.roll(down, tm - shift, axis=0), NEG))
        shift *= 2
    ends = [jnp.concatenate([pre[r + MLSTM_CHUNK - 1:r + MLSTM_CHUNK],
                             jnp.where(fwd[0:1], up[r + MLSTM_CHUNK - 1:r + MLSTM_CHUNK], down[r:r + 1]),
                             jnp.zeros((SUBLANES - 2, LANES), F32)], axis=0) for r in range(0, tm, MLSTM_CHUNK)]
    return cum, e, jnp.where(fwd, up, down), jnp.stack(ends)


def _is_fwd_lane(shape):
    return lax.broadcasted_iota(jnp.int32, shape, len(shape) - 1) < 3 * M_HEADS


def _mlstm_state_kernel(kf_ref, vtf_ref, ef_ref, endf_ref, kb_ref, vtb_ref, eb_ref, endb_ref,
                        sf_ref, mf_ref, sb_ref, mb_ref, c_ref, m_ref):
    L = MLSTM_CHUNK
    per = ef_ref.shape[0] // L
    cpt = kf_ref.shape[2] // L

    @pl.when(pl.program_id(1) == 0)
    def _():
        c_ref[...] = jnp.zeros_like(c_ref)
        m_ref[...] = jnp.zeros_like(m_ref)

    fwd = _is_fwd_lane((1, LANES))
    ones = jnp.ones((MLSTM_VPAD, L), BF16)
    pairs = [(d, hd) for d in range(2) for hd in range(M_HEADS)]
    k_refs, vt_refs, s_refs = (kf_ref, kb_ref), (vtf_ref, vtb_ref), (sf_ref, sb_ref)
    lane = lambda d, hd: (2 + d) * M_HEADS + hd

    for j in range(per):
        jb = per - 1 - j
        rows = (slice(j * L, (j + 1) * L), slice(jb * L, (jb + 1) * L))
        tile = (j // cpt, jb // cpt)
        trow = tuple(slice((c % cpt) * L, (c % cpt + 1) * L) for c in (j, jb))
        tot_f = endf_ref[j, 0:1, :]
        tot_b = endb_ref[jb, 0:1, :]
        wend_f = tot_f + ef_ref[rows[0], :]
        wend_b = tot_b + eb_ref[rows[1], :]
        tot = jnp.where(fwd, tot_f, tot_b)
        m_prev = m_ref[...]
        m_new = jnp.maximum(tot + m_prev, tot + jnp.where(fwd, endf_ref[j, 1:2, :], endb_ref[jb, 1:2, :]))
        dec = jnp.exp2(tot + m_prev - m_new)
        m_ref[...] = m_new
        m_rows = jnp.broadcast_to(m_prev, (SUBLANES, LANES))
        mf_ref[j] = m_rows
        mb_ref[jb] = m_rows
        we_t = (jnp.exp2(wend_f - m_new).T, jnp.exp2(wend_b - m_new).T)

        wvs = {}
        for d, hd in pairs:
            vext = jnp.concatenate([vt_refs[d][hd, tile[d], :, trow[d]], ones], axis=0)
            wvs[d, hd] = (vext.astype(F32) * we_t[d][lane(d, hd):lane(d, hd) + 1, :]).astype(BF16)
        adds = {(d, hd): jnp.dot(wvs[d, hd], k_refs[d][hd, tile[d], trow[d], :], preferred_element_type=F32)
                for d, hd in pairs}
        for d, hd in pairs:
            cext = c_ref[d * M_HEADS + hd]
            s_refs[d][(j, jb)[d], hd] = cext
            c_ref[d * M_HEADS + hd] = dec[:, lane(d, hd):lane(d, hd) + 1] * cext + adds[d, hd]


def _mlstm_state(mk, mvt, e, ends, T):
    _, nblk, blk, _ = mk.shape
    N = nblk * blk
    L = MLSTM_CHUNK
    g = MLSTM_STATE_TILES
    per = g * blk // L
    assert (T // blk) % g == 0
    nt = T // (g * blk)
    B = N // T
    tf = lambda b, t: b * nt + t
    tb = lambda b, t: b * nt + (nt - 1 - t)

    def specs(ti):
        return [pl.BlockSpec((M_HEADS, g, blk, HEAD_DIM), lambda b, t: (0, ti(b, t), 0, 0)),
                pl.BlockSpec((M_HEADS, g, HEAD_DIM, blk), lambda b, t: (0, ti(b, t), 0, 0)),
                pl.BlockSpec((g * blk, LANES), lambda b, t: (ti(b, t), 0)),
                pl.BlockSpec((per, SUBLANES, LANES), lambda b, t: (ti(b, t), 0, 0))]

    def outs(ti):
        return [pl.BlockSpec((per, M_HEADS, HEAD_DIM + MLSTM_VPAD, HEAD_DIM), lambda b, t: (ti(b, t), 0, 0, 0)),
                pl.BlockSpec((per, SUBLANES, LANES), lambda b, t: (ti(b, t), 0, 0))]

    s_shape = jax.ShapeDtypeStruct((N // L, M_HEADS, HEAD_DIM + MLSTM_VPAD, HEAD_DIM), F32)
    m_shape = jax.ShapeDtypeStruct((N // L, SUBLANES, LANES), F32)
    return pl.pallas_call(
        _mlstm_state_kernel,
        grid=(B, nt),
        in_specs=specs(tf) + specs(tb),
        out_specs=tuple(outs(tf) + outs(tb)),
        out_shape=(s_shape, m_shape, s_shape, m_shape),
        scratch_shapes=[pltpu.VMEM((2 * M_HEADS, HEAD_DIM + MLSTM_VPAD, HEAD_DIM), F32), pltpu.VMEM((1, LANES), F32)],
        compiler_params=_cparams(("parallel", "arbitrary")),
        name="mlstm_state",
    )(mk, mvt, e, ends, mk, mvt, e, ends)


def _mlstm_out_kernel(qt_ref, k_ref, vt_ref, mot_ref, cum_ref, e_ref, emax_ref, sf_ref, mf_ref, sb_ref, mb_ref,
                      nw_ref, out_ref):
    L = MLSTM_CHUNK
    for j in range(e_ref.shape[0] // L):
        tok = pl.ds(j * L, L)
        _mlstm_out_chunk(qt_ref.at[:, :, :, tok], k_ref.at[:, :, tok, :], vt_ref.at[:, :, :, tok],
                         mot_ref.at[:, :, :, tok], cum_ref.at[tok], e_ref.at[tok], emax_ref.at[tok],
                         sf_ref.at[j:j + 1], mf_ref.at[j:j + 1], sb_ref.at[j:j + 1], mb_ref.at[j:j + 1],
                         nw_ref, out_ref.at[tok])


def _mlstm_out_chunk(qt_ref, k_ref, vt_ref, mot_ref, cum_ref, e_ref, emax_ref, sf_ref, mf_ref, sb_ref, mb_ref,
                     nw_ref, out_ref):
    L = e_ref.shape[0]
    e = e_ref[...]
    m_in = jnp.where(_is_fwd_lane((1, LANES)), mf_ref[0, 0:1, :], mb_ref[0, 0:1, :])
    cum_t = cum_ref[...].T
    mx_t = jnp.maximum(emax_ref[...], m_in).T

    si = lax.broadcasted_iota(jnp.int32, (L, L), 0)
    li = lax.broadcasted_iota(jnp.int32, (L, L), 1)
    keep = (si <= li, si >= li)
    ones = jnp.ones((MLSTM_VPAD, L), BF16)

    pairs = [(hd, d) for hd in range(M_HEADS) for d in range(2)]
    s_refs = (sf_ref, sb_ref)
    lane = lambda hd, d: (2 + d) * M_HEADS + hd
    qts = [qt_ref[hd, 0] for hd in range(M_HEADS)]
    raws = [jnp.dot(k_ref[hd, 0], qts[hd], preferred_element_type=F32) for hd in range(M_HEADS)]
    carried = {(hd, d): jnp.dot(s_refs[d][0, hd].astype(BF16), qts[hd], preferred_element_type=F32)
               for hd, d in pairs}
    mxs = {(hd, d): mx_t[lane(hd, d):lane(hd, d) + 1, :] for hd, d in pairs}
    ws = {(hd, d): jnp.where(keep[d], raws[hd] * jnp.exp2(e[:, lane(hd, d):lane(hd, d) + 1] - mxs[hd, d]),
                             0.0).astype(BF16) for hd, d in pairs}
    vexts = [jnp.concatenate([vt_ref[hd, 0], ones], axis=0) for hd in range(M_HEADS)]
    nds = {(hd, d): (jnp.exp2(m_in[:, lane(hd, d):lane(hd, d) + 1] - mxs[hd, d]) * carried[hd, d]
                     + jnp.dot(vexts[hd], ws[hd, d], preferred_element_type=F32)) for hd, d in pairs}
    hhs = {(hd, d): nds[hd, d][0:HEAD_DIM] / jnp.maximum(
        jnp.abs(nds[hd, d][HEAD_DIM:HEAD_DIM + 1]),
        jnp.exp2(-(cum_t[lane(hd, d):lane(hd, d) + 1, :] + mxs[hd, d]))) for hd, d in pairs}
    outs = []
    for hd in range(M_HEADS):
        hs = hhs[hd, 0] + hhs[hd, 1]
        hn = hs * lax.rsqrt(jnp.mean(hs * hs, axis=0, keepdims=True) + EPS) * nw_ref[hd]
        mo = mot_ref[hd, 0].astype(F32)
        outs.append(hn / (1.0 + jnp.exp(-mo)))
    for jj in range(M_HEADS // 2):
        o2 = jnp.concatenate([outs[2 * jj], outs[2 * jj + 1]], axis=0)
        out_ref[:, jj * LANES:(jj + 1) * LANES] = o2.T.astype(out_ref.dtype)


def _mlstm_out(mqt, mk, mvt, mot, cum, e, emax, sf, mf, sb, mb, nwb):
    _, nblk, _, blk = mqt.shape
    N = nblk * blk
    L = MLSTM_CHUNK
    per = blk // L
    tspec = pl.BlockSpec((M_HEADS, 1, HEAD_DIM, blk), lambda t: (0, t, 0, 0))
    kspec = pl.BlockSpec((M_HEADS, 1, blk, HEAD_DIM), lambda t: (0, t, 0, 0))
    sspec = pl.BlockSpec((per, M_HEADS, HEAD_DIM + MLSTM_VPAD, HEAD_DIM), lambda t: (t, 0, 0, 0))
    mspec = pl.BlockSpec((per, SUBLANES, LANES), lambda t: (t, 0, 0))
    gspec = pl.BlockSpec((blk, LANES), lambda t: (t, 0))
    return pl.pallas_call(
        _mlstm_out_kernel,
        grid=(nblk,),
        in_specs=[tspec, kspec, tspec, tspec, gspec, gspec, gspec, sspec, mspec, sspec, mspec,
                  _const_spec(nwb.shape)],
        out_specs=pl.BlockSpec((blk, M_WIDTH), lambda t: (t, 0)),
        out_shape=jax.ShapeDtypeStruct((N, M_WIDTH), BF16),
        compiler_params=_cparams(("parallel",)),
        name="mlstm_out",
    )(mqt, mk, mvt, mot, cum, e, emax, sf, mf, sb, mb, nwb)


def _attn_kernel(qt_ref, k_ref, vt_ref, o_ref, m_ref, acc_ref, s0_ref, s1_ref, x0_ref, x1_ref):
    nkb = k_ref.shape[1]
    tk = k_ref.shape[2]
    tq = qt_ref.shape[3]
    units = [(j, c) for j in range(A_GROUP) for c in range(0, tq, ATTN_QW)]
    ones = jnp.ones((ATTN_VPAD, tk), BF16)

    m_ref[...] = jnp.full_like(m_ref, -jnp.inf)
    acc_ref[...] = jnp.zeros_like(acc_ref)

    def stage(cur, nxt):
        if nxt is not None:
            k = k_ref[0, nxt[0]]
        if cur is not None:
            vt = jnp.concatenate([vt_ref[0, cur[0]], ones], axis=0)
        for u, (j, c) in enumerate(units):
            cols = slice(c, c + ATTN_QW)
            if nxt is not None:
                st = jnp.dot(k, qt_ref[j, 0, :, cols], preferred_element_type=F32)
                nxt[1][u] = st
                nxt[2][u] = jnp.max(st, axis=0, keepdims=True)
            if cur is not None:
                st = cur[1][u]
                m_old = m_ref[j, :, cols]
                m_new = jnp.maximum(m_old, cur[2][u])
                alpha = jnp.exp2(m_old - m_new)
                p = jnp.exp2(st - m_new)
                acc_ref[j, :, cols] = (alpha * acc_ref[j, :, cols]
                                       + jnp.dot(vt, p.astype(BF16), preferred_element_type=F32))
                m_ref[j, :, cols] = m_new

    stage(None, (0, s0_ref, x0_ref))

    def body(i, carry):
        kb = 2 * i
        stage((kb, s0_ref, x0_ref), (kb + 1, s1_ref, x1_ref))
        stage((kb + 1, s1_ref, x1_ref), (kb + 2, s0_ref, x0_ref))
        return carry

    lax.fori_loop(0, nkb // 2 - 1 + jnp.minimum(pl.program_id(2), 0), body, 0)
    stage((nkb - 2, s0_ref, x0_ref), (nkb - 1, s1_ref, x1_ref))
    stage((nkb - 1, s1_ref, x1_ref), None)

    def head_out(j):
        return acc_ref[j, 0:HEAD_DIM, :] / acc_ref[j, HEAD_DIM:HEAD_DIM + 1, :]

    for jj in range(A_GROUP // 2):
        o2 = jnp.concatenate([head_out(2 * jj), head_out(2 * jj + 1)], axis=0)
        o_ref[:, jj * LANES:(jj + 1) * LANES] = o2.T.astype(o_ref.dtype)


def _attn(qt, k, vt, T):
    _, nblk, _, blk = qt.shape
    N = nblk * blk
    B = N // T
    nq = T // blk
    assert nq % 2 == 0, "attention pipelines key blocks in pairs"
    nu = A_GROUP * (blk // ATTN_QW)
    return pl.pallas_call(
        _attn_kernel,
        grid=(B, A_KV_HEADS, nq),
        in_specs=[
            pl.BlockSpec((A_GROUP, 1, HEAD_DIM, blk), lambda b, g, qi: (g, b * nq + qi, 0, 0)),
            pl.BlockSpec((1, nq, blk, HEAD_DIM), lambda b, g, qi: (g, b, 0, 0)),
            pl.BlockSpec((1, nq, HEAD_DIM, blk), lambda b, g, qi: (g, b, 0, 0)),
        ],
        out_specs=pl.BlockSpec((blk, A_GROUP * HEAD_DIM), lambda b, g, qi: (b * nq + qi, g)),
        out_shape=jax.ShapeDtypeStruct((N, A_WIDTH), BF16),
        scratch_shapes=[pltpu.VMEM((A_GROUP, 1, blk), F32),
                        pltpu.VMEM((A_GROUP, HEAD_DIM + ATTN_VPAD, blk), F32),
                        pltpu.VMEM((nu, blk, ATTN_QW), F32), pltpu.VMEM((nu, blk, ATTN_QW), F32),
                        pltpu.VMEM((nu, 1, ATTN_QW), F32), pltpu.VMEM((nu, 1, ATTN_QW), F32)],
        compiler_params=_cparams(("parallel", "parallel", "arbitrary")),
        name="attn",
    )(qt, k, vt)


def _ffn_kernel(xp_ref, xm_ref, xn_ref, mp_ref, mm_ref, mn_ref, ap_ref, am_ref, an_ref, wom_ref, woa_ref, n2_ref,
                wup_ref, cw_ref, cb_ref, wdn_ref, fw_ref, y_ref, lhs_ref, x1_ref, u0_ref, u1_ref, acc_ref,
                *, tiles_per_seq):
    i = pl.program_id(0)
    tm = xm_ref.shape[0]
    H = CONV_HALO
    t = i % tiles_per_seq
    xe = jnp.concatenate([xp_ref[...], xm_ref[...], xn_ref[...]], axis=0)
    me = jnp.concatenate([mp_ref[...], mm_ref[...], mn_ref[...]], axis=0)
    ae = jnp.concatenate([ap_ref[...], am_ref[...], an_ref[...]], axis=0)
    x1e = (xe + jnp.dot(me, wom_ref[...], preferred_element_type=F32)
           + jnp.dot(ae, woa_ref[...], preferred_element_type=F32))
    h2e = (x1e * lax.rsqrt(jnp.mean(x1e * x1e, axis=-1, keepdims=True) + EPS) * n2_ref[...]).astype(BF16)
    x1_ref[...] = x1e[H:H + tm]
    lhs_ref[0:H] = jnp.where(t == 0, jnp.zeros((H, D_MODEL), BF16), h2e[0:H])
    lhs_ref[H:H + tm] = h2e[H:H + tm]
    lhs_ref[H + tm:H + tm + H] = jnp.where(t == tiles_per_seq - 1, jnp.zeros((H, D_MODEL), BF16),
                                           h2e[H + tm:H + tm + H])
    acc_ref[...] = jnp.zeros_like(acc_ref)
    nchunk = D_FF // FF_CHUNK

    def cols(ref, c):
        off = c * FF_CHUNK if isinstance(c, int) else pl.multiple_of(c * FF_CHUNK, FF_CHUNK)
        return ref[:, pl.ds(off, FF_CHUNK)], ref[:, pl.ds(D_FF + off, FF_CHUNK)]

    def up(c, u_ref):
        lhs = lhs_ref[...]
        wa, wg = cols(wup_ref, c)
        u_ref[:, 0:FF_CHUNK] = jnp.dot(lhs, wa, preferred_element_type=F32)
        u_ref[:, FF_CHUNK:] = jnp.dot(lhs, wg, preferred_element_type=F32)

    def gate_down(c, u_ref):
        w = jnp.concatenate(cols(cw_ref, c), axis=1)
        b = jnp.concatenate(cols(cb_ref, c), axis=1)
        conv = (u_ref[pl.ds(H - 1, tm), :] * w[0:1] + u_ref[pl.ds(H, tm), :] * w[1:2]
                + u_ref[pl.ds(H + 1, tm), :] * w[2:3] + b)
        a = conv[:, 0:FF_CHUNK]
        gt = conv[:, FF_CHUNK:2 * FF_CHUNK]
        act = (gt / (1.0 + jnp.exp(-gt))) * a
        row0 = c * FF_CHUNK if isinstance(c, int) else pl.multiple_of(c * FF_CHUNK, FF_CHUNK)
        acc_ref[...] += jnp.dot(act.astype(BF16), wdn_ref[pl.ds(row0, FF_CHUNK), :], preferred_element_type=F32)

    assert nchunk % 2 == 1
    up(0, u0_ref)

    def body(i, carry):
        c = 2 * i
        up(c + 1, u1_ref)
        gate_down(c, u0_ref)
        up(c + 2, u0_ref)
        gate_down(c + 1, u1_ref)
        return carry

    lax.fori_loop(0, nchunk // 2, body, 0)
    gate_down(nchunk - 1, u0_ref)
    x2 = x1_ref[...] + acc_ref[...]
    y_ref[...] = x2 * lax.rsqrt(jnp.mean(x2 * x2, axis=-1, keepdims=True) + EPS) * fw_ref[...]


def _ffn(x2, m_out, a_out, T, wom, woa, n2, wup, cw, cb, wdn, fw, tm):
    N = x2.shape[0]
    H = CONV_HALO
    tps = T // tm
    r = tm // H
    nblk = N // H
    row = lambda i: (i, 0)
    prev = lambda i: (jnp.maximum(i * r - 1, 0), 0)
    nxt = lambda i: (jnp.minimum((i + 1) * r, nblk - 1), 0)

    def halo3(width):
        return [pl.BlockSpec((H, width), prev), pl.BlockSpec((tm, width), row), pl.BlockSpec((H, width), nxt)]

    return pl.pallas_call(
        functools.partial(_ffn_kernel, tiles_per_seq=tps),
        grid=(N // tm,),
        in_specs=halo3(D_MODEL) + halo3(M_WIDTH) + halo3(A_WIDTH) + [
            _const_spec(wom.shape), _const_spec(woa.shape), _const_spec(n2.shape),
            _const_spec(wup.shape), _const_spec(cw.shape), _const_spec(cb.shape), _const_spec(wdn.shape),
            _const_spec(fw.shape),
        ],
        out_specs=pl.BlockSpec((tm, D_MODEL), row),
        out_shape=jax.ShapeDtypeStruct((N, D_MODEL), F32),
        scratch_shapes=[pltpu.VMEM((tm + 2 * H, D_MODEL), BF16), pltpu.VMEM((tm, D_MODEL), F32),
                        pltpu.VMEM((tm + 2 * H, 2 * FF_CHUNK), F32), pltpu.VMEM((tm + 2 * H, 2 * FF_CHUNK), F32),
                        pltpu.VMEM((tm, D_MODEL), F32)],
        compiler_params=_cparams(("parallel",)),
        name="ffn",
    )(x2, x2, x2, m_out, m_out, m_out, a_out, a_out, a_out, wom, woa, n2, wup, cw, cb, wdn, fw)


def _rope_tables(T, tm):
    nf = HEAD_DIM // 4
    inv = ROPE_THETA ** (-jnp.arange(nf, dtype=F32) / nf)
    zero = jnp.zeros((1, 2 * nf), F32)

    def parts(pos, row_part):
        ang = pos[:, None] * inv
        cos, sin = jnp.cos(ang), jnp.sin(ang)
        z = jnp.broadcast_to(zero, (pos.shape[0], 2 * nf))
        halves = lambda a, b: jnp.concatenate([a, b, z] if row_part else [z, a, b], axis=-1)
        tabs = (halves(cos, cos), halves(-sin, 0.0 * sin), halves(0.0 * sin, sin))
        return jnp.stack([jnp.concatenate([t, t], axis=-1) for t in tabs])

    rrow = parts(jnp.arange(T // GRID_W, dtype=F32), True)
    rrow = rrow.reshape(3, T // tm, tm // GRID_W, LANES).transpose(1, 0, 2, 3)
    rcol = parts(jnp.arange(GRID_W, dtype=F32), False)
    return rrow, jnp.tile(rcol, (1, tm // GRID_W, 1))


def _prep_weights(w_in, b_gates, mh_norm_w, q_norm_w, k_norm_w, w_out, norm1_w, norm2_w, w_up, conv_w, conv_b,
                  w_down, final_norm_w):
    gate0 = 4 * M_WIDTH
    wm = w_in[:, :gate0].astype(BF16)
    wg = jnp.pad(w_in[:, gate0:gate0 + N_GATES], ((0, 0), (0, LANES - N_GATES))).astype(BF16)
    wa = w_in[:, gate0 + N_GATES:].astype(BF16)
    bg = jnp.pad(b_gates, (0, LANES - N_GATES)).reshape(1, LANES)
    return dict(
        wm=wm, wg=wg, wa=wa, bg=bg,
        n1=norm1_w.reshape(1, D_MODEL), n2=norm2_w.reshape(1, D_MODEL), fw=final_norm_w.reshape(1, D_MODEL),
        qw2=jnp.tile(q_norm_w, 2).reshape(1, LANES), kw2=jnp.tile(k_norm_w, 2).reshape(1, LANES),
        nw=mh_norm_w.reshape(1, M_WIDTH),
        wom=w_out[:M_WIDTH].astype(BF16), woa=w_out[M_WIDTH:].astype(BF16),
        wup=w_up.astype(BF16), cw=conv_w, cb=conv_b.reshape(1, 2 * D_FF), wdn=w_down.astype(BF16),
    )


def _tiles(T):
    return min(512, T // 2)


def _trunk(x, p):
    B, T, _ = x.shape
    N = B * T
    tm = _tiles(T)
    x2 = x.reshape(N, D_MODEL)
    rrow, rcol = _rope_tables(T, tm)
    blk = np.arange(LANES) // HEAD_DIM
    ones_bd = jnp.asarray(blk[:, None] == blk[None, :], BF16)
    idx = np.arange(tm)
    tri = jnp.asarray((idx[None, :] <= idx[:, None])
                      & (idx[None, :] // MLSTM_CHUNK == idx[:, None] // MLSTM_CHUNK), BF16)

    mq, mk, mv, mo, cum, e, emax, ends, qt, k, vt = _inproj(
        x2, T, p["n1"], p["wm"], p["wg"], p["wa"], p["bg"], p["qw2"], p["kw2"], rrow, rcol, ones_bd, tri, tm)
    nwb = jnp.broadcast_to(p["nw"].reshape(M_HEADS, HEAD_DIM, 1), (M_HEADS, HEAD_DIM, MLSTM_CHUNK))
    s_f, m_f, s_b, m_b = _mlstm_state(mk, mv, e, ends, T)
    m_out = _mlstm_out(mq, mk, mv, mo, cum, e, emax, s_f, m_f, s_b, m_b, nwb)
    a_out = _attn(qt, k, vt, T)
    y = _ffn(x2, m_out, a_out, T, p["wom"], p["woa"], p["n2"], p["wup"], p["cw"], p["cb"], p["wdn"], p["fw"], tm)
    return y.reshape(B, T, D_MODEL)


def kernel(x_prompt, x_sample, w_in, b_gates, mh_norm_w, q_norm_w, k_norm_w, w_out, norm1_w, norm2_w, w_up,
           conv_w, conv_b, w_down, final_norm_w):
    assert w_in.shape[0] == 1, "single-layer trunk"
    p = _prep_weights(w_in[0], b_gates[0], mh_norm_w[0], q_norm_w[0], k_norm_w[0], w_out[0], norm1_w[0],
                      norm2_w[0], w_up[0], conv_w[0], conv_b[0], w_down[0], final_norm_w)
    return (_trunk(x_prompt, p), _trunk(x_sample, p))
```

```python
import functools
import math

import jax
import jax.numpy as jnp
import numpy as np
from jax import lax
from jax.experimental import pallas as pl
from jax.experimental.pallas import tpu as pltpu

F32 = jnp.float32
BF16 = jnp.bfloat16

D_MODEL = 1024
HEAD_DIM = 64
M_HEADS = 8
M_WIDTH = M_HEADS * HEAD_DIM
A_HEADS = 8
A_KV_HEADS = 2
A_GROUP = A_HEADS // A_KV_HEADS
A_WIDTH = A_HEADS * HEAD_DIM
KV_WIDTH = A_KV_HEADS * HEAD_DIM
N_GATES = 4 * M_HEADS
D_FF = 2816
GRID_W = 64
ROPE_THETA = 10000.0
EPS = 1e-6

LANES = 128
SUBLANES = 8
VMEM_LIMIT = 56 * 1024 * 1024

MLSTM_CHUNK = 256
MLSTM_VPAD = 16
MLSTM_STATE_TILES = 2
FF_CHUNK = 256
CONV_HALO = 16
NEG = -1e30
ATTN_QW = 256
ATTN_VPAD = 16


def _cparams(sem):
    return pltpu.CompilerParams(dimension_semantics=sem, vmem_limit_bytes=VMEM_LIMIT)


def _const_spec(shape):
    nd = len(shape)
    return pl.BlockSpec(shape, lambda *_: (0,) * nd, pipeline_mode=pl.Buffered(1))


def _inproj_kernel(x_ref, n1_ref, wm_ref, wg_ref, wa_ref, bg_ref, qw_ref, kw_ref, rrow_ref, rcol_ref,
                   ones_ref, tri_ref, mq_ref, mk_ref, mv_ref, mo_ref, cum_ref, e_ref, emax_ref, ends_ref,
                   qt_ref, k_ref, vt_ref):
    x = x_ref[...]
    h = x * lax.rsqrt(jnp.mean(x * x, axis=-1, keepdims=True) + EPS) * n1_ref[...]
    h = h.astype(BF16)

    g = jnp.dot(h, wg_ref[...], preferred_element_type=F32) + bg_ref[...]
    cum_ref[...], e_ref[...], emax_ref[...], ends_ref[...] = _gate_sums(g, tri_ref[...])

    pa = jnp.dot(h, wa_ref[...], preferred_element_type=F32)
    tm = x.shape[0]
    rrow = rrow_ref[0]
    cos, sa, sb = (jnp.broadcast_to(rrow[i][:, None, :], (tm // GRID_W, GRID_W, LANES)).reshape(tm, LANES)
                   + rcol_ref[i] for i in range(3))
    ones_bd = ones_ref[...]

    def norm_rope(y, w):
        sq = y * y
        hi = sq.astype(BF16)
        lo = (sq - hi.astype(F32)).astype(BF16)
        ss = (jnp.dot(hi, ones_bd, preferred_element_type=F32)
              + jnp.dot(lo, ones_bd, preferred_element_type=F32))
        yn = y * lax.rsqrt(ss * (1.0 / HEAD_DIM) + EPS) * w
        up = pltpu.roll(yn, LANES - HEAD_DIM // 4, axis=1)
        dn = pltpu.roll(yn, HEAD_DIM // 4, axis=1)
        return yn * cos + up * sa + dn * sb

    qscale = (HEAD_DIM ** -0.5) * math.log2(math.e)
    for c in range(A_WIDTH // LANES):
        q2 = norm_rope(pa[:, c * LANES:(c + 1) * LANES], qw_ref[...]) * qscale
        q2t = q2.T.astype(BF16)
        qt_ref[2 * c, 0] = q2t[0:HEAD_DIM]
        qt_ref[2 * c + 1, 0] = q2t[HEAD_DIM:2 * HEAD_DIM]
    k2 = norm_rope(pa[:, A_WIDTH:A_WIDTH + KV_WIDTH], kw_ref[...]).astype(BF16)
    k_ref[0, 0] = k2[:, 0:HEAD_DIM]
    k_ref[1, 0] = k2[:, HEAD_DIM:2 * HEAD_DIM]
    v2t = pa[:, A_WIDTH + KV_WIDTH:A_WIDTH + 2 * KV_WIDTH].T.astype(BF16)
    vt_ref[0, 0] = v2t[0:HEAD_DIM]
    vt_ref[1, 0] = v2t[HEAD_DIM:2 * HEAD_DIM]

    pm = jnp.dot(h, wm_ref[...], preferred_element_type=F32)

    def heads_t(cols0, ref):
        for c in range(M_WIDTH // LANES):
            t2 = pm[:, cols0 + c * LANES:cols0 + (c + 1) * LANES].T.astype(BF16)
            ref[2 * c, 0] = t2[0:HEAD_DIM]
            ref[2 * c + 1, 0] = t2[HEAD_DIM:2 * HEAD_DIM]

    heads_t(0, mq_ref)
    heads_t(2 * M_WIDTH, mv_ref)
    heads_t(3 * M_WIDTH, mo_ref)
    mk_ref[0] = (pm[:, M_WIDTH:2 * M_WIDTH] * (HEAD_DIM ** -0.5)).astype(BF16)


def _inproj(x2, T, n1, wm, wg, wa, bg, qw2, kw2, rrow, rcol, ones_bd, tri, tm):
    N = x2.shape[0]
    tps = T // tm
    row = lambda i: (i, 0)
    out_shape = (
        jax.ShapeDtypeStruct((M_HEADS, N // tm, HEAD_DIM, tm), BF16),
        jax.ShapeDtypeStruct((N // tm, tm, M_WIDTH), BF16),
        jax.ShapeDtypeStruct((M_HEADS, N // tm, HEAD_DIM, tm), BF16),
        jax.ShapeDtypeStruct((M_HEADS, N // tm, HEAD_DIM, tm), BF16),
        jax.ShapeDtypeStruct((N, LANES), F32), jax.ShapeDtypeStruct((N, LANES), F32),
        jax.ShapeDtypeStruct((N, LANES), F32), jax.ShapeDtypeStruct((N // MLSTM_CHUNK, SUBLANES, LANES), F32),
        jax.ShapeDtypeStruct((A_HEADS, N // tm, HEAD_DIM, tm), BF16),
        jax.ShapeDtypeStruct((A_KV_HEADS, N // tm, tm, HEAD_DIM), BF16),
        jax.ShapeDtypeStruct((A_KV_HEADS, N // tm, HEAD_DIM, tm), BF16),
    )
    return pl.pallas_call(
        _inproj_kernel,
        grid=(N // tm,),
        in_specs=[
            pl.BlockSpec((tm, D_MODEL), row),
            _const_spec(n1.shape), _const_spec(wm.shape), _const_spec(wg.shape), _const_spec(wa.shape),
            _const_spec(bg.shape), _const_spec(qw2.shape), _const_spec(kw2.shape),
            pl.BlockSpec((1,) + rrow.shape[1:], lambda i: (i % tps, 0, 0, 0)), _const_spec(rcol.shape),
            _const_spec(ones_bd.shape), _const_spec(tri.shape),
        ],
        out_specs=(
            pl.BlockSpec((M_HEADS, 1, HEAD_DIM, tm), lambda i: (0, i, 0, 0)),
            pl.BlockSpec((1, tm, M_WIDTH), lambda i: (i, 0, 0)),
            pl.BlockSpec((M_HEADS, 1, HEAD_DIM, tm), lambda i: (0, i, 0, 0)),
            pl.BlockSpec((M_HEADS, 1, HEAD_DIM, tm), lambda i: (0, i, 0, 0)),
            pl.BlockSpec((tm, LANES), row), pl.BlockSpec((tm, LANES), row), pl.BlockSpec((tm, LANES), row),
            pl.BlockSpec((tm // MLSTM_CHUNK, SUBLANES, LANES), lambda i: (i, 0, 0)),
            pl.BlockSpec((A_HEADS, 1, HEAD_DIM, tm), lambda i: (0, i, 0, 0)),
            pl.BlockSpec((A_KV_HEADS, 1, tm, HEAD_DIM), lambda i: (0, i, 0, 0)),
            pl.BlockSpec((A_KV_HEADS, 1, HEAD_DIM, tm), lambda i: (0, i, 0, 0)),
        ),
        out_shape=out_shape,
        compiler_params=_cparams(("parallel",)),
        name="inproj",
    )(x2, n1, wm, wg, wa, bg, qw2, kw2, rrow, rcol, ones_bd, tri)


def _log_sigmoid(x):
    return jnp.minimum(x, 0.0) - jnp.log1p(jnp.exp(-jnp.abs(x)))


def _split3(x):
    a = x.astype(BF16)
    r = x - a.astype(F32)
    b = r.astype(BF16)
    c = (r - b.astype(F32)).astype(BF16)
    return a, b, c


def _gate_sums(g, tri):
    tm = g.shape[0]
    logf = _log_sigmoid(g) * math.log2(math.e)
    a, b, c = _split3(logf)
    pre = (jnp.dot(tri, a, preferred_element_type=F32) + jnp.dot(tri, b, preferred_element_type=F32)
           + jnp.dot(tri, c, preferred_element_type=F32))
    tot = jnp.concatenate([jnp.broadcast_to(pre[r + MLSTM_CHUNK - 1:r + MLSTM_CHUNK], (MLSTM_CHUNK, LANES))
                           for r in range(0, tm, MLSTM_CHUNK)], axis=0)
    fwd = _is_fwd_lane((tm, LANES))
    cum = jnp.where(fwd, pre, tot - pre + logf)
    e = pltpu.roll(g, 2 * M_HEADS, axis=1) * math.log2(math.e) - cum

    row = lax.broadcasted_iota(jnp.int32, (tm, LANES), 0) & (MLSTM_CHUNK - 1)
    up, down = e, e
    shift = 1
    while shift < MLSTM_CHUNK:
        up = jnp.maximum(up, jnp.where(row >= shift, pltpu.roll(up, shift, axis=0), NEG))
        down = jnp.maximum(down, jnp.where(row < MLSTM_CHUNK - shift, pltpu.roll(down, tm - shift, axis=0), NEG))
        shift *= 2
    ends = [jnp.concatenate([pre[r + MLSTM_CHUNK - 1:r + MLSTM_CHUNK],
                             jnp.where(fwd[0:1], up[r + MLSTM_CHUNK - 1:r + MLSTM_CHUNK], down[r:r + 1]),
                             jnp.zeros((SUBLANES - 2, LANES), F32)], axis=0) for r in range(0, tm, MLSTM_CHUNK)]
    return cum, e, jnp.where(fwd, up, down), jnp.stack(ends)


def _is_fwd_lane(shape):
    return lax.broadcasted_iota(jnp.int32, shape, len(shape) - 1) < 3 * M_HEADS


def _mlstm_state_kernel(kf_ref, vtf_ref, ef_ref, endf_ref, kb_ref, vtb_ref, eb_ref, endb_ref,
                        sf_ref, mf_ref, sb_ref, mb_ref, c_ref, m_ref):
    L = MLSTM_CHUNK
    per = ef_ref.shape[0] // L
    cpt = kf_ref.shape[1] // L

    @pl.when(pl.program_id(1) == 0)
    def _():
        c_ref[...] = jnp.zeros_like(c_ref)
        m_ref[...] = jnp.zeros_like(m_ref)

    fwd = _is_fwd_lane((1, LANES))
    ones = jnp.ones((MLSTM_VPAD, L), BF16)
    pairs = [(d, hd) for d in range(2) for hd in range(M_HEADS)]
    k_refs, vt_refs, s_refs = (kf_ref, kb_ref), (vtf_ref, vtb_ref), (sf_ref, sb_ref)
    lane = lambda d, hd: (2 + d) * M_HEADS + hd

    for j in range(per):
        jb = per - 1 - j
        rows = (slice(j * L, (j + 1) * L), slice(jb * L, (jb + 1) * L))
        tile = (j // cpt, jb // cpt)
        trow = tuple(slice((c % cpt) * L, (c % cpt + 1) * L) for c in (j, jb))
        tot_f = endf_ref[j, 0:1, :]
        tot_b = endb_ref[jb, 0:1, :]
        wend_f = tot_f + ef_ref[rows[0], :]
        wend_b = tot_b + eb_ref[rows[1], :]
        tot = jnp.where(fwd, tot_f, tot_b)
        m_prev = m_ref[...]
        m_new = jnp.maximum(tot + m_prev, tot + jnp.where(fwd, endf_ref[j, 1:2, :], endb_ref[jb, 1:2, :]))
        dec = jnp.exp2(tot + m_prev - m_new)
        m_ref[...] = m_new
        m_rows = jnp.broadcast_to(m_prev, (SUBLANES, LANES))
        mf_ref[j] = m_rows
        mb_ref[jb] = m_rows
        we_t = (jnp.exp2(wend_f - m_new).T, jnp.exp2(wend_b - m_new).T)

        wvs = {}
        for d, hd in pairs:
            vext = jnp.concatenate([vt_refs[d][hd, tile[d], :, trow[d]], ones], axis=0)
            wvs[d, hd] = (vext.astype(F32) * we_t[d][lane(d, hd):lane(d, hd) + 1, :]).astype(BF16)
        def kpair(d, hd):
            return k_refs[d][tile[d], trow[d], (hd // 2) * LANES:(hd // 2 + 1) * LANES]

        adds = {(d, hd): jnp.dot(wvs[d, hd], kpair(d, hd), preferred_element_type=F32)[
            :, (hd % 2) * HEAD_DIM:(hd % 2 + 1) * HEAD_DIM] for d, hd in pairs}
        for d, hd in pairs:
            cext = c_ref[d * M_HEADS + hd]
            s_refs[d][(j, jb)[d], hd] = cext
            c_ref[d * M_HEADS + hd] = dec[:, lane(d, hd):lane(d, hd) + 1] * cext + adds[d, hd]


def _mlstm_state(mk, mvt, e, ends, T):
    nblk, blk, _ = mk.shape
    N = nblk * blk
    L = MLSTM_CHUNK
    g = MLSTM_STATE_TILES
    per = g * blk // L
    assert (T // blk) % g == 0
    nt = T // (g * blk)
    B = N // T
    tf = lambda b, t: b * nt + t
    tb = lambda b, t: b * nt + (nt - 1 - t)

    def specs(ti):
        return [pl.BlockSpec((g, blk, M_WIDTH), lambda b, t: (ti(b, t), 0, 0)),
                pl.BlockSpec((M_HEADS, g, HEAD_DIM, blk), lambda b, t: (0, ti(b, t), 0, 0)),
                pl.BlockSpec((g * blk, LANES), lambda b, t: (ti(b, t), 0)),
                pl.BlockSpec((per, SUBLANES, LANES), lambda b, t: (ti(b, t), 0, 0))]

    def outs(ti):
        return [pl.BlockSpec((per, M_HEADS, HEAD_DIM + MLSTM_VPAD, HEAD_DIM), lambda b, t: (ti(b, t), 0, 0, 0)),
                pl.BlockSpec((per, SUBLANES, LANES), lambda b, t: (ti(b, t), 0, 0))]

    s_shape = jax.ShapeDtypeStruct((N // L, M_HEADS, HEAD_DIM + MLSTM_VPAD, HEAD_DIM), F32)
    m_shape = jax.ShapeDtypeStruct((N // L, SUBLANES, LANES), F32)
    return pl.pallas_call(
        _mlstm_state_kernel,
        grid=(B, nt),
        in_specs=specs(tf) + specs(tb),
        out_specs=tuple(outs(tf) + outs(tb)),
        out_shape=(s_shape, m_shape, s_shape, m_shape),
        scratch_shapes=[pltpu.VMEM((2 * M_HEADS, HEAD_DIM + MLSTM_VPAD, HEAD_DIM), F32), pltpu.VMEM((1, LANES), F32)],
        compiler_params=_cparams(("parallel", "arbitrary")),
        name="mlstm_state",
    )(mk, mvt, e, ends, mk, mvt, e, ends)


def _mlstm_out_kernel(qt_ref, k_ref, vt_ref, mot_ref, cum_ref, e_ref, emax_ref, sf_ref, mf_ref, sb_ref, mb_ref,
                      nw_ref, out_ref):
    L = MLSTM_CHUNK
    for j in range(e_ref.shape[0] // L):
        tok = pl.ds(j * L, L)
        _mlstm_out_chunk(qt_ref.at[:, :, :, tok], k_ref.at[0, tok, :], vt_ref.at[:, :, :, tok],
                         mot_ref.at[:, :, :, tok], cum_ref.at[tok], e_ref.at[tok], emax_ref.at[tok],
                         sf_ref.at[j:j + 1], mf_ref.at[j:j + 1], sb_ref.at[j:j + 1], mb_ref.at[j:j + 1],
                         nw_ref, out_ref.at[tok])


def _mlstm_out_chunk(qt_ref, k_ref, vt_ref, mot_ref, cum_ref, e_ref, emax_ref, sf_ref, mf_ref, sb_ref, mb_ref,
                     nw_ref, out_ref):
    L = e_ref.shape[0]
    e = e_ref[...]
    m_in = jnp.where(_is_fwd_lane((1, LANES)), mf_ref[0, 0:1, :], mb_ref[0, 0:1, :])
    cum_t = cum_ref[...].T
    mx_t = jnp.maximum(emax_ref[...], m_in).T

    si = lax.broadcasted_iota(jnp.int32, (L, L), 0)
    li = lax.broadcasted_iota(jnp.int32, (L, L), 1)
    keep = (si <= li, si >= li)
    ones = jnp.ones((MLSTM_VPAD, L), BF16)

    pairs = [(hd, d) for hd in range(M_HEADS) for d in range(2)]
    s_refs = (sf_ref, sb_ref)
    lane = lambda hd, d: (2 + d) * M_HEADS + hd
    qts = [qt_ref[hd, 0] for hd in range(M_HEADS)]
    zq = jnp.zeros((HEAD_DIM, L), BF16)
    raws = [jnp.dot(k_ref[:, (hd // 2) * LANES:(hd // 2 + 1) * LANES],
                    jnp.concatenate([qts[hd], zq] if hd % 2 == 0 else [zq, qts[hd]], axis=0),
                    preferred_element_type=F32) for hd in range(M_HEADS)]
    carried = {(hd, d): jnp.dot(s_refs[d][0, hd].astype(BF16), qts[hd], preferred_element_type=F32)
               for hd, d in pairs}
    mxs = {(hd, d): mx_t[lane(hd, d):lane(hd, d) + 1, :] for hd, d in pairs}
    ws = {(hd, d): jnp.where(keep[d], raws[hd] * jnp.exp2(e[:, lane(hd, d):lane(hd, d) + 1] - mxs[hd, d]),
                             0.0).astype(BF16) for hd, d in pairs}
    vexts = [jnp.concatenate([vt_ref[hd, 0], ones], axis=0) for hd in range(M_HEADS)]
    nds = {(hd, d): (jnp.exp2(m_in[:, lane(hd, d):lane(hd, d) + 1] - mxs[hd, d]) * carried[hd, d]
                     + jnp.dot(vexts[hd], ws[hd, d], preferred_element_type=F32)) for hd, d in pairs}
    hhs = {(hd, d): nds[hd, d][0:HEAD_DIM] / jnp.maximum(
        jnp.abs(nds[hd, d][HEAD_DIM:HEAD_DIM + 1]),
        jnp.exp2(-(cum_t[lane(hd, d):lane(hd, d) + 1, :] + mxs[hd, d]))) for hd, d in pairs}
    outs = []
    for hd in range(M_HEADS):
        hs = hhs[hd, 0] + hhs[hd, 1]
        hn = hs * lax.rsqrt(jnp.mean(hs * hs, axis=0, keepdims=True) + EPS) * nw_ref[hd]
        mo = mot_ref[hd, 0].astype(F32)
        outs.append(hn / (1.0 + jnp.exp(-mo)))
    for jj in range(M_HEADS // 2):
        o2 = jnp.concatenate([outs[2 * jj], outs[2 * jj + 1]], axis=0)
        out_ref[:, jj * LANES:(jj + 1) * LANES] = o2.T.astype(out_ref.dtype)


def _mlstm_out(mqt, mk, mvt, mot, cum, e, emax, sf, mf, sb, mb, nwb):
    _, nblk, _, blk = mqt.shape
    N = nblk * blk
    L = MLSTM_CHUNK
    per = blk // L
    tspec = pl.BlockSpec((M_HEADS, 1, HEAD_DIM, blk), lambda t: (0, t, 0, 0))
    kspec = pl.BlockSpec((1, blk, M_WIDTH), lambda t: (t, 0, 0))
    sspec = pl.BlockSpec((per, M_HEADS, HEAD_DIM + MLSTM_VPAD, HEAD_DIM), lambda t: (t, 0, 0, 0))
    mspec = pl.BlockSpec((per, SUBLANES, LANES), lambda t: (t, 0, 0))
    gspec = pl.BlockSpec((blk, LANES), lambda t: (t, 0))
    return pl.pallas_call(
        _mlstm_out_kernel,
        grid=(nblk,),
        in_specs=[tspec, kspec, tspec, tspec, gspec, gspec, gspec, sspec, mspec, sspec, mspec,
                  _const_spec(nwb.shape)],
        out_specs=pl.BlockSpec((blk, M_WIDTH), lambda t: (t, 0)),
        out_shape=jax.ShapeDtypeStruct((N, M_WIDTH), BF16),
        compiler_params=_cparams(("parallel",)),
        name="mlstm_out",
    )(mqt, mk, mvt, mot, cum, e, emax, sf, mf, sb, mb, nwb)


def _attn_kernel(qt_ref, k_ref, vt_ref, o_ref, m_ref, acc_ref, s0_ref, s1_ref, x0_ref, x1_ref):
    nkb = k_ref.shape[1]
    tk = k_ref.shape[2]
    tq = qt_ref.shape[3]
    units = [(j, c) for j in range(A_GROUP) for c in range(0, tq, ATTN_QW)]
    ones = jnp.ones((ATTN_VPAD, tk), BF16)

    m_ref[...] = jnp.full_like(m_ref, -jnp.inf)
    acc_ref[...] = jnp.zeros_like(acc_ref)

    def stage(cur, nxt):
        if nxt is not None:
            k = k_ref[0, nxt[0]]
        if cur is not None:
            vt = jnp.concatenate([vt_ref[0, cur[0]], ones], axis=0)
        for u, (j, c) in enumerate(units):
            cols = slice(c, c + ATTN_QW)
            if nxt is not None:
                st = jnp.dot(k, qt_ref[j, 0, :, cols], preferred_element_type=F32)
                nxt[1][u] = st
                nxt[2][u] = jnp.max(st, axis=0, keepdims=True)
            if cur is not None:
                st = cur[1][u]
                m_old = m_ref[j, :, cols]
                m_new = jnp.maximum(m_old, cur[2][u])
                alpha = jnp.exp2(m_old - m_new)
                p = jnp.exp2(st - m_new)
                acc_ref[j, :, cols] = (alpha * acc_ref[j, :, cols]
                                       + jnp.dot(vt, p.astype(BF16), preferred_element_type=F32))
                m_ref[j, :, cols] = m_new

    stage(None, (0, s0_ref, x0_ref))

    def body(i, carry):
        kb = 2 * i
        stage((kb, s0_ref, x0_ref), (kb + 1, s1_ref, x1_ref))
        stage((kb + 1, s1_ref, x1_ref), (kb + 2, s0_ref, x0_ref))
        return carry

    lax.fori_loop(0, nkb // 2 - 1 + jnp.minimum(pl.program_id(2), 0), body, 0)
    stage((nkb - 2, s0_ref, x0_ref), (nkb - 1, s1_ref, x1_ref))
    stage((nkb - 1, s1_ref, x1_ref), None)

    def head_out(j):
        return acc_ref[j, 0:HEAD_DIM, :] / acc_ref[j, HEAD_DIM:HEAD_DIM + 1, :]

    for jj in range(A_GROUP // 2):
        o2 = jnp.concatenate([head_out(2 * jj), head_out(2 * jj + 1)], axis=0)
        o_ref[:, jj * LANES:(jj + 1) * LANES] = o2.T.astype(o_ref.dtype)


def _attn(qt, k, vt, T):
    _, nblk, _, blk = qt.shape
    N = nblk * blk
    B = N // T
    nq = T // blk
    assert nq % 2 == 0, "attention pipelines key blocks in pairs"
    nu = A_GROUP * (blk // ATTN_QW)
    return pl.pallas_call(
        _attn_kernel,
        grid=(B, A_KV_HEADS, nq),
        in_specs=[
            pl.BlockSpec((A_GROUP, 1, HEAD_DIM, blk), lambda b, g, qi: (g, b * nq + qi, 0, 0)),
            pl.BlockSpec((1, nq, blk, HEAD_DIM), lambda b, g, qi: (g, b, 0, 0)),
            pl.BlockSpec((1, nq, HEAD_DIM, blk), lambda b, g, qi: (g, b, 0, 0)),
        ],
        out_specs=pl.BlockSpec((blk, A_GROUP * HEAD_DIM), lambda b, g, qi: (b * nq + qi, g)),
        out_shape=jax.ShapeDtypeStruct((N, A_WIDTH), BF16),
        scratch_shapes=[pltpu.VMEM((A_GROUP, 1, blk), F32),
                        pltpu.VMEM((A_GROUP, HEAD_DIM + ATTN_VPAD, blk), F32),
                        pltpu.VMEM((nu, blk, ATTN_QW), F32), pltpu.VMEM((nu, blk, ATTN_QW), F32),
                        pltpu.VMEM((nu, 1, ATTN_QW), F32), pltpu.VMEM((nu, 1, ATTN_QW), F32)],
        compiler_params=_cparams(("parallel", "parallel", "arbitrary")),
        name="attn",
    )(qt, k, vt)


def _ffn_kernel(xp_ref, xm_ref, xn_ref, mp_ref, mm_ref, mn_ref, ap_ref, am_ref, an_ref, wom_ref, woa_ref, n2_ref,
                wup_ref, cw_ref, cb_ref, wdn_ref, fw_ref, y_ref, lhs_ref, x1_ref, u0_ref, u1_ref, acc_ref,
                *, tiles_per_seq):
    i = pl.program_id(0)
    tm = xm_ref.shape[0]
    H = CONV_HALO
    t = i % tiles_per_seq
    xe = jnp.concatenate([xp_ref[...], xm_ref[...], xn_ref[...]], axis=0)
    me = jnp.concatenate([mp_ref[...], mm_ref[...], mn_ref[...]], axis=0)
    ae = jnp.concatenate([ap_ref[...], am_ref[...], an_ref[...]], axis=0)
    x1e = (xe + jnp.dot(me, wom_ref[...], preferred_element_type=F32)
           + jnp.dot(ae, woa_ref[...], preferred_element_type=F32))
    h2e = (x1e * lax.rsqrt(jnp.mean(x1e * x1e, axis=-1, keepdims=True) + EPS) * n2_ref[...]).astype(BF16)
    x1_ref[...] = x1e[H:H + tm]
    lhs_ref[0:H] = jnp.where(t == 0, jnp.zeros((H, D_MODEL), BF16), h2e[0:H])
    lhs_ref[H:H + tm] = h2e[H:H + tm]
    lhs_ref[H + tm:H + tm + H] = jnp.where(t == tiles_per_seq - 1, jnp.zeros((H, D_MODEL), BF16),
                                           h2e[H + tm:H + tm + H])
    acc_ref[...] = jnp.zeros_like(acc_ref)
    nchunk = D_FF // FF_CHUNK

    def cols(ref, c):
        off = c * FF_CHUNK if isinstance(c, int) else pl.multiple_of(c * FF_CHUNK, FF_CHUNK)
        return ref[:, pl.ds(off, FF_CHUNK)], ref[:, pl.ds(D_FF + off, FF_CHUNK)]

    def up(c, u_ref):
        lhs = lhs_ref[...]
        wa, wg = cols(wup_ref, c)
        u_ref[:, 0:FF_CHUNK] = jnp.dot(lhs, wa, preferred_element_type=F32)
        u_ref[:, FF_CHUNK:] = jnp.dot(lhs, wg, preferred_element_type=F32)

    def gate_down(c, u_ref):
        w = jnp.concatenate(cols(cw_ref, c), axis=1)
        b = jnp.concatenate(cols(cb_ref, c), axis=1)
        conv = (u_ref[pl.ds(H - 1, tm), :] * w[0:1] + u_ref[pl.ds(H, tm), :] * w[1:2]
                + u_ref[pl.ds(H + 1, tm), :] * w[2:3] + b)
        a = conv[:, 0:FF_CHUNK]
        gt = conv[:, FF_CHUNK:2 * FF_CHUNK]
        act = (gt / (1.0 + jnp.exp(-gt))) * a
        row0 = c * FF_CHUNK if isinstance(c, int) else pl.multiple_of(c * FF_CHUNK, FF_CHUNK)
        acc_ref[...] += jnp.dot(act.astype(BF16), wdn_ref[pl.ds(row0, FF_CHUNK), :], preferred_element_type=F32)

    assert nchunk % 2 == 1
    up(0, u0_ref)

    def body(i, carry):
        c = 2 * i
        up(c + 1, u1_ref)
        gate_down(c, u0_ref)
        up(c + 2, u0_ref)
        gate_down(c + 1, u1_ref)
        return carry

    lax.fori_loop(0, nchunk // 2, body, 0)
    gate_down(nchunk - 1, u0_ref)
    x2 = x1_ref[...] + acc_ref[...]
    y_ref[...] = x2 * lax.rsqrt(jnp.mean(x2 * x2, axis=-1, keepdims=True) + EPS) * fw_ref[...]


def _ffn(x2, m_out, a_out, T, wom, woa, n2, wup, cw, cb, wdn, fw, tm):
    N = x2.shape[0]
    H = CONV_HALO
    tps = T // tm
    r = tm // H
    nblk = N // H
    row = lambda i: (i, 0)
    prev = lambda i: (jnp.maximum(i * r - 1, 0), 0)
    nxt = lambda i: (jnp.minimum((i + 1) * r, nblk - 1), 0)

    def halo3(width):
        return [pl.BlockSpec((H, width), prev), pl.BlockSpec((tm, width), row), pl.BlockSpec((H, width), nxt)]

    return pl.pallas_call(
        functools.partial(_ffn_kernel, tiles_per_seq=tps),
        grid=(N // tm,),
        in_specs=halo3(D_MODEL) + halo3(M_WIDTH) + halo3(A_WIDTH) + [
            _const_spec(wom.shape), _const_spec(woa.shape), _const_spec(n2.shape),
            _const_spec(wup.shape), _const_spec(cw.shape), _const_spec(cb.shape), _const_spec(wdn.shape),
            _const_spec(fw.shape),
        ],
        out_specs=pl.BlockSpec((tm, D_MODEL), row),
        out_shape=jax.ShapeDtypeStruct((N, D_MODEL), F32),
        scratch_shapes=[pltpu.VMEM((tm + 2 * H, D_MODEL), BF16), pltpu.VMEM((tm, D_MODEL), F32),
                        pltpu.VMEM((tm + 2 * H, 2 * FF_CHUNK), F32), pltpu.VMEM((tm + 2 * H, 2 * FF_CHUNK), F32),
                        pltpu.VMEM((tm, D_MODEL), F32)],
        compiler_params=_cparams(("parallel",)),
        name="ffn",
    )(x2, x2, x2, m_out, m_out, m_out, a_out, a_out, a_out, wom, woa, n2, wup, cw, cb, wdn, fw)


def _rope_tables(T, tm):
    nf = HEAD_DIM // 4
    inv = ROPE_THETA ** (-jnp.arange(nf, dtype=F32) / nf)
    zero = jnp.zeros((1, 2 * nf), F32)

    def parts(pos, row_part):
        ang = pos[:, None] * inv
        cos, sin = jnp.cos(ang), jnp.sin(ang)
        z = jnp.broadcast_to(zero, (pos.shape[0], 2 * nf))
        halves = lambda a, b: jnp.concatenate([a, b, z] if row_part else [z, a, b], axis=-1)
        tabs = (halves(cos, cos), halves(-sin, 0.0 * sin), halves(0.0 * sin, sin))
        return jnp.stack([jnp.concatenate([t, t], axis=-1) for t in tabs])

    rrow = parts(jnp.arange(T // GRID_W, dtype=F32), True)
    rrow = rrow.reshape(3, T // tm, tm // GRID_W, LANES).transpose(1, 0, 2, 3)
    rcol = parts(jnp.arange(GRID_W, dtype=F32), False)
    return rrow, jnp.tile(rcol, (1, tm // GRID_W, 1))


def _prep_weights(w_in, b_gates, mh_norm_w, q_norm_w, k_norm_w, w_out, norm1_w, norm2_w, w_up, conv_w, conv_b,
                  w_down, final_norm_w):
    gate0 = 4 * M_WIDTH
    wm = w_in[:, :gate0].astype(BF16)
    wg = jnp.pad(w_in[:, gate0:gate0 + N_GATES], ((0, 0), (0, LANES - N_GATES))).astype(BF16)
    wa = w_in[:, gate0 + N_GATES:].astype(BF16)
    bg = jnp.pad(b_gates, (0, LANES - N_GATES)).reshape(1, LANES)
    return dict(
        wm=wm, wg=wg, wa=wa, bg=bg,
        n1=norm1_w.reshape(1, D_MODEL), n2=norm2_w.reshape(1, D_MODEL), fw=final_norm_w.reshape(1, D_MODEL),
        qw2=jnp.tile(q_norm_w, 2).reshape(1, LANES), kw2=jnp.tile(k_norm_w, 2).reshape(1, LANES),
        nw=mh_norm_w.reshape(1, M_WIDTH),
        wom=w_out[:M_WIDTH].astype(BF16), woa=w_out[M_WIDTH:].astype(BF16),
        wup=w_up.astype(BF16), cw=conv_w, cb=conv_b.reshape(1, 2 * D_FF), wdn=w_down.astype(BF16),
    )


def _tiles(T):
    return min(512, T // 2)


def _trunk(x, p):
    B, T, _ = x.shape
    N = B * T
    tm = _tiles(T)
    x2 = x.reshape(N, D_MODEL)
    rrow, rcol = _rope_tables(T, tm)
    blk = np.arange(LANES) // HEAD_DIM
    ones_bd = jnp.asarray(blk[:, None] == blk[None, :], BF16)
    idx = np.arange(tm)
    tri = jnp.asarray((idx[None, :] <= idx[:, None])
                      & (idx[None, :] // MLSTM_CHUNK == idx[:, None] // MLSTM_CHUNK), BF16)

    mq, mk, mv, mo, cum, e, emax, ends, qt, k, vt = _inproj(
        x2, T, p["n1"], p["wm"], p["wg"], p["wa"], p["bg"], p["qw2"], p["kw2"], rrow, rcol, ones_bd, tri, tm)
    nwb = jnp.broadcast_to(p["nw"].reshape(M_HEADS, HEAD_DIM, 1), (M_HEADS, HEAD_DIM, MLSTM_CHUNK))
    s_f, m_f, s_b, m_b = _mlstm_state(mk, mv, e, ends, T)
    m_out = _mlstm_out(mq, mk, mv, mo, cum, e, emax, s_f, m_f, s_b, m_b, nwb)
    a_out = _attn(qt, k, vt, T)
    y = _ffn(x2, m_out, a_out, T, p["wom"], p["woa"], p["n2"], p["wup"], p["cw"], p["cb"], p["wdn"], p["fw"], tm)
    return y.reshape(B, T, D_MODEL)


def kernel(x_prompt, x_sample, w_in, b_gates, mh_norm_w, q_norm_w, k_norm_w, w_out, norm1_w, norm2_w, w_up,
           conv_w, conv_b, w_down, final_norm_w):
    assert w_in.shape[0] == 1, "single-layer trunk"
    p = _prep_weights(w_in[0], b_gates[0], mh_norm_w[0], q_norm_w[0], k_norm_w[0], w_out[0], norm1_w[0],
                      norm2_w[0], w_up[0], conv_w[0], conv_b[0], w_down[0], final_norm_w)
    return (_trunk(x_prompt, p), _trunk(x_sample, p))
```

```python
import functools
import math

import jax
import jax.numpy as jnp
import numpy as np
from jax import lax
from jax.experimental import pallas as pl
from jax.experimental.pallas import tpu as pltpu

F32 = jnp.float32
BF16 = jnp.bfloat16

D_MODEL = 1024
HEAD_DIM = 64
M_HEADS = 8
M_WIDTH = M_HEADS * HEAD_DIM
A_HEADS = 8
A_KV_HEADS = 2
A_GROUP = A_HEADS // A_KV_HEADS
A_WIDTH = A_HEADS * HEAD_DIM
KV_WIDTH = A_KV_HEADS * HEAD_DIM
N_GATES = 4 * M_HEADS
D_FF = 2816
GRID_W = 64
ROPE_THETA = 10000.0
EPS = 1e-6

LANES = 128
SUBLANES = 8
VMEM_LIMIT = 56 * 1024 * 1024

MLSTM_CHUNK = 256
MLSTM_VPAD = 16
MLSTM_STATE_TILES = 2
FF_CHUNK = 256
CONV_HALO = 16
NEG = -1e30
ATTN_QW = 256
ATTN_VPAD = 16


def _cparams(sem):
    return pltpu.CompilerParams(dimension_semantics=sem, vmem_limit_bytes=VMEM_LIMIT)


def _const_spec(shape):
    nd = len(shape)
    return pl.BlockSpec(shape, lambda *_: (0,) * nd, pipeline_mode=pl.Buffered(1))


def _inproj_kernel(x_ref, n1_ref, wm_ref, wg_ref, wa_ref, bg_ref, qw_ref, kw_ref, rrow_ref, rcol_ref,
                   ones_ref, tri_ref, mq_ref, mk_ref, mv_ref, mo_ref, cum_ref, e_ref, emax_ref, ends_ref,
                   qt_ref, k_ref, vt_ref):
    x = x_ref[...]
    h = x * lax.rsqrt(jnp.mean(x * x, axis=-1, keepdims=True) + EPS) * n1_ref[...]
    h = h.astype(BF16)

    g = jnp.dot(h, wg_ref[...], preferred_element_type=F32) + bg_ref[...]
    cum_ref[...], e_ref[...], emax_ref[...], ends_ref[...] = _gate_sums(g, tri_ref[...])

    pa = jnp.dot(h, wa_ref[...], preferred_element_type=F32)
    tm = x.shape[0]
    rrow = rrow_ref[0]
    cos, sa, sb = (jnp.broadcast_to(rrow[i][:, None, :], (tm // GRID_W, GRID_W, LANES)).reshape(tm, LANES)
                   + rcol_ref[i] for i in range(3))
    ones_bd = ones_ref[...]

    def norm_rope(y, w):
        sq = y * y
        hi = sq.astype(BF16)
        lo = (sq - hi.astype(F32)).astype(BF16)
        ss = (jnp.dot(hi, ones_bd, preferred_element_type=F32)
              + jnp.dot(lo, ones_bd, preferred_element_type=F32))
        yn = y * lax.rsqrt(ss * (1.0 / HEAD_DIM) + EPS) * w
        up = pltpu.roll(yn, LANES - HEAD_DIM // 4, axis=1)
        dn = pltpu.roll(yn, HEAD_DIM // 4, axis=1)
        return yn * cos + up * sa + dn * sb

    qscale = (HEAD_DIM ** -0.5) * math.log2(math.e)
    for c in range(A_WIDTH // LANES):
        q2 = norm_rope(pa[:, c * LANES:(c + 1) * LANES], qw_ref[...]) * qscale
        q2t = q2.T.astype(BF16)
        qt_ref[2 * c, 0] = q2t[0:HEAD_DIM]
        qt_ref[2 * c + 1, 0] = q2t[HEAD_DIM:2 * HEAD_DIM]
    k2 = norm_rope(pa[:, A_WIDTH:A_WIDTH + KV_WIDTH], kw_ref[...]).astype(BF16)
    k_ref[0, 0] = k2[:, 0:HEAD_DIM]
    k_ref[1, 0] = k2[:, HEAD_DIM:2 * HEAD_DIM]
    v2t = pa[:, A_WIDTH + KV_WIDTH:A_WIDTH + 2 * KV_WIDTH].T.astype(BF16)
    vt_ref[0, 0] = v2t[0:HEAD_DIM]
    vt_ref[1, 0] = v2t[HEAD_DIM:2 * HEAD_DIM]

    pm = jnp.dot(h, wm_ref[...], preferred_element_type=F32)

    def heads_t(cols0, ref):
        for c in range(M_WIDTH // LANES):
            t2 = pm[:, cols0 + c * LANES:cols0 + (c + 1) * LANES].T.astype(BF16)
            ref[2 * c, 0] = t2[0:HEAD_DIM]
            ref[2 * c + 1, 0] = t2[HEAD_DIM:2 * HEAD_DIM]

    heads_t(0, mq_ref)
    heads_t(2 * M_WIDTH, mv_ref)
    heads_t(3 * M_WIDTH, mo_ref)
    mk_ref[0] = (pm[:, M_WIDTH:2 * M_WIDTH] * (HEAD_DIM ** -0.5)).astype(BF16)


def _inproj(x2, T, n1, wm, wg, wa, bg, qw2, kw2, rrow, rcol, ones_bd, tri, tm):
    N = x2.shape[0]
    tps = T // tm
    row = lambda i: (i, 0)
    out_shape = (
        jax.ShapeDtypeStruct((M_HEADS, N // tm, HEAD_DIM, tm), BF16),
        jax.ShapeDtypeStruct((N // tm, tm, M_WIDTH), BF16),
        jax.ShapeDtypeStruct((M_HEADS, N // tm, HEAD_DIM, tm), BF16),
        jax.ShapeDtypeStruct((M_HEADS, N // tm, HEAD_DIM, tm), BF16),
        jax.ShapeDtypeStruct((N, LANES), F32), jax.ShapeDtypeStruct((N, LANES), F32),
        jax.ShapeDtypeStruct((N, LANES), F32), jax.ShapeDtypeStruct((N // MLSTM_CHUNK, SUBLANES, LANES), F32),
        jax.ShapeDtypeStruct((A_HEADS, N // tm, HEAD_DIM, tm), BF16),
        jax.ShapeDtypeStruct((A_KV_HEADS, N // tm, tm, HEAD_DIM), BF16),
        jax.ShapeDtypeStruct((A_KV_HEADS, N // tm, HEAD_DIM, tm), BF16),
    )
    return pl.pallas_call(
        _inproj_kernel,
        grid=(N // tm,),
        in_specs=[
            pl.BlockSpec((tm, D_MODEL), row),
            _const_spec(n1.shape), _const_spec(wm.shape), _const_spec(wg.shape), _const_spec(wa.shape),
            _const_spec(bg.shape), _const_spec(qw2.shape), _const_spec(kw2.shape),
            pl.BlockSpec((1,) + rrow.shape[1:], lambda i: (i % tps, 0, 0, 0)), _const_spec(rcol.shape),
            _const_spec(ones_bd.shape), _const_spec(tri.shape),
        ],
        out_specs=(
            pl.BlockSpec((M_HEADS, 1, HEAD_DIM, tm), lambda i: (0, i, 0, 0)),
            pl.BlockSpec((1, tm, M_WIDTH), lambda i: (i, 0, 0)),
            pl.BlockSpec((M_HEADS, 1, HEAD_DIM, tm), lambda i: (0, i, 0, 0)),
            pl.BlockSpec((M_HEADS, 1, HEAD_DIM, tm), lambda i: (0, i, 0, 0)),
            pl.BlockSpec((tm, LANES), row), pl.BlockSpec((tm, LANES), row), pl.BlockSpec((tm, LANES), row),
            pl.BlockSpec((tm // MLSTM_CHUNK, SUBLANES, LANES), lambda i: (i, 0, 0)),
            pl.BlockSpec((A_HEADS, 1, HEAD_DIM, tm), lambda i: (0, i, 0, 0)),
            pl.BlockSpec((A_KV_HEADS, 1, tm, HEAD_DIM), lambda i: (0, i, 0, 0)),
            pl.BlockSpec((A_KV_HEADS, 1, HEAD_DIM, tm), lambda i: (0, i, 0, 0)),
        ),
        out_shape=out_shape,
        compiler_params=_cparams(("parallel",)),
        name="inproj",
    )(x2, n1, wm, wg, wa, bg, qw2, kw2, rrow, rcol, ones_bd, tri)


def _log_sigmoid(x):
    return jnp.minimum(x, 0.0) - jnp.log1p(jnp.exp(-jnp.abs(x)))


def _split3(x):
    a = x.astype(BF16)
    r = x - a.astype(F32)
    b = r.astype(BF16)
    c = (r - b.astype(F32)).astype(BF16)
    return a, b, c


def _gate_sums(g, tri):
    tm = g.shape[0]
    logf = _log_sigmoid(g) * math.log2(math.e)
    a, b, c = _split3(logf)
    pre = (jnp.dot(tri, a, preferred_element_type=F32) + jnp.dot(tri, b, preferred_element_type=F32)
           + jnp.dot(tri, c, preferred_element_type=F32))
    tot = jnp.concatenate([jnp.broadcast_to(pre[r + MLSTM_CHUNK - 1:r + MLSTM_CHUNK], (MLSTM_CHUNK, LANES))
                           for r in range(0, tm, MLSTM_CHUNK)], axis=0)
    fwd = _is_fwd_lane((tm, LANES))
    cum = jnp.where(fwd, pre, tot - pre + logf)
    e = pltpu.roll(g, 2 * M_HEADS, axis=1) * math.log2(math.e) - cum

    row = lax.broadcasted_iota(jnp.int32, (tm, LANES), 0) & (MLSTM_CHUNK - 1)
    up, down = e, e
    shift = 1
    while shift < MLSTM_CHUNK:
        up = jnp.maximum(up, jnp.where(row >= shift, pltpu.roll(up, shift, axis=0), NEG))
        down = jnp.maximum(down, jnp.where(row < MLSTM_CHUNK - shift, pltpu.roll(down, tm - shift, axis=0), NEG))
        shift *= 2
    ends = [jnp.concatenate([pre[r + MLSTM_CHUNK - 1:r + MLSTM_CHUNK],
                             jnp.where(fwd[0:1], up[r + MLSTM_CHUNK - 1:r + MLSTM_CHUNK], down[r:r + 1]),
                             jnp.zeros((SUBLANES - 2, LANES), F32)], axis=0) for r in range(0, tm, MLSTM_CHUNK)]
    return cum, e, jnp.where(fwd, up, down), jnp.stack(ends)


def _is_fwd_lane(shape):
    return lax.broadcasted_iota(jnp.int32, shape, len(shape) - 1) < 3 * M_HEADS


def _mlstm_state_kernel(kf_ref, vtf_ref, ef_ref, endf_ref, kb_ref, vtb_ref, eb_ref, endb_ref,
                        sf_ref, mf_ref, sb_ref, mb_ref, c_ref, m_ref):
    L = MLSTM_CHUNK
    per = ef_ref.shape[0] // L
    cpt = kf_ref.shape[1] // L

    @pl.when(pl.program_id(1) == 0)
    def _():
        c_ref[...] = jnp.zeros_like(c_ref)
        m_ref[...] = jnp.zeros_like(m_ref)

    fwd = _is_fwd_lane((1, LANES))
    ones = jnp.ones((MLSTM_VPAD, L), BF16)
    pairs = [(d, hd) for d in range(2) for hd in range(M_HEADS)]
    k_refs, vt_refs, s_refs = (kf_ref, kb_ref), (vtf_ref, vtb_ref), (sf_ref, sb_ref)
    lane = lambda d, hd: (2 + d) * M_HEADS + hd

    for j in range(per):
        jb = per - 1 - j
        rows = (slice(j * L, (j + 1) * L), slice(jb * L, (jb + 1) * L))
        tile = (j // cpt, jb // cpt)
        trow = tuple(slice((c % cpt) * L, (c % cpt + 1) * L) for c in (j, jb))
        tot_f = endf_ref[j, 0:1, :]
        tot_b = endb_ref[jb, 0:1, :]
        wend_f = tot_f + ef_ref[rows[0], :]
        wend_b = tot_b + eb_ref[rows[1], :]
        tot = jnp.where(fwd, tot_f, tot_b)
        m_prev = m_ref[...]
        m_new = jnp.maximum(tot + m_prev, tot + jnp.where(fwd, endf_ref[j, 1:2, :], endb_ref[jb, 1:2, :]))
        dec = jnp.exp2(tot + m_prev - m_new)
        m_ref[...] = m_new
        m_rows = jnp.broadcast_to(m_prev, (SUBLANES, LANES))
        mf_ref[j] = m_rows
        mb_ref[jb] = m_rows
        we_t = (jnp.exp2(wend_f - m_new).T, jnp.exp2(wend_b - m_new).T)

        wvs = {}
        for d, hd in pairs:
            vext = jnp.concatenate([vt_refs[d][hd, tile[d], :, trow[d]], ones], axis=0)
            wvs[d, hd] = (vext.astype(F32) * we_t[d][lane(d, hd):lane(d, hd) + 1, :]).astype(BF16)
        def kpair(d, hd):
            return k_refs[d][tile[d], trow[d], (hd // 2) * LANES:(hd // 2 + 1) * LANES]

        adds = {(d, hd): jnp.dot(wvs[d, hd], kpair(d, hd), preferred_element_type=F32)[
            :, (hd % 2) * HEAD_DIM:(hd % 2 + 1) * HEAD_DIM] for d, hd in pairs}
        for d, hd in pairs:
            cext = c_ref[d * M_HEADS + hd]
            s_refs[d][(j, jb)[d], hd] = cext
            c_ref[d * M_HEADS + hd] = dec[:, lane(d, hd):lane(d, hd) + 1] * cext + adds[d, hd]


def _mlstm_state(mk, mvt, e, ends, T):
    nblk, blk, _ = mk.shape
    N = nblk * blk
    L = MLSTM_CHUNK
    g = MLSTM_STATE_TILES
    per = g * blk // L
    assert (T // blk) % g == 0
    nt = T // (g * blk)
    B = N // T
    tf = lambda b, t: b * nt + t
    tb = lambda b, t: b * nt + (nt - 1 - t)

    def specs(ti):
        return [pl.BlockSpec((g, blk, M_WIDTH), lambda b, t: (ti(b, t), 0, 0)),
                pl.BlockSpec((M_HEADS, g, HEAD_DIM, blk), lambda b, t: (0, ti(b, t), 0, 0)),
                pl.BlockSpec((g * blk, LANES), lambda b, t: (ti(b, t), 0)),
                pl.BlockSpec((per, SUBLANES, LANES), lambda b, t: (ti(b, t), 0, 0))]

    def outs(ti):
        return [pl.BlockSpec((per, M_HEADS, HEAD_DIM + MLSTM_VPAD, HEAD_DIM), lambda b, t: (ti(b, t), 0, 0, 0)),
                pl.BlockSpec((per, SUBLANES, LANES), lambda b, t: (ti(b, t), 0, 0))]

    s_shape = jax.ShapeDtypeStruct((N // L, M_HEADS, HEAD_DIM + MLSTM_VPAD, HEAD_DIM), F32)
    m_shape = jax.ShapeDtypeStruct((N // L, SUBLANES, LANES), F32)
    return pl.pallas_call(
        _mlstm_state_kernel,
        grid=(B, nt),
        in_specs=specs(tf) + specs(tb),
        out_specs=tuple(outs(tf) + outs(tb)),
        out_shape=(s_shape, m_shape, s_shape, m_shape),
        scratch_shapes=[pltpu.VMEM((2 * M_HEADS, HEAD_DIM + MLSTM_VPAD, HEAD_DIM), F32), pltpu.VMEM((1, LANES), F32)],
        compiler_params=_cparams(("parallel", "arbitrary")),
        name="mlstm_state",
    )(mk, mvt, e, ends, mk, mvt, e, ends)


def _mlstm_out_kernel(qt_ref, k_ref, vt_ref, mot_ref, cum_ref, e_ref, emax_ref, sf_ref, mf_ref, sb_ref, mb_ref,
                      nw_ref, out_ref):
    L = MLSTM_CHUNK
    for j in range(e_ref.shape[0] // L):
        tok = pl.ds(j * L, L)
        _mlstm_out_chunk(qt_ref.at[:, :, :, tok], k_ref.at[0, tok, :], vt_ref.at[:, :, :, tok],
                         mot_ref.at[:, :, :, tok], cum_ref.at[tok], e_ref.at[tok], emax_ref.at[tok],
                         sf_ref.at[j:j + 1], mf_ref.at[j:j + 1], sb_ref.at[j:j + 1], mb_ref.at[j:j + 1],
                         nw_ref, out_ref.at[tok])


def _mlstm_out_chunk(qt_ref, k_ref, vt_ref, mot_ref, cum_ref, e_ref, emax_ref, sf_ref, mf_ref, sb_ref, mb_ref,
                     nw_ref, out_ref):
    L = e_ref.shape[0]
    e = e_ref[...]
    m_in = jnp.where(_is_fwd_lane((1, LANES)), mf_ref[0, 0:1, :], mb_ref[0, 0:1, :])
    cum_t = cum_ref[...].T
    mx_t = jnp.maximum(emax_ref[...], m_in).T

    si = lax.broadcasted_iota(jnp.int32, (L, L), 0)
    li = lax.broadcasted_iota(jnp.int32, (L, L), 1)
    keep = (si <= li, si >= li)
    ones = jnp.ones((MLSTM_VPAD, L), BF16)

    pairs = [(hd, d) for hd in range(M_HEADS) for d in range(2)]
    s_refs = (sf_ref, sb_ref)
    lane = lambda hd, d: (2 + d) * M_HEADS + hd
    qts = [qt_ref[hd, 0] for hd in range(M_HEADS)]
    zq = jnp.zeros((HEAD_DIM, L), BF16)
    raws = [jnp.dot(k_ref[:, (hd // 2) * LANES:(hd // 2 + 1) * LANES],
                    jnp.concatenate([qts[hd], zq] if hd % 2 == 0 else [zq, qts[hd]], axis=0),
                    preferred_element_type=F32) for hd in range(M_HEADS)]
    carried = {(hd, d): jnp.dot(s_refs[d][0, hd].astype(BF16), qts[hd], preferred_element_type=F32)
               for hd, d in pairs}
    mxs = {(hd, d): mx_t[lane(hd, d):lane(hd, d) + 1, :] for hd, d in pairs}
    ws = {(hd, d): jnp.where(keep[d], raws[hd] * jnp.exp2(e[:, lane(hd, d):lane(hd, d) + 1] - mxs[hd, d]),
                             0.0).astype(BF16) for hd, d in pairs}
    vexts = [jnp.concatenate([vt_ref[hd, 0], ones], axis=0) for hd in range(M_HEADS)]
    nds = {(hd, d): (jnp.exp2(m_in[:, lane(hd, d):lane(hd, d) + 1] - mxs[hd, d]) * carried[hd, d]
                     + jnp.dot(vexts[hd], ws[hd, d], preferred_element_type=F32)) for hd, d in pairs}
    hhs = {(hd, d): nds[hd, d][0:HEAD_DIM] / jnp.maximum(
        jnp.abs(nds[hd, d][HEAD_DIM:HEAD_DIM + 1]),
        jnp.exp2(-(cum_t[lane(hd, d):lane(hd, d) + 1, :] + mxs[hd, d]))) for hd, d in pairs}
    outs = []
    for hd in range(M_HEADS):
        hs = hhs[hd, 0] + hhs[hd, 1]
        hn = hs * lax.rsqrt(jnp.mean(hs * hs, axis=0, keepdims=True) + EPS) * nw_ref[hd]
        mo = mot_ref[hd, 0].astype(F32)
        outs.append(hn / (1.0 + jnp.exp(-mo)))
    for jj in range(M_HEADS // 2):
        o2 = jnp.concatenate([outs[2 * jj], outs[2 * jj + 1]], axis=0)
        out_ref[:, jj * LANES:(jj + 1) * LANES] = o2.T.astype(out_ref.dtype)


def _mlstm_out(mqt, mk, mvt, mot, cum, e, emax, sf, mf, sb, mb, nwb):
    _, nblk, _, blk = mqt.shape
    N = nblk * blk
    L = MLSTM_CHUNK
    per = blk // L
    tspec = pl.BlockSpec((M_HEADS, 1, HEAD_DIM, blk), lambda t: (0, t, 0, 0))
    kspec = pl.BlockSpec((1, blk, M_WIDTH), lambda t: (t, 0, 0))
    sspec = pl.BlockSpec((per, M_HEADS, HEAD_DIM + MLSTM_VPAD, HEAD_DIM), lambda t: (t, 0, 0, 0))
    mspec = pl.BlockSpec((per, SUBLANES, LANES), lambda t: (t, 0, 0))
    gspec = pl.BlockSpec((blk, LANES), lambda t: (t, 0))
    return pl.pallas_call(
        _mlstm_out_kernel,
        grid=(nblk,),
        in_specs=[tspec, kspec, tspec, tspec, gspec, gspec, gspec, sspec, mspec, sspec, mspec,
                  _const_spec(nwb.shape)],
        out_specs=pl.BlockSpec((blk, M_WIDTH), lambda t: (t, 0)),
        out_shape=jax.ShapeDtypeStruct((N, M_WIDTH), BF16),
        compiler_params=_cparams(("parallel",)),
        name="mlstm_out",
    )(mqt, mk, mvt, mot, cum, e, emax, sf, mf, sb, mb, nwb)


def _attn_kernel(qt_ref, k_ref, vt_ref, o_ref, m_ref, acc_ref, s0_ref, s1_ref, x0_ref, x1_ref):
    nkb = k_ref.shape[1]
    tk = k_ref.shape[2]
    tq = qt_ref.shape[3]
    units = [(j, c) for j in range(A_GROUP) for c in range(0, tq, ATTN_QW)]
    ones = jnp.ones((ATTN_VPAD, tk), BF16)

    m_ref[...] = jnp.full_like(m_ref, -jnp.inf)
    acc_ref[...] = jnp.zeros_like(acc_ref)

    def stage(cur, nxt):
        if nxt is not None:
            k = k_ref[0, nxt[0]]
        if cur is not None:
            vt = jnp.concatenate([vt_ref[0, cur[0]], ones], axis=0)
        for u, (j, c) in enumerate(units):
            cols = slice(c, c + ATTN_QW)
            if nxt is not None:
                st = jnp.dot(k, qt_ref[j, 0, :, cols], preferred_element_type=F32)
                nxt[1][u] = st
                nxt[2][u] = jnp.max(st, axis=0, keepdims=True)
            if cur is not None:
                st = cur[1][u]
                m_old = m_ref[j, :, cols]
                m_new = jnp.maximum(m_old, cur[2][u])
                alpha = jnp.exp2(m_old - m_new)
                p = jnp.exp2(st - m_new)
                acc_ref[j, :, cols] = (alpha * acc_ref[j, :, cols]
                                       + jnp.dot(vt, p.astype(BF16), preferred_element_type=F32))
                m_ref[j, :, cols] = m_new

    stage(None, (0, s0_ref, x0_ref))

    def body(i, carry):
        kb = 2 * i
        stage((kb, s0_ref, x0_ref), (kb + 1, s1_ref, x1_ref))
        stage((kb + 1, s1_ref, x1_ref), (kb + 2, s0_ref, x0_ref))
        return carry

    lax.fori_loop(0, nkb // 2 - 1 + jnp.minimum(pl.program_id(2), 0), body, 0)
    stage((nkb - 2, s0_ref, x0_ref), (nkb - 1, s1_ref, x1_ref))
    stage((nkb - 1, s1_ref, x1_ref), None)

    def head_out(j):
        return acc_ref[j, 0:HEAD_DIM, :] / acc_ref[j, HEAD_DIM:HEAD_DIM + 1, :]

    for jj in range(A_GROUP // 2):
        o2 = jnp.concatenate([head_out(2 * jj), head_out(2 * jj + 1)], axis=0)
        o_ref[:, jj * LANES:(jj + 1) * LANES] = o2.T.astype(o_ref.dtype)


def _attn(qt, k, vt, T):
    _, nblk, _, blk = qt.shape
    N = nblk * blk
    B = N // T
    nq = T // blk
    assert nq % 2 == 0, "attention pipelines key blocks in pairs"
    nu = A_GROUP * (blk // ATTN_QW)
    return pl.pallas_call(
        _attn_kernel,
        grid=(B, A_KV_HEADS, nq),
        in_specs=[
            pl.BlockSpec((A_GROUP, 1, HEAD_DIM, blk), lambda b, g, qi: (g, b * nq + qi, 0, 0)),
            pl.BlockSpec((1, nq, blk, HEAD_DIM), lambda b, g, qi: (g, b, 0, 0)),
            pl.BlockSpec((1, nq, HEAD_DIM, blk), lambda b, g, qi: (g, b, 0, 0)),
        ],
        out_specs=pl.BlockSpec((blk, A_GROUP * HEAD_DIM), lambda b, g, qi: (b * nq + qi, g)),
        out_shape=jax.ShapeDtypeStruct((N, A_WIDTH), BF16),
        scratch_shapes=[pltpu.VMEM((A_GROUP, 1, blk), F32),
                        pltpu.VMEM((A_GROUP, HEAD_DIM + ATTN_VPAD, blk), F32),
                        pltpu.VMEM((nu, blk, ATTN_QW), F32), pltpu.VMEM((nu, blk, ATTN_QW), F32),
                        pltpu.VMEM((nu, 1, ATTN_QW), F32), pltpu.VMEM((nu, 1, ATTN_QW), F32)],
        compiler_params=_cparams(("parallel", "parallel", "arbitrary")),
        name="attn",
    )(qt, k, vt)


def _ffn_kernel(xp_ref, xm_ref, xn_ref, mp_ref, mm_ref, mn_ref, ap_ref, am_ref, an_ref, wom_ref, woa_ref, n2_ref,
                wup_ref, cw_ref, cb_ref, wdn_ref, fw_ref, y_ref, lhs_ref, x1_ref, u0_ref, u1_ref, acc_ref,
                *, tiles_per_seq):
    i = pl.program_id(0)
    tm = xm_ref.shape[0]
    H = CONV_HALO
    t = i % tiles_per_seq
    xe = jnp.concatenate([xp_ref[...], xm_ref[...], xn_ref[...]], axis=0)
    me = jnp.concatenate([mp_ref[...], mm_ref[...], mn_ref[...]], axis=0)
    ae = jnp.concatenate([ap_ref[...], am_ref[...], an_ref[...]], axis=0)
    x1e = (xe + jnp.dot(me, wom_ref[...], preferred_element_type=F32)
           + jnp.dot(ae, woa_ref[...], preferred_element_type=F32))
    h2e = (x1e * lax.rsqrt(jnp.mean(x1e * x1e, axis=-1, keepdims=True) + EPS) * n2_ref[...]).astype(BF16)
    x1_ref[...] = x1e[H:H + tm]
    lhs_ref[0:H] = jnp.where(t == 0, jnp.zeros((H, D_MODEL), BF16), h2e[0:H])
    lhs_ref[H:H + tm] = h2e[H:H + tm]
    lhs_ref[H + tm:H + tm + H] = jnp.where(t == tiles_per_seq - 1, jnp.zeros((H, D_MODEL), BF16),
                                           h2e[H + tm:H + tm + H])
    acc_ref[...] = jnp.zeros_like(acc_ref)
    nchunk = D_FF // FF_CHUNK

    def cols(ref, c):
        off = c * FF_CHUNK if isinstance(c, int) else pl.multiple_of(c * FF_CHUNK, FF_CHUNK)
        return ref[:, pl.ds(off, FF_CHUNK)], ref[:, pl.ds(D_FF + off, FF_CHUNK)]

    def up(c, u_ref):
        lhs = lhs_ref[...]
        wa, wg = cols(wup_ref, c)
        u_ref[:, 0:FF_CHUNK] = jnp.dot(lhs, wa, preferred_element_type=F32)
        u_ref[:, FF_CHUNK:] = jnp.dot(lhs, wg, preferred_element_type=F32)

    def gate_down(c, u_ref):
        w = jnp.concatenate(cols(cw_ref, c), axis=1)
        b = jnp.concatenate(cols(cb_ref, c), axis=1)
        conv = (u_ref[pl.ds(H - 1, tm), :] * w[0:1] + u_ref[pl.ds(H, tm), :] * w[1:2]
                + u_ref[pl.ds(H + 1, tm), :] * w[2:3] + b)
        a = conv[:, 0:FF_CHUNK]
        gt = conv[:, FF_CHUNK:2 * FF_CHUNK]
        hg = 0.5 * gt
        ah = a * hg
        act = ah + ah * jnp.tanh(hg)
        row0 = c * FF_CHUNK if isinstance(c, int) else pl.multiple_of(c * FF_CHUNK, FF_CHUNK)
        acc_ref[...] += jnp.dot(act.astype(BF16), wdn_ref[pl.ds(row0, FF_CHUNK), :], preferred_element_type=F32)

    assert nchunk % 2 == 1
    up(0, u0_ref)

    def body(i, carry):
        c = 2 * i
        up(c + 1, u1_ref)
        gate_down(c, u0_ref)
        up(c + 2, u0_ref)
        gate_down(c + 1, u1_ref)
        return carry

    lax.fori_loop(0, nchunk // 2, body, 0)
    gate_down(nchunk - 1, u0_ref)
    x2 = x1_ref[...] + acc_ref[...]
    y_ref[...] = x2 * lax.rsqrt(jnp.mean(x2 * x2, axis=-1, keepdims=True) + EPS) * fw_ref[...]


def _ffn(x2, m_out, a_out, T, wom, woa, n2, wup, cw, cb, wdn, fw, tm):
    N = x2.shape[0]
    H = CONV_HALO
    tps = T // tm
    r = tm // H
    nblk = N // H
    row = lambda i: (i, 0)
    prev = lambda i: (jnp.maximum(i * r - 1, 0), 0)
    nxt = lambda i: (jnp.minimum((i + 1) * r, nblk - 1), 0)

    def halo3(width):
        return [pl.BlockSpec((H, width), prev), pl.BlockSpec((tm, width), row), pl.BlockSpec((H, width), nxt)]

    return pl.pallas_call(
        functools.partial(_ffn_kernel, tiles_per_seq=tps),
        grid=(N // tm,),
        in_specs=halo3(D_MODEL) + halo3(M_WIDTH) + halo3(A_WIDTH) + [
            _const_spec(wom.shape), _const_spec(woa.shape), _const_spec(n2.shape),
            _const_spec(wup.shape), _const_spec(cw.shape), _const_spec(cb.shape), _const_spec(wdn.shape),
            _const_spec(fw.shape),
        ],
        out_specs=pl.BlockSpec((tm, D_MODEL), row),
        out_shape=jax.ShapeDtypeStruct((N, D_MODEL), F32),
        scratch_shapes=[pltpu.VMEM((tm + 2 * H, D_MODEL), BF16), pltpu.VMEM((tm, D_MODEL), F32),
                        pltpu.VMEM((tm + 2 * H, 2 * FF_CHUNK), F32), pltpu.VMEM((tm + 2 * H, 2 * FF_CHUNK), F32),
                        pltpu.VMEM((tm, D_MODEL), F32)],
        compiler_params=_cparams(("parallel",)),
        name="ffn",
    )(x2, x2, x2, m_out, m_out, m_out, a_out, a_out, a_out, wom, woa, n2, wup, cw, cb, wdn, fw)


def _rope_tables(T, tm):
    nf = HEAD_DIM // 4
    inv = ROPE_THETA ** (-jnp.arange(nf, dtype=F32) / nf)
    zero = jnp.zeros((1, 2 * nf), F32)

    def parts(pos, row_part):
        ang = pos[:, None] * inv
        cos, sin = jnp.cos(ang), jnp.sin(ang)
        z = jnp.broadcast_to(zero, (pos.shape[0], 2 * nf))
        halves = lambda a, b: jnp.concatenate([a, b, z] if row_part else [z, a, b], axis=-1)
        tabs = (halves(cos, cos), halves(-sin, 0.0 * sin), halves(0.0 * sin, sin))
        return jnp.stack([jnp.concatenate([t, t], axis=-1) for t in tabs])

    rrow = parts(jnp.arange(T // GRID_W, dtype=F32), True)
    rrow = rrow.reshape(3, T // tm, tm // GRID_W, LANES).transpose(1, 0, 2, 3)
    rcol = parts(jnp.arange(GRID_W, dtype=F32), False)
    return rrow, jnp.tile(rcol, (1, tm // GRID_W, 1))


def _prep_weights(w_in, b_gates, mh_norm_w, q_norm_w, k_norm_w, w_out, norm1_w, norm2_w, w_up, conv_w, conv_b,
                  w_down, final_norm_w):
    gate0 = 4 * M_WIDTH
    wm = w_in[:, :gate0].astype(BF16)
    wg = jnp.pad(w_in[:, gate0:gate0 + N_GATES], ((0, 0), (0, LANES - N_GATES))).astype(BF16)
    wa = w_in[:, gate0 + N_GATES:].astype(BF16)
    bg = jnp.pad(b_gates, (0, LANES - N_GATES)).reshape(1, LANES)
    return dict(
        wm=wm, wg=wg, wa=wa, bg=bg,
        n1=norm1_w.reshape(1, D_MODEL), n2=norm2_w.reshape(1, D_MODEL), fw=final_norm_w.reshape(1, D_MODEL),
        qw2=jnp.tile(q_norm_w, 2).reshape(1, LANES), kw2=jnp.tile(k_norm_w, 2).reshape(1, LANES),
        nw=mh_norm_w.reshape(1, M_WIDTH),
        wom=w_out[:M_WIDTH].astype(BF16), woa=w_out[M_WIDTH:].astype(BF16),
        wup=w_up.astype(BF16), cw=conv_w, cb=conv_b.reshape(1, 2 * D_FF), wdn=w_down.astype(BF16),
    )


def _tiles(T):
    return min(512, T // 2)


def _trunk(x, p):
    B, T, _ = x.shape
    N = B * T
    tm = _tiles(T)
    x2 = x.reshape(N, D_MODEL)
    rrow, rcol = _rope_tables(T, tm)
    blk = np.arange(LANES) // HEAD_DIM
    ones_bd = jnp.asarray(blk[:, None] == blk[None, :], BF16)
    idx = np.arange(tm)
    tri = jnp.asarray((idx[None, :] <= idx[:, None])
                      & (idx[None, :] // MLSTM_CHUNK == idx[:, None] // MLSTM_CHUNK), BF16)

    mq, mk, mv, mo, cum, e, emax, ends, qt, k, vt = _inproj(
        x2, T, p["n1"], p["wm"], p["wg"], p["wa"], p["bg"], p["qw2"], p["kw2"], rrow, rcol, ones_bd, tri, tm)
    nwb = jnp.broadcast_to(p["nw"].reshape(M_HEADS, HEAD_DIM, 1), (M_HEADS, HEAD_DIM, MLSTM_CHUNK))
    s_f, m_f, s_b, m_b = _mlstm_state(mk, mv, e, ends, T)
    m_out = _mlstm_out(mq, mk, mv, mo, cum, e, emax, s_f, m_f, s_b, m_b, nwb)
    a_out = _attn(qt, k, vt, T)
    y = _ffn(x2, m_out, a_out, T, p["wom"], p["woa"], p["n2"], p["wup"], p["cw"], p["cb"], p["wdn"], p["fw"], tm)
    return y.reshape(B, T, D_MODEL)


def kernel(x_prompt, x_sample, w_in, b_gates, mh_norm_w, q_norm_w, k_norm_w, w_out, norm1_w, norm2_w, w_up,
           conv_w, conv_b, w_down, final_norm_w):
    assert w_in.shape[0] == 1, "single-layer trunk"
    p = _prep_weights(w_in[0], b_gates[0], mh_norm_w[0], q_norm_w[0], k_norm_w[0], w_out[0], norm1_w[0],
                      norm2_w[0], w_up[0], conv_w[0], conv_b[0], w_down[0], final_norm_w)
    return (_trunk(x_prompt, p), _trunk(x_sample, p))
```
